```python
import math
import numpy as np
import jax, jax.numpy as jnp
from jax import lax

D_MODEL = 4096
BATCH = 2
SEQ = 4096
DEPTH = 2

CTX_LEN = 256
GRID_W = 64
F32 = jnp.float32
EPS = 1e-6
N_MOD = 6
N_BRANCH = 4
MIX_W = D_MODEL // 4
LRU_BLOCKS = 16
LRU_BLOCK_W = MIX_W // LRU_BLOCKS
CONV_W = 4
LRU_C = 8.0
S5_GROUP = 16
S5_GROUPS = MIX_W // S5_GROUP
S5_STATE = 64
GLA_HEADS = 4
GLA_DK = MIX_W // 2 // GLA_HEADS
GLA_DV = MIX_W // GLA_HEADS
GLA_RANK = 16
GLA_TAU = 16.0
HGRN_HEADS = 8
HGRN_DK = MIX_W // HGRN_HEADS
HGRN_DV = MIX_W // HGRN_HEADS
CHUNK = 16
PEER_HEADS = 8
N_KEYS = 128
N_EXPERTS = N_KEYS * N_KEYS
PEER_DQ = 256
PEER_TOPK = 16
PEER_BLOCK = 128
IN_SPLITS = (
    MIX_W, MIX_W,
    MIX_W,
    GLA_HEADS * GLA_DK, GLA_HEADS * GLA_DK, GLA_HEADS * GLA_DV, MIX_W, GLA_RANK,
    2 * HGRN_HEADS * HGRN_DK, HGRN_HEADS * HGRN_DK, HGRN_HEADS * HGRN_DV, MIX_W,
    N_BRANCH * D_MODEL,
)
IN_COLS = sum(IN_SPLITS)

kernel_name = "hybrid_lru_s5_gla_hgrn2_peer_block"


def rms_norm(x, gain):
    xf = x.astype(F32)
    y = xf * lax.rsqrt(jnp.mean(xf * xf, axis=-1, keepdims=True) + EPS)
    return (y * gain.astype(F32)).astype(x.dtype)


def head_rms_norm(o, gain):
    o = o * lax.rsqrt(jnp.mean(o * o, axis=-1, keepdims=True) + EPS)
    return o.reshape(o.shape[0], o.shape[1], -1) * gain.astype(F32)


def modulate(h, shift, scale):
    return h * (1 + scale) + shift


def to_col_major(t, rows):
    b_, l_, ch = t.shape
    return t.reshape(b_, rows, GRID_W, ch).transpose(0, 2, 1, 3).reshape(b_, l_, ch)


def to_row_major(t, rows):
    b_, l_, ch = t.shape
    return t.reshape(b_, GRID_W, rows, ch).transpose(0, 2, 1, 3).reshape(b_, l_, ch)


def maybe_flip(t, rev):
    return jnp.flip(t, axis=1) if rev else t


def short_conv(x, w, b):
    L = x.shape[1]
    xp = jnp.pad(x, ((0, 0), (CONV_W // 2 - 1, CONV_W // 2), (0, 0)))
    out = xp[:, 0:L] * w[0]
    for j in range(1, CONV_W):
        out = out + xp[:, j:j + L] * w[j]
    return out + b


def linear_scan(a, b, h0, reverse):
    def combine(e1, e2):
        a1, b1 = e1
        a2, b2 = e2
        return a1 * a2, a2 * b1 + b2
    a_cum, h = lax.associative_scan(combine, (a, b), reverse=reverse, axis=1)
    if h0 is not None:
        h = h + a_cum * h0[:, None]
    last = h[:, 0] if reverse else h[:, -1]
    return h, last


def rglru_coeffs(xs, w_a, b_a, w_i, b_i, lam):
    B_, L, _ = xs.shape
    xb = xs.reshape(B_, L, LRU_BLOCKS, LRU_BLOCK_W)
    r = jax.nn.sigmoid(jnp.einsum('blhi,hij->blhj', xb, w_a.astype(F32)).reshape(B_, L, MIX_W) + b_a.astype(F32))
    gi = jax.nn.sigmoid(jnp.einsum('blhi,hij->blhj', xb, w_i.astype(F32)).reshape(B_, L, MIX_W) + b_i.astype(F32))
    log_a = -LRU_C * r * jax.nn.softplus(-lam.astype(F32))
    return jnp.exp(log_a), jnp.sqrt(-jnp.expm1(2.0 * log_a)) * gi * xs


def rglru_branch(ax_l, ay_l, ax_c, ay_c, conv_w, conv_b, w_a, b_a, w_i, b_i, lam):
    xl = short_conv(ax_l, conv_w, conv_b).astype(F32)
    xc = short_conv(ax_c, conv_w, conv_b).astype(F32)
    hl, hc = [], []
    for d in range(2):
        rev = d == 1
        a, u = rglru_coeffs(xc, w_a[d], b_a[d], w_i[d], b_i[d], lam[d])
        h, s = linear_scan(a, u, None, rev)
        hc.append(h)
        a, u = rglru_coeffs(xl, w_a[d], b_a[d], w_i[d], b_i[d], lam[d])
        h, _ = linear_scan(a, u, s, rev)
        hl.append(h)
    return (jax.nn.gelu(ay_l.astype(F32)) * (hl[0] + hl[1]),
            jax.nn.gelu(ay_c.astype(F32)) * (hc[0] + hc[1]))


def s5_discretise(a_re, a_im, log_dt, b_re, b_im):
    A = lax.complex(a_re.astype(F32), a_im.astype(F32))
    a_bar = jnp.exp(A * jnp.exp(log_dt.astype(F32))[:, None])
    b_bar = ((a_bar - 1.0) / A)[..., None] * lax.complex(b_re.astype(F32), b_im.astype(F32))
    return a_bar, b_bar


def s5_scan(u, a_bar, b_bar, c_mat, h0, rev):
    bu = jnp.einsum('gpc,blgc->blgp', b_bar, u)
    h, last = linear_scan(jnp.broadcast_to(a_bar, bu.shape), bu, h0, rev)
    return jnp.real(jnp.einsum('gcp,blgp->blgc', c_mat, h)), last


def s5_branch(u_l, u_c, rows, a_re, a_im, log_dt, b_re, b_im, c_re, c_im, d_skip, w_glu):
    def groups(u):
        return u.astype(F32).reshape(u.shape[0], u.shape[1], S5_GROUPS, S5_GROUP)
    ul = groups(to_col_major(u_l, rows))
    uc = groups(u_c)
    dd = d_skip.astype(F32).reshape(S5_GROUPS, S5_GROUP)
    yl, yc = ul * dd, uc * dd
    for d in range(2):
        rev = d == 1
        a_bar, b_bar = s5_discretise(a_re[d], a_im[d], log_dt[d], b_re[d], b_im[d])
        c_mat = lax.complex(c_re[d].astype(F32), c_im[d].astype(F32))
        y, s = s5_scan(uc, a_bar, b_bar, c_mat, None, rev)
        yc = yc + y
        y, _ = s5_scan(ul, a_bar, b_bar, c_mat, s, rev)
        yl = yl + y
    wg = w_glu.astype(F32)
    def glu(y):
        y = jax.nn.gelu(y.reshape(y.shape[0], y.shape[1], MIX_W))
        z = y @ wg
        return z[..., :MIX_W] * jax.nn.sigmoid(z[..., MIX_W:])
    return to_row_major(glu(yl), rows), glu(yc)


def gla_chunked(q, k, v, log_f, s0):
    B_, L, H, K = q.shape
    V = v.shape[-1]
    n = L // CHUNK
    q, k, log_f = (t.reshape(B_, n, CHUNK, H, K) for t in (q, k, log_f))
    v = v.reshape(B_, n, CHUNK, H, V)
    b = jnp.cumsum(log_f, axis=2)
    upto = jnp.tril(jnp.ones((CHUNK, CHUNK), dtype=bool))[:, :, None, None]
    rel = b[:, :, :, None] - b[:, :, None, :]
    decay = jnp.where(upto, jnp.exp(jnp.minimum(rel, 0.0)), 0.0)
    scores = jnp.einsum('bnthk,bnshk,bntshk->bnhts', q, k, decay)
    o = jnp.einsum('bnhts,bnshv->bnthv', scores, v)
    b_last = b[:, :, -1]
    q_in = q * jnp.exp(b)
    k_in = k * jnp.exp(b_last[:, :, None] - b)
    if s0 is None:
        s0 = jnp.zeros((B_, H, K, V), q.dtype)

    def step(s, xs):
        qc, kc, vc, dc = xs
        o_c = jnp.einsum('bchk,bhkv->bchv', qc, s)
        s = s * dc[..., None] + jnp.einsum('bchk,bchv->bhkv', kc, vc)
        return s, o_c

    xs = tuple(jnp.moveaxis(t, 1, 0) for t in (q_in, k_in, v, jnp.exp(b_last)))
    s_fin, o_inter = lax.scan(step, s0, xs)
    o = o + jnp.moveaxis(o_inter, 0, 1)
    return o.reshape(B_, L, H, V), s_fin


def chunked_bidir(q_l, v_l, kf_l, q_c, v_c, kf_c):
    outs_l, outs_c = [], []
    for d in range(2):
        rev = d == 1
        k_c, lf_c = kf_c[d]
        k_l, lf_l = kf_l[d]
        o_c, s_c = gla_chunked(maybe_flip(q_c, rev), maybe_flip(k_c, rev), maybe_flip(v_c, rev), maybe_flip(lf_c, rev), None)
        o_l, _ = gla_chunked(maybe_flip(q_l, rev), maybe_flip(k_l, rev), maybe_flip(v_l, rev), maybe_flip(lf_l, rev), s_c)
        outs_c.append(maybe_flip(o_c, rev))
        outs_l.append(maybe_flip(o_l, rev))
    return outs_l[0] + outs_l[1], outs_c[0] + outs_c[1]


def gla_inputs(q, k, v, za, w_alpha, b_alpha):
    B_, L, _ = q.shape
    qh = q.astype(F32).reshape(B_, L, GLA_HEADS, GLA_DK) * GLA_DK ** -0.5
    kh = k.astype(F32).reshape(B_, L, GLA_HEADS, GLA_DK)
    vh = v.astype(F32).reshape(B_, L, GLA_HEADS, GLA_DV)
    za = za.astype(F32)
    kf = []
    for d in range(2):
        la = jax.nn.log_sigmoid(za @ w_alpha[d].astype(F32) + b_alpha[d].astype(F32)) / GLA_TAU
        kf.append((kh, la.reshape(B_, L, GLA_HEADS, GLA_DK)))
    return qh, vh, kf


def hgrn_inputs(zf, q, i, lower_bound):
    B_, L, _ = q.shape
    qh = jax.nn.silu(q.astype(F32)).reshape(B_, L, HGRN_HEADS, HGRN_DK) * HGRN_DK ** -0.5
    ih = i.astype(F32).reshape(B_, L, HGRN_HEADS, HGRN_DV)
    zf = zf.astype(F32).reshape(B_, L, 2, HGRN_HEADS, HGRN_DK)
    lbh = lower_bound.reshape(HGRN_HEADS, HGRN_DK)
    kf = []
    for d in range(2):
        z = zf[:, :, d]
        f = lbh + (1.0 - lbh) * jax.nn.sigmoid(z)
        kf.append(((1.0 - lbh) * jax.nn.sigmoid(-z), jnp.log(f)))
    return qh, ih, kf


def merge_branches(branches, gate_logits, w_branch, w_out):
    B_, L = gate_logits.shape[:2]
    y = jnp.stack(branches, axis=2).astype(w_branch.dtype)
    proj = jnp.einsum('blkm,kmd->blkd', y, w_branch)
    g = jax.nn.sigmoid(gate_logits.reshape(B_, L, N_BRANCH, D_MODEL))
    return jnp.einsum('blkd,blkd->bld', g, proj) @ w_out


def token_mixer(h_l, h_c, rows, lower_bound, need_ctx, w_in, conv_w, conv_b, lw_a, lb_a, lw_i, lb_i, lam,
                a_re, a_im, log_dt, b_re, b_im, c_re, c_im, d_skip, w_glu,
                w_alpha, b_alpha, gla_gain, hgrn_gain, w_branch, w_out):
    idx = np.cumsum(IN_SPLITS)[:-1].tolist()
    (ax_l, ay_l, bu_l, cq_l, ck_l, cv_l, cg_l, ca_l, df_l, dq_l, di_l, dg_l, gt_l) = jnp.split(h_l @ w_in, idx, axis=-1)
    (ax_c, ay_c, bu_c, cq_c, ck_c, cv_c, cg_c, ca_c, df_c, dq_c, di_c, dg_c, gt_c) = jnp.split(h_c @ w_in, idx, axis=-1)
    ya_l, ya_c = rglru_branch(ax_l, ay_l, ax_c, ay_c, conv_w, conv_b, lw_a, lb_a, lw_i, lb_i, lam)
    yb_l, yb_c = s5_branch(bu_l, bu_c, rows, a_re, a_im, log_dt, b_re, b_im, c_re, c_im, d_skip, w_glu)
    o_l, o_c = chunked_bidir(*gla_inputs(cq_l, ck_l, cv_l, ca_l, w_alpha, b_alpha),
                             *gla_inputs(cq_c, ck_c, cv_c, ca_c, w_alpha, b_alpha))
    yc_l = head_rms_norm(o_l, gla_gain) * jax.nn.silu(cg_l.astype(F32))
    yc_c = head_rms_norm(o_c, gla_gain) * jax.nn.silu(cg_c.astype(F32))
    o_l, o_c = chunked_bidir(*hgrn_inputs(to_col_major(df_l, rows), to_col_major(dq_l, rows), to_col_major(di_l, rows), lower_bound),
                             *hgrn_inputs(df_c, dq_c, di_c, lower_bound))
    yd_l = to_row_major(head_rms_norm(o_l, hgrn_gain), rows) * jax.nn.silu(dg_l.astype(F32))
    yd_c = head_rms_norm(o_c, hgrn_gain) * jax.nn.silu(dg_c.astype(F32))
    m_l = merge_branches([ya_l, yb_l, yc_l, yd_l], gt_l, w_branch, w_out)
    m_c = merge_branches([ya_c, yb_c, yc_c, yd_c], gt_c, w_branch, w_out) if need_ctx else None
    return m_l, m_c


def peer_ffn(t, w_q, keys, u_tab, v_tab):
    T = t.shape[0]
    q = (t @ w_q).astype(F32).reshape(T, PEER_HEADS, 2, PEER_DQ // 2)
    s = jnp.einsum('thpk,hpnk->thpn', q, keys.astype(F32))
    top_s, top_i = lax.top_k(s, PEER_TOPK)
    cand = top_s[:, :, 0, :, None] + top_s[:, :, 1, None, :]
    best_s, best_p = lax.top_k(cand.reshape(T, PEER_HEADS, PEER_TOPK * PEER_TOPK), PEER_TOPK)
    i1 = jnp.take_along_axis(top_i[:, :, 0], best_p // PEER_TOPK, axis=-1)
    i2 = jnp.take_along_axis(top_i[:, :, 1], best_p % PEER_TOPK, axis=-1)
    ids = (i1 * N_KEYS + i2).reshape(T, PEER_HEADS * PEER_TOPK)
    w = jax.nn.softmax(best_s, axis=-1).reshape(T, PEER_HEADS * PEER_TOPK).astype(t.dtype)
    nb = T // PEER_BLOCK

    def block(args):
        tb, idb, wb = args
        u = jnp.take(u_tab, idb, axis=0)
        act = jax.nn.gelu(jnp.einsum('td,ted->te', tb, u))
        v = jnp.take(v_tab, idb, axis=0)
        return jnp.einsum('te,ted->td', wb * act, v)

    out = lax.map(block, (t.reshape(nb, PEER_BLOCK, D_MODEL), ids.reshape(nb, PEER_BLOCK, -1), w.reshape(nb, PEER_BLOCK, -1)))
    return out.reshape(T, D_MODEL)


def setup_inputs(seed: int = 0) -> dict:
    key = jax.random.key(seed)
    k = jax.random.split(key, 40)

    def nrm(i, shape, std):
        return std * jax.random.normal(k[i], shape, F32)

    def uni(i, shape, lo, hi):
        return jax.random.uniform(k[i], shape, F32, lo, hi)

    D = D_MODEL
    a0 = uni(14, (DEPTH, 2, MIX_W), 0.9, 0.999) ** (1.0 / LRU_C)
    return {
        "x": nrm(0, (BATCH, SEQ, D), 1.0),
        "c": nrm(1, (BATCH, D), 1.0),
        "ctx": nrm(2, (BATCH, CTX_LEN, D), 1.0),
        "c_ctx": nrm(3, (D,), 1.0),
        "w_ada": nrm(4, (DEPTH, D, N_MOD * D), 0.2 * D ** -0.5),
        "b_ada": nrm(5, (DEPTH, N_MOD * D), 0.02),
        "norm_gain": 1.0 + nrm(6, (DEPTH, 4, D), 0.02),
        "w_in": nrm(7, (DEPTH, D, IN_COLS), D ** -0.5),
        "lru_conv_w": nrm(8, (DEPTH, CONV_W, MIX_W), CONV_W ** -0.5),
        "lru_conv_b": nrm(9, (DEPTH, MIX_W), 0.02),
        "lru_w_a": nrm(10, (DEPTH, 2, LRU_BLOCKS, LRU_BLOCK_W, LRU_BLOCK_W), LRU_BLOCK_W ** -0.5),
        "lru_b_a": nrm(11, (DEPTH, 2, MIX_W), 0.02),
        "lru_w_i": nrm(12, (DEPTH, 2, LRU_BLOCKS, LRU_BLOCK_W, LRU_BLOCK_W), LRU_BLOCK_W ** -0.5),
        "lru_b_i": nrm(13, (DEPTH, 2, MIX_W), 0.02),
        "lru_lambda": jnp.log(a0) - jnp.log1p(-a0),
        "s5_a_re": -0.5 + nrm(15, (DEPTH, 2, S5_GROUPS, S5_STATE), 0.01),
        "s5_a_im": math.pi * jnp.arange(S5_STATE, dtype=F32) + nrm(16, (DEPTH, 2, S5_GROUPS, S5_STATE), 0.01),
        "s5_log_dt": uni(17, (DEPTH, 2, S5_GROUPS), math.log(1e-3), math.log(1e-1)),
        "s5_b_re": nrm(18, (DEPTH, 2, S5_GROUPS, S5_STATE, S5_GROUP), (2 * S5_GROUP) ** -0.5),
        "s5_b_im": nrm(19, (DEPTH, 2, S5_GROUPS, S5_STATE, S5_GROUP), (2 * S5_GROUP) ** -0.5),
        "s5_c_re": nrm(20, (DEPTH, 2, S5_GROUPS, S5_GROUP, S5_STATE), (2 * S5_STATE) ** -0.5),
        "s5_c_im": nrm(21, (DEPTH, 2, S5_GROUPS, S5_GROUP, S5_STATE), (2 * S5_STATE) ** -0.5),
        "s5_d": nrm(22, (DEPTH, MIX_W), 0.5),
        "s5_w_glu": nrm(23, (DEPTH, MIX_W, 2 * MIX_W), MIX_W ** -0.5),
        "gla_w_alpha": nrm(24, (DEPTH, 2, GLA_RANK, GLA_HEADS * GLA_DK), GLA_RANK ** -0.5),
        "gla_b_alpha": uni(25, (DEPTH, 2, GLA_HEADS * GLA_DK), -1.0, 4.0),
        "gla_norm": 1.0 + nrm(26, (DEPTH, GLA_HEADS * GLA_DV), 0.02),
        "hgrn_lb_logits": nrm(27, (DEPTH, HGRN_HEADS * HGRN_DK), 1.0),
        "hgrn_norm": 1.0 + nrm(28, (DEPTH, HGRN_HEADS * HGRN_DV), 0.02),
        "w_branch": nrm(29, (DEPTH, N_BRANCH, MIX_W, D), MIX_W ** -0.5),
        "w_out": nrm(30, (DEPTH, D, D), D ** -0.5),
        "peer_w_q": nrm(31, (DEPTH, D, PEER_HEADS * PEER_DQ), D ** -0.5),
        "peer_keys": nrm(32, (DEPTH, PEER_HEADS, 2, N_KEYS, PEER_DQ // 2), (PEER_DQ // 2) ** -0.5),
        "peer_u": nrm(33, (DEPTH, N_EXPERTS, D), D ** -0.5),
        "peer_v": nrm(34, (DEPTH, N_EXPERTS, D), PEER_HEADS ** -0.5),
    }


def reference(x, c, ctx, c_ctx, w_ada, b_ada, norm_gain, w_in,
              lru_conv_w, lru_conv_b, lru_w_a, lru_b_a, lru_w_i, lru_b_i, lru_lambda,
              s5_a_re, s5_a_im, s5_log_dt, s5_b_re, s5_b_im, s5_c_re, s5_c_im, s5_d, s5_w_glu,
              gla_w_alpha, gla_b_alpha, gla_norm, hgrn_lb_logits, hgrn_norm,
              w_branch, w_out, peer_w_q, peer_keys, peer_u, peer_v):
    B_, L, _ = x.shape
    rows = L // GRID_W
    n_lat = B_ * L
    p = jax.nn.softmax(hgrn_lb_logits.astype(F32), axis=0)
    lower_bounds = jnp.cumsum(p, axis=0) - p[0]
    sc = jax.nn.silu(c)
    scc = jax.nn.silu(c_ctx)
    x_l, x_c = x, ctx
    for l in range(DEPTH):
        last = l == DEPTH - 1
        mod_l = (sc @ w_ada[l] + b_ada[l]).reshape(B_, N_MOD, 1, D_MODEL)
        mod_c = (scc @ w_ada[l] + b_ada[l]).reshape(N_MOD, D_MODEL)
        gain = norm_gain[l]
        h_l = modulate(rms_norm(x_l, gain[0]), mod_l[:, 0], mod_l[:, 1])
        h_c = modulate(rms_norm(x_c, gain[0]), mod_c[0], mod_c[1])
        m_l, m_c = token_mixer(h_l, h_c, rows, lower_bounds[l], not last, w_in[l],
                               lru_conv_w[l], lru_conv_b[l], lru_w_a[l], lru_b_a[l], lru_w_i[l], lru_b_i[l], lru_lambda[l],
                               s5_a_re[l], s5_a_im[l], s5_log_dt[l], s5_b_re[l], s5_b_im[l], s5_c_re[l], s5_c_im[l], s5_d[l], s5_w_glu[l],
                               gla_w_alpha[l], gla_b_alpha[l], gla_norm[l], hgrn_norm[l], w_branch[l], w_out[l])
        x_l = x_l + mod_l[:, 2] * rms_norm(m_l, gain[1])
        f_l = modulate(rms_norm(x_l, gain[2]), mod_l[:, 3], mod_l[:, 4])
        tok = f_l.reshape(-1, D_MODEL)
        if not last:
            x_c = x_c + mod_c[2] * rms_norm(m_c, gain[1])
            f_c = modulate(rms_norm(x_c, gain[2]), mod_c[3], mod_c[4])
            tok = jnp.concatenate([tok, f_c.reshape(-1, D_MODEL)], axis=0)
        y = peer_ffn(tok, peer_w_q[l], peer_keys[l], peer_u[l], peer_v[l])
        x_l = x_l + mod_l[:, 5] * rms_norm(y[:n_lat].reshape(x_l.shape), gain[3])
        if not last:
            x_c = x_c + mod_c[5] * rms_norm(y[n_lat:].reshape(x_c.shape), gain[3])
    return x_l
```

```python
import functools
import math

import jax
import jax.numpy as jnp
from jax import lax
from jax.experimental import pallas as pl
from jax.experimental.pallas import tpu as pltpu

F32 = jnp.float32
BF16 = jnp.bfloat16
EPS = 1e-6
GRID_W = 64
N_MOD = 6
N_BRANCH = 4
LRU_BLOCKS = 16
LRU_C = 8.0
S5_GROUP = 16
S5_STATE = 64
S5_CHUNKS = 4
GLA_HEADS = 4
GLA_TAU = 16.0
HGRN_HEADS = 8
PEER_HEADS = 8
N_KEYS = 128
PEER_TOPK = 16
SUBLANES = 8
VMEM_LIMIT = 56 * 1024 * 1024

NT_DIMS = (((1,), (1,)), ((), ()))
TN_DIMS = (((0,), (0,)), ((), ()))


def _params(n_axes, vmem=VMEM_LIMIT):
    return pltpu.CompilerParams(dimension_semantics=("arbitrary",) * n_axes, vmem_limit_bytes=vmem)


def _tile(n, prefs):
    for p in prefs:
        if n % p == 0:
            return p
    raise ValueError(f"no tile for {n} in {prefs}")


def _rms(x):
    return x * lax.rsqrt(jnp.mean(x * x, axis=-1, keepdims=True) + EPS)


def _ada_body(c_ref, w_ref, b_ref, o_ref):
    c = c_ref[...]
    s = (c * jax.nn.sigmoid(c)).astype(BF16)
    o_ref[...] = jnp.dot(s, w_ref[...].astype(BF16), preferred_element_type=F32) + b_ref[...]


def _adaln(cvec, w_ada, b_ada3, l):
    rows, d = cvec.shape
    n = w_ada.shape[2]
    tn = 512
    return pl.pallas_call(
        _ada_body,
        grid=(n // tn,),
        in_specs=[
            pl.BlockSpec((rows, d), lambda j: (0, 0)),
            pl.BlockSpec((None, d, tn), lambda j: (l, 0, j)),
            pl.BlockSpec((None, 1, tn), lambda j: (l, 0, j)),
        ],
        out_specs=pl.BlockSpec((rows, tn), lambda j: (0, j)),
        out_shape=jax.ShapeDtypeStruct((rows, n), F32),
        compiler_params=_params(1),
        name="adaln",
    )(cvec, w_ada, b_ada3)


def _normmod_body(x_ref, g_ref, mod_ref, o_ref, *, gi, shift_i, scale_i):
    y = _rms(x_ref[...]) * g_ref[gi:gi + 1, :]
    o_ref[...] = (y * (1.0 + mod_ref[scale_i:scale_i + 1, :]) + mod_ref[shift_i:shift_i + 1, :]).astype(o_ref.dtype)


def _mod_row(i, tm, seq, nb):
    return jnp.minimum((i * tm) // seq, nb)


def _normmod(x, gain, mod, l, seq, nb, tm):
    t, d = x.shape
    return pl.pallas_call(
        functools.partial(_normmod_body, gi=0, shift_i=0, scale_i=1),
        grid=(t // tm,),
        in_specs=[
            pl.BlockSpec((tm, d), lambda i: (i, 0)),
            pl.BlockSpec((None, 4, d), lambda i: (l, 0, 0)),
            pl.BlockSpec((None, N_MOD, d), lambda i: (_mod_row(i, tm, seq, nb), 0, 0)),
        ],
        out_specs=pl.BlockSpec((tm, d), lambda i: (i, 0)),
        out_shape=jax.ShapeDtypeStruct((t, d), BF16),
        compiler_params=_params(1),
        name="normmod",
    )(x, gain, mod)


def _mm_body(a_ref, w_ref, o_ref, wb_ref):
    @pl.when(pl.program_id(1) == 0)
    def _():
        wb_ref[...] = w_ref[...].astype(BF16)

    o_ref[...] = jnp.dot(a_ref[...], wb_ref[...], preferred_element_type=F32).astype(o_ref.dtype)


def _mm(a, w, l, col_off, ncols, out_dtype, m, tm, name):
    k = a.shape[1]
    tn = _tile(math.gcd(ncols, col_off) if col_off else ncols, (512, 256, 128))
    off = col_off // tn
    if w.ndim == 3:
        w_spec = pl.BlockSpec((None, k, tn), lambda j, i: (l, 0, j + off))
    else:
        w_spec = pl.BlockSpec((k, tn), lambda j, i: (0, j + off))
    return pl.pallas_call(
        _mm_body,
        grid=(ncols // tn, m // tm),
        in_specs=[pl.BlockSpec((tm, k), lambda j, i: (i, 0)), w_spec],
        out_specs=pl.BlockSpec((tm, tn), lambda j, i: (i, j)),
        out_shape=jax.ShapeDtypeStruct((m, ncols), out_dtype),
        scratch_shapes=[pltpu.VMEM((k, tn), BF16)],
        compiler_params=_params(2),
        name=name,
    )(a, w)


def _seq_block(b, i, rev, nb, seq, ctx, tt):
    nctx, nlat = ctx // tt, seq // tt
    ic = (nctx - 1 - i) if rev else i
    il = (nlat - 1 - (i - nctx)) if rev else (i - nctx)
    return jnp.where(i < nctx, (nb * seq) // tt + b * nctx + ic, b * nlat + il)


def _seq_spec(cols, col_blk, rev, nb, seq, ctx, tt):
    return pl.BlockSpec((tt, cols), lambda b, i: (_seq_block(b, i, rev, nb, seq, ctx, tt), col_blk))


def _halo_spec(cols, col_blk, rev, side, nb, seq, ctx, tt, total):
    per = tt // SUBLANES
    last = total // SUBLANES - 1

    def index(b, i):
        blk = _seq_block(b, i, rev, nb, seq, ctx, tt) * per
        blk = blk - 1 if side < 0 else blk + per
        return jnp.clip(blk, 0, last), col_blk

    return pl.BlockSpec((SUBLANES, cols), index)


def _stream_pos(i, rev, seq, ctx, tt):
    nctx, nlat = ctx // tt, seq // tt
    is_ctx = i < nctx
    ii = jnp.where(is_ctx, (nctx - 1 - i) if rev else i, (nlat - 1 - (i - nctx)) if rev else (i - nctx))
    n = jnp.where(is_ctx, nctx, nlat)
    return ii == 0, ii == n - 1


def _scan_tile_real(a, u, carry, rev):
    row = lax.broadcasted_iota(jnp.int32, a.shape, 0)
    for s in (1, 2, 4):
        if rev:
            a_sh, u_sh = pltpu.roll(a, SUBLANES - s, 0), pltpu.roll(u, SUBLANES - s, 0)
            ok = row < SUBLANES - s
        else:
            a_sh, u_sh = pltpu.roll(a, s, 0), pltpu.roll(u, s, 0)
            ok = row >= s
        u = jnp.where(ok, a * u_sh + u, u)
        a = jnp.where(ok, a * a_sh, a)
    h = u + a * carry
    return h, (h[0:1] if rev else h[SUBLANES - 1:SUBLANES])


def _lru_body(xf_ref, pf_ref, nf_ref, xb_ref, pb_ref, nb_ref, cw_ref, cb_ref, wd_ref, bd_ref, sp_ref,
              hf_ref, hb_ref, a_s, u_s, carry_s, *, seq, ctx, tt, c):
    i = pl.program_id(1)

    @pl.when(i == 0)
    def _():
        carry_s[...] = jnp.zeros_like(carry_s)

    row = lax.broadcasted_iota(jnp.int32, (tt, c), 0)
    for d, (x_ref, p_ref, n_ref) in enumerate(((xf_ref, pf_ref, nf_ref), (xb_ref, pb_ref, nb_ref))):
        first, last = _stream_pos(i, d == 1, seq, ctx, tt)
        x = x_ref[...]
        prev = jnp.where(first, 0.0, p_ref[SUBLANES - 1:SUBLANES, :])
        nx1 = jnp.where(last, 0.0, n_ref[0:1, :])
        nx2 = jnp.where(last, 0.0, n_ref[1:2, :])
        xm1 = jnp.where(row == 0, prev, pltpu.roll(x, 1, 0))
        xp1 = jnp.where(row == tt - 1, nx1, pltpu.roll(x, tt - 1, 0))
        xp2 = jnp.where(row == tt - 1, nx2, jnp.where(row == tt - 2, nx1, pltpu.roll(x, tt - 2, 0)))
        xl = xm1 * cw_ref[0:1, :] + x * cw_ref[1:2, :] + xp1 * cw_ref[2:3, :] + xp2 * cw_ref[3:4, :] + cb_ref[...]
        z = jnp.dot(xl.astype(BF16), wd_ref[d], preferred_element_type=F32) + bd_ref[d]
        r = jax.nn.sigmoid(z[:, :c])
        gi = jax.nn.sigmoid(z[:, c:])
        log_a = -LRU_C * r * sp_ref[d]
        a = jnp.exp(log_a)
        a_s[d] = a
        u_s[d] = jnp.sqrt(-jnp.tanh(log_a) * (a * a + 1.0)) * gi * xl

    ntile = tt // SUBLANES

    def step(k, carry):
        cf, cb = carry
        rows_f = pl.ds(pl.multiple_of(k * SUBLANES, SUBLANES), SUBLANES)
        rows_b = pl.ds(pl.multiple_of((ntile - 1 - k) * SUBLANES, SUBLANES), SUBLANES)
        h, cf = _scan_tile_real(a_s[0, rows_f, :], u_s[0, rows_f, :], cf, False)
        hf_ref[rows_f, :] = h
        h, cb = _scan_tile_real(a_s[1, rows_b, :], u_s[1, rows_b, :], cb, True)
        hb_ref[rows_b, :] = h
        return cf, cb

    cf, cb = lax.fori_loop(0, ntile, step, (carry_s[0, 0:1, :], carry_s[1, 0:1, :]))
    carry_s[0, 0:1, :] = cf
    carry_s[1, 0:1, :] = cb


def _lru(pa, cw, cb, wd, bd, sp, nb, seq, ctx, tt):
    t = pa.shape[0]
    c = cw.shape[1]
    geo = (nb, seq, ctx, tt)
    full = lambda shape: pl.BlockSpec(shape, lambda b, i: (0,) * len(shape))
    return pl.pallas_call(
        functools.partial(_lru_body, seq=seq, ctx=ctx, tt=tt, c=c),
        grid=(nb, (seq + ctx) // tt),
        in_specs=[
            _seq_spec(c, 0, False, *geo), _halo_spec(c, 0, False, -1, *geo, t), _halo_spec(c, 0, False, 1, *geo, t),
            _seq_spec(c, 0, True, *geo), _halo_spec(c, 0, True, -1, *geo, t), _halo_spec(c, 0, True, 1, *geo, t),
            full(cw.shape), full(cb.shape), full(wd.shape), full(bd.shape), full(sp.shape),
        ],
        out_specs=[_seq_spec(c, 0, False, *geo), _seq_spec(c, 0, True, *geo)],
        out_shape=[jax.ShapeDtypeStruct((t, c), F32)] * 2,
        scratch_shapes=[pltpu.VMEM((2, tt, c), F32), pltpu.VMEM((2, tt, c), F32), pltpu.VMEM((2, SUBLANES, c), F32)],
        compiler_params=_params(2),
        name="rglru_scan",
    )(pa, pa, pa, pa, pa, pa, cw, cb, wd, bd, sp)


def _lru_out_body(ay_ref, hf_ref, hb_ref, o_ref):
    o_ref[...] = (jax.nn.gelu(ay_ref[...]) * (hf_ref[...] + hb_ref[...])).astype(o_ref.dtype)


def _lru_out(pa, hf, hb, tm):
    t, c = hf.shape
    return pl.pallas_call(
        _lru_out_body,
        grid=(t // tm,),
        in_specs=[pl.BlockSpec((tm, c), lambda i: (i, 1)), pl.BlockSpec((tm, c), lambda i: (i, 0)),
                  pl.BlockSpec((tm, c), lambda i: (i, 0))],
        out_specs=pl.BlockSpec((tm, c), lambda i: (i, 0)),
        out_shape=jax.ShapeDtypeStruct((t, c), BF16),
        compiler_params=_params(1),
        name="rglru_out",
    )(pa, hf, hb)


def _scan_tile_cplx(xr, xi, dbl_ref, pw_ref, d, lanes, cr, ci, rev):
    row = lax.broadcasted_iota(jnp.int32, xr.shape, 0)
    for n, s in enumerate((1, 2, 4)):
        ar = dbl_ref[d, 0, n:n + 1, lanes]
        ai = dbl_ref[d, 1, n:n + 1, lanes]
        if rev:
            sr, si = pltpu.roll(xr, SUBLANES - s, 0), pltpu.roll(xi, SUBLANES - s, 0)
            ok = row < SUBLANES - s
        else:
            sr, si = pltpu.roll(xr, s, 0), pltpu.roll(xi, s, 0)
            ok = row >= s
        sr = jnp.where(ok, sr, 0.0)
        si = jnp.where(ok, si, 0.0)
        xr, xi = xr + ar * sr - ai * si, xi + ar * si + ai * sr
    pr = pw_ref[d, 0, :, lanes]
    pi = pw_ref[d, 1, :, lanes]
    hr = xr + pr * cr - pi * ci
    hi = xi + pr * ci + pi * cr
    sel = slice(0, 1) if rev else slice(SUBLANES - 1, SUBLANES)
    return hr, hi, hr[sel], hi[sel]


def _s5_body(uf_ref, ub_ref, bre_ref, bim_ref, cre_ref, cim_ref, dbl_ref, pw_ref, yf_ref, yb_ref,
             hr_s, hi_s, carry_s, *, tt, gw, sw):
    i = pl.program_id(1)

    @pl.when(i == 0)
    def _():
        carry_s[...] = jnp.zeros_like(carry_s)

    ntile = tt // SUBLANES
    for ch in range(S5_CHUNKS):
        cols = slice(ch * gw, (ch + 1) * gw)
        lanes = slice(ch * sw, (ch + 1) * sw)
        for d, u_ref in enumerate((uf_ref, ub_ref)):
            u = u_ref[:, cols].astype(BF16)
            hr_s[d] = jnp.dot(u, bre_ref[d, ch], preferred_element_type=F32)
            hi_s[d] = jnp.dot(u, bim_ref[d, ch], preferred_element_type=F32)

        def step(k, carry):
            crf, cif, crb, cib = carry
            rows_f = pl.ds(pl.multiple_of(k * SUBLANES, SUBLANES), SUBLANES)
            rows_b = pl.ds(pl.multiple_of((ntile - 1 - k) * SUBLANES, SUBLANES), SUBLANES)
            hr, hi, crf, cif = _scan_tile_cplx(hr_s[0, rows_f, :], hi_s[0, rows_f, :], dbl_ref, pw_ref, 0, lanes,
                                               crf, cif, False)
            hr_s[0, rows_f, :] = hr
            hi_s[0, rows_f, :] = hi
            hr, hi, crb, cib = _scan_tile_cplx(hr_s[1, rows_b, :], hi_s[1, rows_b, :], dbl_ref, pw_ref, 1, lanes,
                                               crb, cib, True)
            hr_s[1, rows_b, :] = hr
            hi_s[1, rows_b, :] = hi
            return crf, cif, crb, cib

        init = tuple(carry_s[n, 0:1, lanes] for n in range(4))
        fin = lax.fori_loop(0, ntile, step, init)
        for n in range(4):
            carry_s[n, 0:1, lanes] = fin[n]
        for d, y_ref in enumerate((yf_ref, yb_ref)):
            y_ref[:, cols] = (jnp.dot(hr_s[d].astype(BF16), cre_ref[d, ch], preferred_element_type=F32)
                              - jnp.dot(hi_s[d].astype(BF16), cim_ref[d, ch], preferred_element_type=F32))


def _s5(u, bre, bim, cre, cim, dbl, pw, nb, seq, ctx, tt):
    t, c = u.shape
    gw = c // S5_CHUNKS
    sw = bre.shape[-1]
    geo = (nb, seq, ctx, tt)
    full = lambda shape: pl.BlockSpec(shape, lambda b, i: (0,) * len(shape))
    return pl.pallas_call(
        functools.partial(_s5_body, tt=tt, gw=gw, sw=sw),
        grid=(nb, (seq + ctx) // tt),
        in_specs=[_seq_spec(c, 0, False, *geo), _seq_spec(c, 0, True, *geo),
                  full(bre.shape), full(bim.shape), full(cre.shape), full(cim.shape), full(dbl.shape), full(pw.shape)],
        out_specs=[_seq_spec(c, 0, False, *geo), _seq_spec(c, 0, True, *geo)],
        out_shape=[jax.ShapeDtypeStruct((t, c), F32)] * 2,
        scratch_shapes=[pltpu.VMEM((2, tt, sw), F32), pltpu.VMEM((2, tt, sw), F32),
                        pltpu.VMEM((4, SUBLANES, sw * S5_CHUNKS), F32)],
        compiler_params=_params(2),
        name="s5_scan",
    )(u, u, bre, bim, cre, cim, dbl, pw)


def _s5_glu_body(u_ref, yf_ref, yb_ref, d_ref, w_ref, o_ref, wb_ref, *, c):
    @pl.when(pl.program_id(0) == 0)
    def _():
        wb_ref[...] = w_ref[...].astype(BF16)

    y = jax.nn.gelu(u_ref[...] * d_ref[...] + yf_ref[...] + yb_ref[...])
    z = jnp.dot(y.astype(BF16), wb_ref[...], preferred_element_type=F32)
    o_ref[...] = (z[:, :c] * jax.nn.sigmoid(z[:, c:])).astype(o_ref.dtype)


def _s5_glu(u, yf, yb, dskip, w_glu, l, tm):
    t, c = u.shape
    return pl.pallas_call(
        functools.partial(_s5_glu_body, c=c),
        grid=(t // tm,),
        in_specs=[pl.BlockSpec((tm, c), lambda i: (i, 0))] * 3
        + [pl.BlockSpec((None, 1, c), lambda i: (l, 0, 0)), pl.BlockSpec((None, c, 2 * c), lambda i: (l, 0, 0))],
        out_specs=pl.BlockSpec((tm, c), lambda i: (i, 0)),
        out_shape=jax.ShapeDtypeStruct((t, c), BF16),
        scratch_shapes=[pltpu.VMEM((c, 2 * c), BF16)],
        compiler_params=_params(1),
        name="s5_glu",
    )(u, yf, yb, dskip, w_glu)


def _chunk_cumsum(x, rev, ch):
    row = lax.broadcasted_iota(jnp.int32, x.shape, 0)
    s = 1
    while s < ch:
        if rev:
            x = x + jnp.where(row < ch - s, pltpu.roll(x, ch - s, 0), 0.0)
        else:
            x = x + jnp.where(row >= s, pltpu.roll(x, s, 0), 0.0)
        s *= 2
    return x


def _chunk_scan(q_s, k_s, v_s, lf_s, st_s, o_refs, *, tt, ch, heads, dk, dv):
    nchunk = tt // ch
    tri_r = lax.broadcasted_iota(jnp.int32, (ch, ch), 0)
    tri_c = lax.broadcasted_iota(jnp.int32, (ch, ch), 1)

    def step(n, _):
        for d in range(2):
            rev = d == 1
            cc = (nchunk - 1 - n) if rev else n
            rows = pl.ds(pl.multiple_of(cc * ch, ch), ch)
            b = _chunk_cumsum(lf_s[d, rows, :], rev, ch)
            piv = ch // 2 if rev else ch // 2 - 1
            end = 0 if rev else ch - 1
            m = b[piv:piv + 1, :]
            bl = b[end:end + 1, :]
            qm = q_s[d, rows, :] * jnp.exp(b - m)
            km = k_s[d, rows, :] * jnp.exp(m - b)
            qg = (qm * jnp.exp(m)).astype(BF16)
            kg = (km * jnp.exp(bl - m)).astype(BF16)
            dec = jnp.exp(bl)
            qm = qm.astype(BF16)
            km = km.astype(BF16)
            v = v_s[d, rows, :].astype(BF16)
            keep = (tri_r <= tri_c) if rev else (tri_r >= tri_c)
            for h in range(heads):
                ks = slice(h * dk, (h + 1) * dk)
                vs = slice(h * dv, (h + 1) * dv)
                sc = lax.dot_general(qm[:, ks], km[:, ks], NT_DIMS, preferred_element_type=F32)
                sc = jnp.where(keep, sc, 0.0).astype(BF16)
                st = st_s[d, h]
                o = jnp.dot(sc, v[:, vs], preferred_element_type=F32)
                o = o + lax.dot_general(qg[:, ks], st.astype(BF16), NT_DIMS, preferred_element_type=F32)
                st_s[d, h] = st * dec[:, ks] + lax.dot_general(v[:, vs], kg[:, ks], TN_DIMS,
                                                               preferred_element_type=F32)
                o_refs[d][rows, vs] = o
        return 0

    lax.fori_loop(0, nchunk, step, 0)


def _gla_body(qf_ref, zf_ref, qb_ref, zb_ref, wa_ref, ba_ref, of_ref, ob_ref, q_s, k_s, v_s, lf_s, st_s,
              *, tt, ch, heads, dk, dv):
    @pl.when(pl.program_id(1) == 0)
    def _():
        st_s[...] = jnp.zeros_like(st_s)

    hk = heads * dk
    for d, (x_ref, z_ref) in enumerate(((qf_ref, zf_ref), (qb_ref, zb_ref))):
        q_s[d] = x_ref[:, :hk] * dk ** -0.5
        k_s[d] = x_ref[:, hk:2 * hk]
        v_s[d] = x_ref[:, 2 * hk:]
        la = jnp.dot(z_ref[...].astype(BF16), wa_ref[d], preferred_element_type=F32) + ba_ref[d]
        lf_s[d] = jax.nn.log_sigmoid(la) / GLA_TAU
    _chunk_scan(q_s, k_s, v_s, lf_s, st_s, (of_ref, ob_ref), tt=tt, ch=ch, heads=heads, dk=dk, dv=dv)


def _hgrn_body(xf_ref, xb_ref, lb_ref, of_ref, ob_ref, q_s, k_s, v_s, lf_s, st_s, *, tt, ch, heads, dk, dv):
    @pl.when(pl.program_id(1) == 0)
    def _():
        st_s[...] = jnp.zeros_like(st_s)

    hk = heads * dk
    lb = lb_ref[...]
    for d, x_ref in enumerate((xf_ref, xb_ref)):
        z = x_ref[:, d * hk:(d + 1) * hk]
        q_s[d] = jax.nn.silu(x_ref[:, 2 * hk:3 * hk]) * dk ** -0.5
        v_s[d] = x_ref[:, 3 * hk:]
        k_s[d] = (1.0 - lb) * jax.nn.sigmoid(-z)
        lf_s[d] = jnp.log(lb + (1.0 - lb) * jax.nn.sigmoid(z))
    _chunk_scan(q_s, k_s, v_s, lf_s, st_s, (of_ref, ob_ref), tt=tt, ch=ch, heads=heads, dk=dk, dv=dv)


def _chunk_scratch(tt, heads, dk, dv):
    return [pltpu.VMEM((2, tt, heads * dk), F32), pltpu.VMEM((2, tt, heads * dk), F32),
            pltpu.VMEM((2, tt, heads * dv), F32), pltpu.VMEM((2, tt, heads * dk), F32),
            pltpu.VMEM((2, heads, dv, dk), F32)]


def _gla(pc, za, wa, ba, nb, seq, ctx, tt, ch, heads, dk, dv):
    t = pc.shape[0]
    geo = (nb, seq, ctx, tt)
    wide = 2 * heads * dk + heads * dv
    full = lambda shape: pl.BlockSpec(shape, lambda b, i: (0,) * len(shape))
    return pl.pallas_call(
        functools.partial(_gla_body, tt=tt, ch=ch, heads=heads, dk=dk, dv=dv),
        grid=(nb, (seq + ctx) // tt),
        in_specs=[_seq_spec(wide, 0, False, *geo), _seq_spec(za.shape[1], 0, False, *geo),
                  _seq_spec(wide, 0, True, *geo), _seq_spec(za.shape[1], 0, True, *geo),
                  full(wa.shape), full(ba.shape)],
        out_specs=[_seq_spec(heads * dv, 0, False, *geo), _seq_spec(heads * dv, 0, True, *geo)],
        out_shape=[jax.ShapeDtypeStruct((t, heads * dv), F32)] * 2,
        scratch_shapes=_chunk_scratch(tt, heads, dk, dv),
        compiler_params=_params(2),
        name="gla_scan",
    )(pc, za, pc, za, wa, ba)


def _hgrn(pd, lb, nb, seq, ctx, tt, ch, heads, dk, dv):
    t = pd.shape[0]
    geo = (nb, seq, ctx, tt)
    wide = 3 * heads * dk + heads * dv
    return pl.pallas_call(
        functools.partial(_hgrn_body, tt=tt, ch=ch, heads=heads, dk=dk, dv=dv),
        grid=(nb, (seq + ctx) // tt),
        in_specs=[_seq_spec(wide, 0, False, *geo), _seq_spec(wide, 0, True, *geo),
                  pl.BlockSpec(lb.shape, lambda b, i: (0, 0))],
        out_specs=[_seq_spec(heads * dv, 0, False, *geo), _seq_spec(heads * dv, 0, True, *geo)],
        out_shape=[jax.ShapeDtypeStruct((t, heads * dv), F32)] * 2,
        scratch_shapes=_chunk_scratch(tt, heads, dk, dv),
        compiler_params=_params(2),
        name="hgrn_scan",
    )(pd, pd, lb)


def _headnorm_body(of_ref, ob_ref, g_ref, n_ref, o_ref, *, heads, dv):
    o = of_ref[...] + ob_ref[...]
    parts = [_rms(o[:, h * dv:(h + 1) * dv]) for h in range(heads)]
    y = jnp.concatenate(parts, axis=-1) * n_ref[...]
    o_ref[...] = (y * jax.nn.silu(g_ref[...])).astype(o_ref.dtype)


def _headnorm(of, ob, gate_arr, gate_blk, gain, l, heads, tm):
    t, c = of.shape
    return pl.pallas_call(
        functools.partial(_headnorm_body, heads=heads, dv=c // heads),
        grid=(t // tm,),
        in_specs=[pl.BlockSpec((tm, c), lambda i: (i, 0)), pl.BlockSpec((tm, c), lambda i: (i, 0)),
                  pl.BlockSpec((tm, c), lambda i: (i, gate_blk)), pl.BlockSpec((None, 1, c), lambda i: (l, 0, 0))],
        out_specs=pl.BlockSpec((tm, c), lambda i: (i, 0)),
        out_shape=jax.ShapeDtypeStruct((t, c), BF16),
        compiler_params=_params(1),
        name="headnorm_gate",
    )(of, ob, gate_arr, gain)


def _merge_body(ya_ref, yb_ref, yc_ref, yd_ref, g0_ref, g1_ref, g2_ref, g3_ref, w_ref, o_ref, wb_ref):
    @pl.when(pl.program_id(1) == 0)
    def _():
        wb_ref[...] = w_ref[...].astype(BF16)

    acc = None
    for k, (y_ref, g_ref) in enumerate(((ya_ref, g0_ref), (yb_ref, g1_ref), (yc_ref, g2_ref), (yd_ref, g3_ref))):
        term = jax.nn.sigmoid(g_ref[...]) * jnp.dot(y_ref[...], wb_ref[k], preferred_element_type=F32)
        acc = term if acc is None else acc + term
    o_ref[...] = acc.astype(o_ref.dtype)


def _merge(ys, pg, w_branch, l, m, tm):
    c = ys[0].shape[1]
    d = w_branch.shape[3]
    tn = 512
    nj = d // tn
    y_spec = pl.BlockSpec((tm, c), lambda j, i: (i, 0))
    g_specs = [pl.BlockSpec((tm, tn), functools.partial(lambda j, i, k: (i, k * nj + j), k=k)) for k in range(N_BRANCH)]
    return pl.pallas_call(
        _merge_body,
        grid=(nj, m // tm),
        in_specs=[y_spec] * N_BRANCH + g_specs + [pl.BlockSpec((None, N_BRANCH, c, tn), lambda j, i: (l, 0, 0, j))],
        out_specs=pl.BlockSpec((tm, tn), lambda j, i: (i, j)),
        out_shape=jax.ShapeDtypeStruct((m, d), BF16),
        scratch_shapes=[pltpu.VMEM((N_BRANCH, c, tn), BF16)],
        compiler_params=_params(2),
        name="branch_merge",
    )(*ys, pg, pg, pg, pg, w_branch)


def _resid_body(x_ref, y_ref, g_ref, mod_ref, xo_ref, *f_refs, gate_i, gy, gf, shift_i, scale_i):
    xn = x_ref[...] + mod_ref[gate_i:gate_i + 1, :] * (_rms(y_ref[...]) * g_ref[gy:gy + 1, :])
    xo_ref[...] = xn
    if f_refs:
        f = _rms(xn) * g_ref[gf:gf + 1, :]
        f_refs[0][...] = (f * (1.0 + mod_ref[scale_i:scale_i + 1, :]) + mod_ref[shift_i:shift_i + 1, :]).astype(BF16)


def _resid(x, y, gain, mod, l, m, seq, nb, tm, gate_i, gy, with_f):
    d = x.shape[1]
    row = pl.BlockSpec((tm, d), lambda i: (i, 0))
    out_shape = [jax.ShapeDtypeStruct((m, d), F32)]
    out_specs = [row]
    if with_f:
        out_shape.append(jax.ShapeDtypeStruct((m, d), BF16))
        out_specs.append(row)
    return pl.pallas_call(
        functools.partial(_resid_body, gate_i=gate_i, gy=gy, gf=2, shift_i=3, scale_i=4),
        grid=(m // tm,),
        in_specs=[row, row, pl.BlockSpec((None, 4, d), lambda i: (l, 0, 0)),
                  pl.BlockSpec((None, N_MOD, d), lambda i: (_mod_row(i, tm, seq, nb), 0, 0))],
        out_specs=out_specs,
        out_shape=out_shape,
        compiler_params=_params(1),
        name="residual",
    )(x, y, gain, mod)


def _top_rows(x, n_rows, vals_ref):
    iota = lax.broadcasted_iota(jnp.int32, x.shape, 0)

    def step(r, x):
        mx = jnp.max(x, axis=0, keepdims=True)
        vals_ref[pl.ds(r, 1), :] = mx
        first = jnp.min(jnp.where(x == mx, iota, n_rows), axis=0, keepdims=True)
        return jnp.where(iota == first, -jnp.inf, x)

    lax.fori_loop(0, PEER_TOPK, step, x)


def _route_body(q_ref, keys_ref, s1_ref, e1_ref, s2_ref, e2_ref, tau_ref, va_s, vb_s, vc_s, *, dq):
    for h in range(PEER_HEADS):
        tops = []
        scores = []
        for p, v_s in enumerate((va_s, vb_s)):
            qh = q_ref[:, h * 2 * dq + p * dq:h * 2 * dq + (p + 1) * dq].astype(BF16)
            s = lax.dot_general(keys_ref[h, p].astype(BF16), qh, NT_DIMS, preferred_element_type=F32)
            _top_rows(s, N_KEYS, v_s)
            scores.append(s)
            tops.append(v_s[...])
        a, b = tops
        cand = jnp.concatenate([a[r:r + 1, :] + b for r in range(PEER_TOPK)], axis=0)
        _top_rows(cand, PEER_TOPK * PEER_TOPK, vc_s)
        best = vc_s[...]
        z = jnp.sum(jnp.exp(best - best[0:1, :]), axis=0, keepdims=True)
        s1_ref[h] = scores[0]
        s2_ref[h] = scores[1]
        e1_ref[h] = jnp.exp(scores[0] - a[0:1, :]) / z
        e2_ref[h] = jnp.exp(scores[1] - b[0:1, :])
        tau_ref[h:h + 1, :] = best[PEER_TOPK - 1:PEER_TOPK, :]


def _route(q, keys, l, tm):
    t = q.shape[0]
    dq = keys.shape[-1]
    big = jax.ShapeDtypeStruct((PEER_HEADS, N_KEYS, t), F32)
    big_spec = pl.BlockSpec((PEER_HEADS, N_KEYS, tm), lambda i: (0, 0, i))
    return pl.pallas_call(
        functools.partial(_route_body, dq=dq),
        grid=(t // tm,),
        in_specs=[pl.BlockSpec((tm, q.shape[1]), lambda i: (i, 0)),
                  pl.BlockSpec((None,) + keys.shape[1:], lambda i: (l, 0, 0, 0, 0))],
        out_specs=[big_spec] * 4 + [pl.BlockSpec((PEER_HEADS, tm), lambda i: (0, i))],
        out_shape=[big] * 4 + [jax.ShapeDtypeStruct((PEER_HEADS, t), F32)],
        scratch_shapes=[pltpu.VMEM((PEER_TOPK, tm), F32)] * 3,
        compiler_params=_params(1),
        name="peer_route",
    )(q, keys)


def _peer_body(ft_ref, u_ref, vt_ref, s1_ref, e1_ref, s2_ref, e2_ref, tau_ref, o_ref, *, te):
    j = pl.program_id(1)

    @pl.when(j == 0)
    def _():
        o_ref[...] = jnp.zeros_like(o_ref)

    act = jax.nn.gelu(jnp.dot(u_ref[...], ft_ref[...], preferred_element_type=F32))
    blocks = []
    for r in range(te // N_KEYS):
        i1 = j * (te // N_KEYS) + r
        w = None
        for h in range(PEER_HEADS):
            c = s2_ref[h] + s1_ref[h, pl.ds(i1, 1), :]
            term = jnp.where(c >= tau_ref[h:h + 1, :], e2_ref[h] * e1_ref[h, pl.ds(i1, 1), :], 0.0)
            w = term if w is None else w + term
        blocks.append((w * act[r * N_KEYS:(r + 1) * N_KEYS, :]).astype(BF16))
    g = jnp.concatenate(blocks, axis=0) if len(blocks) > 1 else blocks[0]
    o_ref[...] += jnp.dot(vt_ref[...], g, preferred_element_type=F32)


def _peer(ft, u, vt, s1, e1, s2, e2, tau, tm, te):
    d, t = ft.shape
    e = u.shape[0]
    big_spec = pl.BlockSpec((PEER_HEADS, N_KEYS, tm), lambda i, j: (0, 0, i))
    return pl.pallas_call(
        functools.partial(_peer_body, te=te),
        grid=(t // tm, e // te),
        in_specs=[pl.BlockSpec((d, tm), lambda i, j: (0, i)), pl.BlockSpec((te, d), lambda i, j: (j, 0)),
                  pl.BlockSpec((d, te), lambda i, j: (0, j)), big_spec, big_spec, big_spec, big_spec,
                  pl.BlockSpec((PEER_HEADS, tm), lambda i, j: (0, i))],
        out_specs=pl.BlockSpec((d, tm), lambda i, j: (0, i)),
        out_shape=jax.ShapeDtypeStruct((d, t), F32),
        compiler_params=_params(2),
        name="peer_experts",
    )(ft, u, vt, s1, e1, s2, e2, tau)


def _block_diag(w):
    n, r, c = w.shape
    eye = jnp.eye(n, dtype=w.dtype)
    return (w[:, :, None, :] * eye[:, None, :, None]).reshape(n * r, n * c)


def _lru_params(w_a, b_a, w_i, b_i, lam):
    wd = jnp.stack([jnp.concatenate([_block_diag(w_a[d]), _block_diag(w_i[d])], axis=1) for d in range(2)])
    bd = jnp.stack([jnp.concatenate([b_a[d], b_i[d]])[None, :] for d in range(2)])
    sp = jax.nn.softplus(-lam)[:, None, :]
    return wd.astype(BF16), bd, sp


def _s5_params(a_re, a_im, log_dt, b_re, b_im, c_re, c_im):
    g, p = a_re.shape[1:]
    gc = g // S5_CHUNKS
    outs = [[] for _ in range(6)]
    for d in range(2):
        big_a = lax.complex(a_re[d], a_im[d])
        dt = jnp.exp(log_dt[d])[:, None]
        a_bar = jnp.exp(big_a * dt)
        b_bar = ((a_bar - 1.0) / big_a)[..., None] * lax.complex(b_re[d], b_im[d])
        c_mat = lax.complex(c_re[d], c_im[d])
        bm = jnp.swapaxes(b_bar, 1, 2).reshape(S5_CHUNKS, gc, S5_GROUP, p)
        cm = jnp.swapaxes(c_mat, 1, 2).reshape(S5_CHUNKS, gc, p, S5_GROUP)
        bm = jnp.stack([_block_diag(bm[k]) for k in range(S5_CHUNKS)])
        cm = jnp.stack([_block_diag(cm[k]) for k in range(S5_CHUNKS)])
        pows = jnp.stack([jnp.exp(big_a * dt * float(n)) for n in range(1, SUBLANES + 1)]).reshape(SUBLANES, g * p)
        dbl = jnp.stack([pows[0], pows[1], pows[3]] + [pows[0]] * (SUBLANES - 3))
        pw = pows[::-1] if d == 1 else pows
        for lst, val in zip(outs, (jnp.real(bm), jnp.imag(bm), jnp.real(cm), jnp.imag(cm),
                                   jnp.stack([jnp.real(dbl), jnp.imag(dbl)]), jnp.stack([jnp.real(pw), jnp.imag(pw)]))):
            lst.append(val)
    bre, bim, cre, cim, dbl, pw = (jnp.stack(o) for o in outs)
    return bre.astype(BF16), bim.astype(BF16), cre.astype(BF16), cim.astype(BF16), dbl, pw


def _to_col_major(t, nb, seq):
    rows = seq // GRID_W
    lat = t[:nb * seq].reshape(nb, rows, GRID_W, -1).transpose(0, 2, 1, 3).reshape(nb * seq, -1)
    return jnp.concatenate([lat, t[nb * seq:]], axis=0)


def _to_row_major(t, nb, seq):
    rows = seq // GRID_W
    lat = t[:nb * seq].reshape(nb, GRID_W, rows, -1).transpose(0, 2, 1, 3).reshape(nb * seq, -1)
    return jnp.concatenate([lat, t[nb * seq:]], axis=0)


def kernel(x, c, ctx, c_ctx, w_ada, b_ada, norm_gain, w_in, lru_conv_w, lru_conv_b, lru_w_a, lru_b_a, lru_w_i,
           lru_b_i, lru_lambda, s5_a_re, s5_a_im, s5_log_dt, s5_b_re, s5_b_im, s5_c_re, s5_c_im, s5_d, s5_w_glu,
           gla_w_alpha, gla_b_alpha, gla_norm, hgrn_lb_logits, hgrn_norm, w_branch, w_out, peer_w_q, peer_keys,
           peer_u, peer_v):
    nb, seq, d = x.shape
    nctx = ctx.shape[1]
    depth = w_ada.shape[0]
    mix = d // 4
    n_lat = nb * seq
    n_tok = n_lat + nb * nctx
    assert nb + 1 <= SUBLANES and seq % GRID_W == 0
    gla_dk, gla_dv = mix // 2 // GLA_HEADS, mix // GLA_HEADS
    hg_dk = hg_dv = mix // HGRN_HEADS
    gla_rank = gla_w_alpha.shape[2]
    tm = _tile(math.gcd(seq, nb * nctx), (512, 256, 128))
    tr = min(tm, 256)
    tt =_tile(math.gcd(seq, nctx), (256, 128, 64, 32))
    ch = 16

    o_c = 3 * mix
    o_ca = o_c + 2 * GLA_HEADS * gla_dk + GLA_HEADS * gla_dv + mix
    o_d = o_ca + gla_rank
    o_dg = 2 * HGRN_HEADS * hg_dk + HGRN_HEADS * hg_dk + HGRN_HEADS * hg_dv
    o_gt = o_dg + mix
    za_w = 128

    p_lb = jax.nn.softmax(hgrn_lb_logits.astype(F32), axis=0)
    lower = jnp.cumsum(p_lb, axis=0) - p_lb[0]
    cvec = jnp.zeros((SUBLANES, d), F32).at[:nb].set(c).at[nb].set(c_ctx)
    b_ada3 = b_ada[:, None, :]

    xs = jnp.concatenate([x.reshape(n_lat, d), ctx.reshape(nb * nctx, d)], axis=0)
    for l in range(depth):
        last = l == depth - 1
        m_out = n_lat if last else n_tok
        mod = _adaln(cvec, w_ada, b_ada3, l).reshape(SUBLANES, N_MOD, d)
        h = _normmod(xs, norm_gain, mod, l, seq, nb, tr)

        w_lo = w_in
        w_hi = w_in[l, :, o_d:]
        mm_in = lambda w, ll, off, n, name: _mm(h, w, ll, off, n, F32, n_tok, tm, name)
        pa = mm_in(w_lo, l, 0, 2 * mix, "proj_lru")
        pb = mm_in(w_lo, l, 2 * mix, mix, "proj_s5")
        pc = mm_in(w_lo, l, o_c, o_ca - o_c - mix, "proj_gla")
        pcg = mm_in(w_lo, l, o_ca - mix, mix, "proj_gla_gate")
        za = mm_in(w_lo, l, o_ca, za_w, "proj_gla_rank")
        pd = mm_in(w_hi, None, 0, o_gt, "proj_hgrn")
        pg = mm_in(w_hi, None, o_gt, N_BRANCH * d, "proj_gate")

        wd, bd, sp = _lru_params(lru_w_a[l], lru_b_a[l], lru_w_i[l], lru_b_i[l], lru_lambda[l])
        hf, hb = _lru(pa, lru_conv_w[l], lru_conv_b[l][None, :], wd, bd, sp, nb, seq, nctx, tt)
        ya = _lru_out(pa, hf, hb, tm)

        s5p = _s5_params(s5_a_re[l], s5_a_im[l], s5_log_dt[l], s5_b_re[l], s5_b_im[l], s5_c_re[l], s5_c_im[l])
        ub = _to_col_major(pb, nb, seq)
        yf, ybk = _s5(ub, *s5p, nb, seq, nctx, tt)
        yb = _to_row_major(_s5_glu(ub, yf, ybk, s5_d[:, None, :], s5_w_glu, l, tm), nb, seq)

        wa = jnp.zeros((2, za_w, GLA_HEADS * gla_dk), F32).at[:, :gla_rank].set(gla_w_alpha[l]).astype(BF16)
        of, ob = _gla(pc, za, wa, gla_b_alpha[l][:, None, :], nb, seq, nctx, tt, ch, GLA_HEADS, gla_dk, gla_dv)
        yc = _headnorm(of, ob, pcg, 0, gla_norm[:, None, :], l, GLA_HEADS, tm)

        pdc = _to_col_major(pd, nb, seq)
        of, ob = _hgrn(pdc, lower[l][None, :], nb, seq, nctx, tt, ch, HGRN_HEADS, hg_dk, hg_dv)
        yd = _to_row_major(_headnorm(of, ob, pdc, o_dg // mix, hgrn_norm[:, None, :], l, HGRN_HEADS, tm), nb, seq)

        zm = _merge((ya, yb, yc, yd), pg, w_branch, l, m_out, tm)
        mo = _mm(zm, w_out, l, 0, d, F32, m_out, tm, "proj_out")
        xs, f = _resid(xs, mo, norm_gain, mod, l, m_out, seq, nb, tr, 2, 1, True)

        q = _mm(f, peer_w_q, l, 0, peer_w_q.shape[2], F32, m_out, tm, "peer_query")
        s1, e1, s2, e2, tau = _route(q, peer_keys, l, _tile(m_out, (256, 128)))
        yt = _peer(f.T, peer_u[l].astype(BF16), peer_v[l].T.astype(BF16), s1, e1, s2, e2, tau,
                   _tile(m_out, (512, 256, 128)), 256)
        xs = _resid(xs, yt.T, norm_gain, mod, l, m_out, seq, nb, tr, 5, 3, False)[0]
    return xs[:n_lat].reshape(nb, seq, d)
```

```python
import functools
import math

import jax
import jax.numpy as jnp
from jax import lax
from jax.experimental import pallas as pl
from jax.experimental.pallas import tpu as pltpu

F32 = jnp.float32
BF16 = jnp.bfloat16
EPS = 1e-6
GRID_W = 64
N_MOD = 6
N_BRANCH = 4
LRU_BLOCKS = 16
LRU_C = 8.0
S5_GROUP = 16
S5_STATE = 64
S5_CHUNKS = 4
GLA_HEADS = 4
GLA_TAU = 16.0
HGRN_HEADS = 8
PEER_HEADS = 8
N_KEYS = 128
PEER_TOPK = 16
SUBLANES = 8
LANES = 128
VMEM_LIMIT = 56 * 1024 * 1024

NT_DIMS = (((1,), (1,)), ((), ()))
TN_DIMS = (((0,), (0,)), ((), ()))


def _params(n_axes, vmem=VMEM_LIMIT):
    return pltpu.CompilerParams(dimension_semantics=("arbitrary",) * n_axes, vmem_limit_bytes=vmem)


def _tile(n, prefs):
    for p in prefs:
        if n % p == 0:
            return p
    raise ValueError(f"no tile for {n} in {prefs}")


def _rms(x):
    return x * lax.rsqrt(jnp.mean(x * x, axis=-1, keepdims=True) + EPS)


def _ada_body(c_ref, w_ref, b_ref, o_ref):
    c = c_ref[...]
    s = (c * jax.nn.sigmoid(c)).astype(BF16)
    o_ref[...] = jnp.dot(s, w_ref[...].astype(BF16), preferred_element_type=F32) + b_ref[...]


def _adaln(cvec, w_ada, b_ada3, l):
    rows, d = cvec.shape
    n = w_ada.shape[2]
    tn = 512
    return pl.pallas_call(
        _ada_body,
        grid=(n // tn,),
        in_specs=[
            pl.BlockSpec((rows, d), lambda j: (0, 0)),
            pl.BlockSpec((None, d, tn), lambda j: (l, 0, j)),
            pl.BlockSpec((None, 1, tn), lambda j: (l, 0, j)),
        ],
        out_specs=pl.BlockSpec((rows, tn), lambda j: (0, j)),
        out_shape=jax.ShapeDtypeStruct((rows, n), F32),
        compiler_params=_params(1),
        name="adaln",
    )(cvec, w_ada, b_ada3)


def _normmod_body(x_ref, g_ref, mod_ref, o_ref, *, gi, shift_i, scale_i):
    y = _rms(x_ref[...]) * g_ref[gi:gi + 1, :]
    o_ref[...] = (y * (1.0 + mod_ref[scale_i:scale_i + 1, :]) + mod_ref[shift_i:shift_i + 1, :]).astype(o_ref.dtype)


def _mod_row(i, tm, seq, nb):
    return jnp.minimum((i * tm) // seq, nb)


def _normmod(x, gain, mod, l, seq, nb, tm):
    t, d = x.shape
    return pl.pallas_call(
        functools.partial(_normmod_body, gi=0, shift_i=0, scale_i=1),
        grid=(t // tm,),
        in_specs=[
            pl.BlockSpec((tm, d), lambda i: (i, 0)),
            pl.BlockSpec((None, 4, d), lambda i: (l, 0, 0)),
            pl.BlockSpec((None, N_MOD, d), lambda i: (_mod_row(i, tm, seq, nb), 0, 0)),
        ],
        out_specs=pl.BlockSpec((tm, d), lambda i: (i, 0)),
        out_shape=jax.ShapeDtypeStruct((t, d), BF16),
        compiler_params=_params(1),
        name="normmod",
    )(x, gain, mod)


def _mm_body(a_ref, w_ref, o_ref, wb_ref):
    @pl.when(pl.program_id(1) == 0)
    def _():
        wb_ref[...] = w_ref[...].astype(BF16)

    o_ref[...] = jnp.dot(a_ref[...], wb_ref[...], preferred_element_type=F32).astype(o_ref.dtype)


def _mm(a, w, l, col_off, ncols, out_dtype, m, tm, name):
    k = a.shape[1]
    tn = _tile(math.gcd(ncols, col_off) if col_off else ncols, (512, 256, 128))
    off = col_off // tn
    if w.ndim == 3:
        w_spec = pl.BlockSpec((None, k, tn), lambda j, i: (l, 0, j + off))
    else:
        w_spec = pl.BlockSpec((k, tn), lambda j, i: (0, j + off))
    return pl.pallas_call(
        _mm_body,
        grid=(ncols // tn, m // tm),
        in_specs=[pl.BlockSpec((tm, k), lambda j, i: (i, 0)), w_spec],
        out_specs=pl.BlockSpec((tm, tn), lambda j, i: (i, j)),
        out_shape=jax.ShapeDtypeStruct((m, ncols), out_dtype),
        scratch_shapes=[pltpu.VMEM((k, tn), BF16)],
        compiler_params=_params(2),
        name=name,
    )(a, w)


def _mmt_body(a_ref, w_ref, *rest, shift):
    if shift:
        tail_ref, o_ref, wb_ref = rest
    else:
        o_ref, wb_ref = rest
    tn = wb_ref.shape[0]

    @pl.when(pl.program_id(1) == 0)
    def _():
        if shift:
            wb_ref[0:tn - shift, :] = w_ref[shift:tn, :].astype(BF16)
            wb_ref[tn - shift:tn, :] = tail_ref[...].astype(BF16)
        else:
            wb_ref[...] = w_ref[...].astype(BF16)

    o_ref[...] = lax.dot_general(a_ref[...], wb_ref[...], NT_DIMS, preferred_element_type=F32).astype(o_ref.dtype)


def _mmt(a, wt, l, row_off, ncols, shift, tn, m, tm, name):
    k = a.shape[1]
    base = row_off // tn
    in_specs = [pl.BlockSpec((tm, k), lambda j, i: (i, 0)), pl.BlockSpec((None, tn, k), lambda j, i: (l, base + j, 0))]
    args = [a, wt]
    if shift:
        per = tn // shift
        in_specs.append(pl.BlockSpec((None, shift, k), lambda j, i: (l, (base + j + 1) * per, 0)))
        args.append(wt)
    return pl.pallas_call(
        functools.partial(_mmt_body, shift=shift),
        grid=(ncols // tn, m // tm),
        in_specs=in_specs,
        out_specs=pl.BlockSpec((tm, tn), lambda j, i: (i, j)),
        out_shape=jax.ShapeDtypeStruct((m, ncols), F32),
        scratch_shapes=[pltpu.VMEM((tn, k), BF16)],
        compiler_params=_params(2),
        name=name,
    )(*args)


def _seq_block(b, i, rev, nb, seq, ctx, tt):
    nctx, nlat = ctx // tt, seq // tt
    ic = (nctx - 1 - i) if rev else i
    il = (nlat - 1 - (i - nctx)) if rev else (i - nctx)
    return jnp.where(i < nctx, (nb * seq) // tt + b * nctx + ic, b * nlat + il)


def _seq_spec(cols, col_blk, rev, nb, seq, ctx, tt):
    return pl.BlockSpec((tt, cols), lambda b, i: (_seq_block(b, i, rev, nb, seq, ctx, tt), col_blk))


def _halo_spec(cols, col_blk, rev, side, nb, seq, ctx, tt, total):
    per = tt // SUBLANES
    last = total // SUBLANES - 1

    def index(b, i):
        blk = _seq_block(b, i, rev, nb, seq, ctx, tt) * per
        blk = blk - 1 if side < 0 else blk + per
        return jnp.clip(blk, 0, last), col_blk

    return pl.BlockSpec((SUBLANES, cols), index)


def _stream_pos(i, rev, seq, ctx, tt):
    nctx, nlat = ctx // tt, seq // tt
    is_ctx = i < nctx
    ii = jnp.where(is_ctx, (nctx - 1 - i) if rev else i, (nlat - 1 - (i - nctx)) if rev else (i - nctx))
    n = jnp.where(is_ctx, nctx, nlat)
    return ii == 0, ii == n - 1


def _scan_tile_real(a, u, carry, rev):
    row = lax.broadcasted_iota(jnp.int32, a.shape, 0)
    for s in (1, 2, 4):
        if rev:
            a_sh, u_sh = pltpu.roll(a, SUBLANES - s, 0), pltpu.roll(u, SUBLANES - s, 0)
            ok = row < SUBLANES - s
        else:
            a_sh, u_sh = pltpu.roll(a, s, 0), pltpu.roll(u, s, 0)
            ok = row >= s
        u = jnp.where(ok, a * u_sh + u, u)
        a = jnp.where(ok, a * a_sh, a)
    h = u + a * carry
    return h, (h[0:1] if rev else h[SUBLANES - 1:SUBLANES])


def _lru_body(xf_ref, pf_ref, nf_ref, xb_ref, pb_ref, nb_ref, cw_ref, cb_ref, wd_ref, bd_ref, sp_ref,
              hf_ref, hb_ref, a_s, u_s, carry_s, *, seq, ctx, tt, c):
    i = pl.program_id(1)

    @pl.when(i == 0)
    def _():
        carry_s[...] = jnp.zeros_like(carry_s)

    row = lax.broadcasted_iota(jnp.int32, (tt, c), 0)
    for d, (x_ref, p_ref, n_ref) in enumerate(((xf_ref, pf_ref, nf_ref), (xb_ref, pb_ref, nb_ref))):
        first, last = _stream_pos(i, d == 1, seq, ctx, tt)
        x = x_ref[...]
        prev = jnp.where(first, 0.0, p_ref[SUBLANES - 1:SUBLANES, :])
        nx1 = jnp.where(last, 0.0, n_ref[0:1, :])
        nx2 = jnp.where(last, 0.0, n_ref[1:2, :])
        xm1 = jnp.where(row == 0, prev, pltpu.roll(x, 1, 0))
        xp1 = jnp.where(row == tt - 1, nx1, pltpu.roll(x, tt - 1, 0))
        xp2 = jnp.where(row == tt - 1, nx2, jnp.where(row == tt - 2, nx1, pltpu.roll(x, tt - 2, 0)))
        xl = xm1 * cw_ref[0:1, :] + x * cw_ref[1:2, :] + xp1 * cw_ref[2:3, :] + xp2 * cw_ref[3:4, :] + cb_ref[...]
        z = jnp.dot(xl.astype(BF16), wd_ref[d], preferred_element_type=F32) + bd_ref[d]
        r = jax.nn.sigmoid(z[:, :c])
        gi = jax.nn.sigmoid(z[:, c:])
        log_a = -LRU_C * r * sp_ref[d]
        a = jnp.exp(log_a)
        a_s[d] = a
        u_s[d] = jnp.sqrt(-jnp.tanh(log_a) * (a * a + 1.0)) * gi * xl

    ntile = tt // SUBLANES

    def step(k, carry):
        cf, cb = carry
        rows_f = pl.ds(pl.multiple_of(k * SUBLANES, SUBLANES), SUBLANES)
        rows_b = pl.ds(pl.multiple_of((ntile - 1 - k) * SUBLANES, SUBLANES), SUBLANES)
        h, cf = _scan_tile_real(a_s[0, rows_f, :], u_s[0, rows_f, :], cf, False)
        hf_ref[rows_f, :] = h
        h, cb = _scan_tile_real(a_s[1, rows_b, :], u_s[1, rows_b, :], cb, True)
        hb_ref[rows_b, :] = h
        return cf, cb

    cf, cb = lax.fori_loop(0, ntile, step, (carry_s[0, 0:1, :], carry_s[1, 0:1, :]))
    carry_s[0, 0:1, :] = cf
    carry_s[1, 0:1, :] = cb


def _lru(pa, cw, cb, wd, bd, sp, nb, seq, ctx, tt):
    t = pa.shape[0]
    c = cw.shape[1]
    geo = (nb, seq, ctx, tt)
    full = lambda shape: pl.BlockSpec(shape, lambda b, i: (0,) * len(shape))
    return pl.pallas_call(
        functools.partial(_lru_body, seq=seq, ctx=ctx, tt=tt, c=c),
        grid=(nb, (seq + ctx) // tt),
        in_specs=[
            _seq_spec(c, 0, False, *geo), _halo_spec(c, 0, False, -1, *geo, t), _halo_spec(c, 0, False, 1, *geo, t),
            _seq_spec(c, 0, True, *geo), _halo_spec(c, 0, True, -1, *geo, t), _halo_spec(c, 0, True, 1, *geo, t),
            full(cw.shape), full(cb.shape), full(wd.shape), full(bd.shape), full(sp.shape),
        ],
        out_specs=[_seq_spec(c, 0, False, *geo), _seq_spec(c, 0, True, *geo)],
        out_shape=[jax.ShapeDtypeStruct((t, c), F32)] * 2,
        scratch_shapes=[pltpu.VMEM((2, tt, c), F32), pltpu.VMEM((2, tt, c), F32), pltpu.VMEM((2, SUBLANES, c), F32)],
        compiler_params=_params(2),
        name="rglru_scan",
    )(pa, pa, pa, pa, pa, pa, cw, cb, wd, bd, sp)


def _lru_out_body(ay_ref, hf_ref, hb_ref, o_ref):
    o_ref[...] = (jax.nn.gelu(ay_ref[...]) * (hf_ref[...] + hb_ref[...])).astype(o_ref.dtype)


def _lru_out(pa, hf, hb, tm):
    t, c = hf.shape
    return pl.pallas_call(
        _lru_out_body,
        grid=(t // tm,),
        in_specs=[pl.BlockSpec((tm, c), lambda i: (i, 1)), pl.BlockSpec((tm, c), lambda i: (i, 0)),
                  pl.BlockSpec((tm, c), lambda i: (i, 0))],
        out_specs=pl.BlockSpec((tm, c), lambda i: (i, 0)),
        out_shape=jax.ShapeDtypeStruct((t, c), BF16),
        compiler_params=_params(1),
        name="rglru_out",
    )(pa, hf, hb)


def _scan_tile_cplx(xr, xi, dbl_ref, pw_ref, d, lanes, cr, ci, rev):
    for n, s in enumerate((1, 2, 4)):
        ar = dbl_ref[d, 0, n, :, lanes]
        ai = dbl_ref[d, 1, n, :, lanes]
        shift = SUBLANES - s if rev else s
        sr, si = pltpu.roll(xr, shift, 0), pltpu.roll(xi, shift, 0)
        xr, xi = xr + ar * sr - ai * si, xi + ar * si + ai * sr
    pr = pw_ref[d, 0, :, lanes]
    pi = pw_ref[d, 1, :, lanes]
    hr = xr + pr * cr - pi * ci
    hi = xi + pr * ci + pi * cr
    sel = slice(0, 1) if rev else slice(SUBLANES - 1, SUBLANES)
    return hr, hi, hr[sel], hi[sel]


def _s5_body(uf_ref, ub_ref, bre_ref, bim_ref, cre_ref, cim_ref, dbl_ref, pw_ref, yf_ref, yb_ref,
             hr_s, hi_s, carry_s, *, tt, gw, sw):
    i = pl.program_id(1)

    @pl.when(i == 0)
    def _():
        carry_s[...] = jnp.zeros_like(carry_s)

    ntile = tt // SUBLANES
    for ch in range(S5_CHUNKS):
        cols = slice(ch * gw, (ch + 1) * gw)
        lanes = slice(ch * sw, (ch + 1) * sw)
        for d, u_ref in enumerate((uf_ref, ub_ref)):
            u = u_ref[:, cols].astype(BF16)
            hr_s[d] = jnp.dot(u, bre_ref[d, ch], preferred_element_type=F32)
            hi_s[d] = jnp.dot(u, bim_ref[d, ch], preferred_element_type=F32)

        def step(k, carry):
            crf, cif, crb, cib = carry
            rows_f = pl.ds(pl.multiple_of(k * SUBLANES, SUBLANES), SUBLANES)
            rows_b = pl.ds(pl.multiple_of((ntile - 1 - k) * SUBLANES, SUBLANES), SUBLANES)
            hr, hi, crf, cif = _scan_tile_cplx(hr_s[0, rows_f, :], hi_s[0, rows_f, :], dbl_ref, pw_ref, 0, lanes,
                                               crf, cif, False)
            hr_s[0, rows_f, :] = hr
            hi_s[0, rows_f, :] = hi
            hr, hi, crb, cib = _scan_tile_cplx(hr_s[1, rows_b, :], hi_s[1, rows_b, :], dbl_ref, pw_ref, 1, lanes,
                                               crb, cib, True)
            hr_s[1, rows_b, :] = hr
            hi_s[1, rows_b, :] = hi
            return crf, cif, crb, cib

        init = tuple(carry_s[n, 0:1, lanes] for n in range(4))
        fin = lax.fori_loop(0, ntile, step, init)
        for n in range(4):
            carry_s[n, 0:1, lanes] = fin[n]
        for d, y_ref in enumerate((yf_ref, yb_ref)):
            y_ref[:, cols] = (jnp.dot(hr_s[d].astype(BF16), cre_ref[d, ch], preferred_element_type=F32)
                              - jnp.dot(hi_s[d].astype(BF16), cim_ref[d, ch], preferred_element_type=F32))


def _s5(u, bre, bim, cre, cim, dbl, pw, nb, seq, ctx, tt):
    t, c = u.shape
    gw = c // S5_CHUNKS
    sw = bre.shape[-1]
    geo = (nb, seq, ctx, tt)
    full = lambda shape: pl.BlockSpec(shape, lambda b, i: (0,) * len(shape))
    return pl.pallas_call(
        functools.partial(_s5_body, tt=tt, gw=gw, sw=sw),
        grid=(nb, (seq + ctx) // tt),
        in_specs=[_seq_spec(c, 0, False, *geo), _seq_spec(c, 0, True, *geo),
                  full(bre.shape), full(bim.shape), full(cre.shape), full(cim.shape), full(dbl.shape), full(pw.shape)],
        out_specs=[_seq_spec(c, 0, False, *geo), _seq_spec(c, 0, True, *geo)],
        out_shape=[jax.ShapeDtypeStruct((t, c), F32)] * 2,
        scratch_shapes=[pltpu.VMEM((2, tt, sw), F32), pltpu.VMEM((2, tt, sw), F32),
                        pltpu.VMEM((4, SUBLANES, sw * S5_CHUNKS), F32)],
        compiler_params=_params(2),
        name="s5_scan",
    )(u, u, bre, bim, cre, cim, dbl, pw)


def _s5_glu_body(u_ref, yf_ref, yb_ref, d_ref, w_ref, o_ref, wb_ref, *, c):
    @pl.when(pl.program_id(0) == 0)
    def _():
        wb_ref[...] = w_ref[...].astype(BF16)

    y = jax.nn.gelu(u_ref[...] * d_ref[...] + yf_ref[...] + yb_ref[...])
    z = jnp.dot(y.astype(BF16), wb_ref[...], preferred_element_type=F32)
    o_ref[...] = (z[:, :c] * jax.nn.sigmoid(z[:, c:])).astype(o_ref.dtype)


def _s5_glu(u, yf, yb, dskip, w_glu, l, tm):
    t, c = u.shape
    return pl.pallas_call(
        functools.partial(_s5_glu_body, c=c),
        grid=(t // tm,),
        in_specs=[pl.BlockSpec((tm, c), lambda i: (i, 0))] * 3
        + [pl.BlockSpec((None, 1, c), lambda i: (l, 0, 0)), pl.BlockSpec((None, c, 2 * c), lambda i: (l, 0, 0))],
        out_specs=pl.BlockSpec((tm, c), lambda i: (i, 0)),
        out_shape=jax.ShapeDtypeStruct((t, c), BF16),
        scratch_shapes=[pltpu.VMEM((c, 2 * c), BF16)],
        compiler_params=_params(1),
        name="s5_glu",
    )(u, yf, yb, dskip, w_glu)


def _chunk_cumsum(x, rev, ch):
    row = lax.broadcasted_iota(jnp.int32, x.shape, 0)
    s = 1
    while s < ch:
        if rev:
            x = x + jnp.where(row < ch - s, pltpu.roll(x, ch - s, 0), 0.0)
        else:
            x = x + jnp.where(row >= s, pltpu.roll(x, s, 0), 0.0)
        s *= 2
    return x


def _chunk_scan(q_s, k_s, v_s, lf_s, st_s, o_refs, *, tt, ch, heads, dk, dv):
    nchunk = tt // ch
    tri_r = lax.broadcasted_iota(jnp.int32, (ch, ch), 0)
    tri_c = lax.broadcasted_iota(jnp.int32, (ch, ch), 1)

    def step(n, _):
        for d in range(2):
            rev = d == 1
            cc = (nchunk - 1 - n) if rev else n
            rows = pl.ds(pl.multiple_of(cc * ch, ch), ch)
            b = _chunk_cumsum(lf_s[d, rows, :], rev, ch)
            piv = ch // 2 if rev else ch // 2 - 1
            end = 0 if rev else ch - 1
            m = b[piv:piv + 1, :]
            bl = b[end:end + 1, :]
            qm = q_s[d, rows, :] * jnp.exp(b - m)
            km = k_s[d, rows, :] * jnp.exp(m - b)
            qg = (qm * jnp.exp(m)).astype(BF16)
            kg = (km * jnp.exp(bl - m)).astype(BF16)
            dec = jnp.exp(bl)
            qm = qm.astype(BF16)
            km = km.astype(BF16)
            v = v_s[d, rows, :].astype(BF16)
            keep = (tri_r <= tri_c) if rev else (tri_r >= tri_c)
            for h in range(heads):
                ks = slice(h * dk, (h + 1) * dk)
                vs = slice(h * dv, (h + 1) * dv)
                sc = lax.dot_general(qm[:, ks], km[:, ks], NT_DIMS, preferred_element_type=F32)
                sc = jnp.where(keep, sc, 0.0).astype(BF16)
                st = st_s[d, h]
                o = jnp.dot(sc, v[:, vs], preferred_element_type=F32)
                o = o + lax.dot_general(qg[:, ks], st.astype(BF16), NT_DIMS, preferred_element_type=F32)
                st_s[d, h] = st * dec[:, ks] + lax.dot_general(v[:, vs], kg[:, ks], TN_DIMS,
                                                               preferred_element_type=F32)
                o_refs[d][rows, vs] = o
        return 0

    lax.fori_loop(0, nchunk, step, 0)


def _gla_body(qf_ref, zf_ref, qb_ref, zb_ref, wa_ref, ba_ref, of_ref, ob_ref, q_s, k_s, v_s, lf_s, st_s,
              *, tt, ch, heads, dk, dv):
    @pl.when(pl.program_id(1) == 0)
    def _():
        st_s[...] = jnp.zeros_like(st_s)

    hk = heads * dk
    for d, (x_ref, z_ref) in enumerate(((qf_ref, zf_ref), (qb_ref, zb_ref))):
        q_s[d] = x_ref[:, :hk] * dk ** -0.5
        k_s[d] = x_ref[:, hk:2 * hk]
        v_s[d] = x_ref[:, 2 * hk:]
        la = jnp.dot(z_ref[...].astype(BF16), wa_ref[d], preferred_element_type=F32) + ba_ref[d]
        lf_s[d] = jax.nn.log_sigmoid(la) / GLA_TAU
    _chunk_scan(q_s, k_s, v_s, lf_s, st_s, (of_ref, ob_ref), tt=tt, ch=ch, heads=heads, dk=dk, dv=dv)


def _hgrn_body(xf_ref, xb_ref, lb_ref, of_ref, ob_ref, q_s, k_s, v_s, lf_s, st_s, *, tt, ch, heads, dk, dv):
    @pl.when(pl.program_id(1) == 0)
    def _():
        st_s[...] = jnp.zeros_like(st_s)

    hk = heads * dk
    lb = lb_ref[...]
    for d, x_ref in enumerate((xf_ref, xb_ref)):
        z = x_ref[:, d * hk:(d + 1) * hk]
        q_s[d] = jax.nn.silu(x_ref[:, 2 * hk:3 * hk]) * dk ** -0.5
        v_s[d] = x_ref[:, 3 * hk:]
        k_s[d] = (1.0 - lb) * jax.nn.sigmoid(-z)
        lf_s[d] = jnp.log(lb + (1.0 - lb) * jax.nn.sigmoid(z))
    _chunk_scan(q_s, k_s, v_s, lf_s, st_s, (of_ref, ob_ref), tt=tt, ch=ch, heads=heads, dk=dk, dv=dv)


def _chunk_scratch(tt, heads, dk, dv):
    return [pltpu.VMEM((2, tt, heads * dk), F32), pltpu.VMEM((2, tt, heads * dk), F32),
            pltpu.VMEM((2, tt, heads * dv), F32), pltpu.VMEM((2, tt, heads * dk), F32),
            pltpu.VMEM((2, heads, dv, dk), F32)]


def _gla(pc, za, wa, ba, nb, seq, ctx, tt, ch, heads, dk, dv):
    t = pc.shape[0]
    geo = (nb, seq, ctx, tt)
    wide = 2 * heads * dk + heads * dv
    full = lambda shape: pl.BlockSpec(shape, lambda b, i: (0,) * len(shape))
    return pl.pallas_call(
        functools.partial(_gla_body, tt=tt, ch=ch, heads=heads, dk=dk, dv=dv),
        grid=(nb, (seq + ctx) // tt),
        in_specs=[_seq_spec(wide, 0, False, *geo), _seq_spec(za.shape[1], 0, False, *geo),
                  _seq_spec(wide, 0, True, *geo), _seq_spec(za.shape[1], 0, True, *geo),
                  full(wa.shape), full(ba.shape)],
        out_specs=[_seq_spec(heads * dv, 0, False, *geo), _seq_spec(heads * dv, 0, True, *geo)],
        out_shape=[jax.ShapeDtypeStruct((t, heads * dv), F32)] * 2,
        scratch_shapes=_chunk_scratch(tt, heads, dk, dv),
        compiler_params=_params(2),
        name="gla_scan",
    )(pc, za, pc, za, wa, ba)


def _hgrn(pd, lb, nb, seq, ctx, tt, ch, heads, dk, dv):
    t = pd.shape[0]
    geo = (nb, seq, ctx, tt)
    wide = 3 * heads * dk + heads * dv
    return pl.pallas_call(
        functools.partial(_hgrn_body, tt=tt, ch=ch, heads=heads, dk=dk, dv=dv),
        grid=(nb, (seq + ctx) // tt),
        in_specs=[_seq_spec(wide, 0, False, *geo), _seq_spec(wide, 0, True, *geo),
                  pl.BlockSpec(lb.shape, lambda b, i: (0, 0))],
        out_specs=[_seq_spec(heads * dv, 0, False, *geo), _seq_spec(heads * dv, 0, True, *geo)],
        out_shape=[jax.ShapeDtypeStruct((t, heads * dv), F32)] * 2,
        scratch_shapes=_chunk_scratch(tt, heads, dk, dv),
        compiler_params=_params(2),
        name="hgrn_scan",
    )(pd, pd, lb)


def _headnorm_body(of_ref, ob_ref, g_ref, n_ref, o_ref, *, heads, dv):
    o = of_ref[...] + ob_ref[...]
    parts = [_rms(o[:, h * dv:(h + 1) * dv]) for h in range(heads)]
    y = jnp.concatenate(parts, axis=-1) * n_ref[...]
    o_ref[...] = (y * jax.nn.silu(g_ref[...])).astype(o_ref.dtype)


def _headnorm(of, ob, gate_arr, gate_blk, gain, l, heads, tm):
    t, c = of.shape
    return pl.pallas_call(
        functools.partial(_headnorm_body, heads=heads, dv=c // heads),
        grid=(t // tm,),
        in_specs=[pl.BlockSpec((tm, c), lambda i: (i, 0)), pl.BlockSpec((tm, c), lambda i: (i, 0)),
                  pl.BlockSpec((tm, c), lambda i: (i, gate_blk)), pl.BlockSpec((None, 1, c), lambda i: (l, 0, 0))],
        out_specs=pl.BlockSpec((tm, c), lambda i: (i, 0)),
        out_shape=jax.ShapeDtypeStruct((t, c), BF16),
        compiler_params=_params(1),
        name="headnorm_gate",
    )(of, ob, gate_arr, gain)


def _merge_body(ya_ref, yb_ref, yc_ref, yd_ref, g0_ref, g1_ref, g2_ref, g3_ref, w_ref, o_ref, wb_ref):
    @pl.when(pl.program_id(1) == 0)
    def _():
        wb_ref[...] = w_ref[...].astype(BF16)

    acc = None
    for k, (y_ref, g_ref) in enumerate(((ya_ref, g0_ref), (yb_ref, g1_ref), (yc_ref, g2_ref), (yd_ref, g3_ref))):
        term = jax.nn.sigmoid(g_ref[...]) * jnp.dot(y_ref[...], wb_ref[k], preferred_element_type=F32)
        acc = term if acc is None else acc + term
    o_ref[...] = acc.astype(o_ref.dtype)


def _merge(ys, pg, w_branch, l, m, tm):
    c = ys[0].shape[1]
    d = w_branch.shape[3]
    tn = 512
    nj = d // tn
    y_spec = pl.BlockSpec((tm, c), lambda j, i: (i, 0))
    g_specs = [pl.BlockSpec((tm, tn), functools.partial(lambda j, i, k: (i, k * nj + j), k=k)) for k in range(N_BRANCH)]
    return pl.pallas_call(
        _merge_body,
        grid=(nj, m // tm),
        in_specs=[y_spec] * N_BRANCH + g_specs + [pl.BlockSpec((None, N_BRANCH, c, tn), lambda j, i: (l, 0, 0, j))],
        out_specs=pl.BlockSpec((tm, tn), lambda j, i: (i, j)),
        out_shape=jax.ShapeDtypeStruct((m, d), BF16),
        scratch_shapes=[pltpu.VMEM((N_BRANCH, c, tn), BF16)],
        compiler_params=_params(2),
        name="branch_merge",
    )(*ys, pg, pg, pg, pg, w_branch)


def _resid_body(x_ref, y_ref, g_ref, mod_ref, xo_ref, *f_refs, gate_i, gy, gf, shift_i, scale_i):
    xn = x_ref[...] + mod_ref[gate_i:gate_i + 1, :] * (_rms(y_ref[...]) * g_ref[gy:gy + 1, :])
    xo_ref[...] = xn
    if f_refs:
        f = _rms(xn) * g_ref[gf:gf + 1, :]
        f_refs[0][...] = (f * (1.0 + mod_ref[scale_i:scale_i + 1, :]) + mod_ref[shift_i:shift_i + 1, :]).astype(BF16)


def _resid(x, y, gain, mod, l, m, seq, nb, tm, gate_i, gy, with_f):
    d = x.shape[1]
    row = pl.BlockSpec((tm, d), lambda i: (i, 0))
    out_shape = [jax.ShapeDtypeStruct((m, d), F32)]
    out_specs = [row]
    if with_f:
        out_shape.append(jax.ShapeDtypeStruct((m, d), BF16))
        out_specs.append(row)
    return pl.pallas_call(
        functools.partial(_resid_body, gate_i=gate_i, gy=gy, gf=2, shift_i=3, scale_i=4),
        grid=(m // tm,),
        in_specs=[row, row, pl.BlockSpec((None, 4, d), lambda i: (l, 0, 0)),
                  pl.BlockSpec((None, N_MOD, d), lambda i: (_mod_row(i, tm, seq, nb), 0, 0))],
        out_specs=out_specs,
        out_shape=out_shape,
        compiler_params=_params(1),
        name="residual",
    )(x, y, gain, mod)


def _top_rows(x, n_rows, vals_ref):
    iota = lax.broadcasted_iota(jnp.int32, x.shape, 0)

    def step(r, x):
        mx = jnp.max(x, axis=0, keepdims=True)
        vals_ref[pl.ds(r, 1), :] = mx
        first = jnp.min(jnp.where(x == mx, iota, n_rows), axis=0, keepdims=True)
        return jnp.where(iota == first, -jnp.inf, x)

    lax.fori_loop(0, PEER_TOPK, step, x)


def _route_body(q_ref, keys_ref, s1_ref, e1_ref, s2_ref, e2_ref, tau_ref, va_s, vb_s, vc_s, *, dq):
    for h in range(PEER_HEADS):
        tops = []
        scores = []
        for p, v_s in enumerate((va_s, vb_s)):
            qh = q_ref[:, h * 2 * dq + p * dq:h * 2 * dq + (p + 1) * dq].astype(BF16)
            s = lax.dot_general(keys_ref[h, p].astype(BF16), qh, NT_DIMS, preferred_element_type=F32)
            _top_rows(s, N_KEYS, v_s)
            scores.append(s)
            tops.append(v_s[...])
        a, b = tops
        half = PEER_TOPK // 2
        cand = jnp.concatenate([a[0:1, :] + b] + [a[r:r + 1, :] + b[0:half, :] for r in range(1, half)]
                               + [a[half:, :] + b[0:1, :]], axis=0)
        _top_rows(cand, cand.shape[0], vc_s)
        best = vc_s[...]
        z = jnp.sum(jnp.exp(best - best[0:1, :]), axis=0, keepdims=True)
        s1_ref[h] = scores[0]
        s2_ref[h] = scores[1]
        e1_ref[h] = jnp.exp(scores[0] - a[0:1, :]) / z
        e2_ref[h] = jnp.exp(scores[1] - b[0:1, :])
        tau_ref[h:h + 1, :] = best[PEER_TOPK - 1:PEER_TOPK, :]


def _route(q, keys, l, tm):
    t = q.shape[0]
    dq = keys.shape[-1]
    big = jax.ShapeDtypeStruct((PEER_HEADS, N_KEYS, t), F32)
    big_spec = pl.BlockSpec((PEER_HEADS, N_KEYS, tm), lambda i: (0, 0, i))
    return pl.pallas_call(
        functools.partial(_route_body, dq=dq),
        grid=(t // tm,),
        in_specs=[pl.BlockSpec((tm, q.shape[1]), lambda i: (i, 0)),
                  pl.BlockSpec((None,) + keys.shape[1:], lambda i: (l, 0, 0, 0, 0))],
        out_specs=[big_spec] * 4 + [pl.BlockSpec((PEER_HEADS, tm), lambda i: (0, i))],
        out_shape=[big] * 4 + [jax.ShapeDtypeStruct((PEER_HEADS, t), F32)],
        scratch_shapes=[pltpu.VMEM((PEER_TOPK, tm), F32)] * 3,
        compiler_params=_params(1),
        name="peer_route",
    )(q, keys)


def _peer_body(ft_ref, u_ref, vt_ref, s1_ref, e1_ref, s2_ref, e2_ref, tau_ref, o_ref, g_s, act_s, *, te, nj):
    j = pl.program_id(1)

    @pl.when(j == 0)
    def _():
        o_ref[...] = jnp.zeros_like(o_ref)
        act_s[...] = jnp.zeros_like(act_s)

    tile = jnp.maximum(j - 1, 0)
    tm = act_s.shape[1]
    for r in range(te // N_KEYS):
        i1 = tile * (te // N_KEYS) + r
        rows = slice(r * N_KEYS, (r + 1) * N_KEYS)
        s1_rows = [s1_ref[h, pl.ds(i1, 1), :] for h in range(PEER_HEADS)]
        e1_rows = [e1_ref[h, pl.ds(i1, 1), :] for h in range(PEER_HEADS)]
        for cb in range(tm // LANES):
            cs = slice(cb * LANES, (cb + 1) * LANES)
            w = None
            for h in range(PEER_HEADS):
                c = s2_ref[h, :, cs] + s1_rows[h][:, cs]
                term = jnp.where(c >= tau_ref[h:h + 1, cs], e2_ref[h, :, cs] * e1_rows[h][:, cs], 0.0)
                w = term if w is None else w + term
            g_s[rows, cs] = (w * act_s[rows, cs]).astype(BF16)

    o_ref[...] += jnp.dot(vt_ref[...], g_s[...], preferred_element_type=F32)
    act_s[...] = jax.nn.gelu(jnp.dot(u_ref[...], ft_ref[...], preferred_element_type=F32))


def _peer(ft, u, vt, s1, e1, s2, e2, tau, tm, te):
    d, t = ft.shape
    nj = u.shape[0] // te
    once = pl.Buffered(1)
    big_spec = pl.BlockSpec((PEER_HEADS, N_KEYS, tm), lambda i, j: (0, 0, i), pipeline_mode=once)
    return pl.pallas_call(
        functools.partial(_peer_body, te=te, nj=nj),
        grid=(t // tm, nj + 1),
        in_specs=[pl.BlockSpec((d, tm), lambda i, j: (0, i), pipeline_mode=once),
                  pl.BlockSpec((te, d), lambda i, j: (jnp.minimum(j, nj - 1), 0)),
                  pl.BlockSpec((d, te), lambda i, j: (0, jnp.maximum(j - 1, 0))),
                  big_spec, big_spec, big_spec, big_spec,
                  pl.BlockSpec((PEER_HEADS, tm), lambda i, j: (0, i), pipeline_mode=once)],
        out_specs=pl.BlockSpec((d, tm), lambda i, j: (0, i)),
        out_shape=jax.ShapeDtypeStruct((d, t), F32),
        scratch_shapes=[pltpu.VMEM((te, tm), BF16), pltpu.VMEM((te, tm), F32)],
        compiler_params=_params(2),
        name="peer_experts",
    )(ft, u, vt, s1, e1, s2, e2, tau)


def _block_diag(w):
    n, r, c = w.shape
    eye = jnp.eye(n, dtype=w.dtype)
    return (w[:, :, None, :] * eye[:, None, :, None]).reshape(n * r, n * c)


def _lru_params(w_a, b_a, w_i, b_i, lam):
    wd = jnp.stack([jnp.concatenate([_block_diag(w_a[d]), _block_diag(w_i[d])], axis=1) for d in range(2)])
    bd = jnp.stack([jnp.concatenate([b_a[d], b_i[d]])[None, :] for d in range(2)])
    sp = jax.nn.softplus(-lam)[:, None, :]
    return wd.astype(BF16), bd, sp


def _s5_params(a_re, a_im, log_dt, b_re, b_im, c_re, c_im):
    g, p = a_re.shape[1:]
    gc = g // S5_CHUNKS
    outs = [[] for _ in range(6)]
    for d in range(2):
        big_a = lax.complex(a_re[d], a_im[d])
        dt = jnp.exp(log_dt[d])[:, None]
        a_bar = jnp.exp(big_a * dt)
        b_bar = ((a_bar - 1.0) / big_a)[..., None] * lax.complex(b_re[d], b_im[d])
        c_mat = lax.complex(c_re[d], c_im[d])
        bm = jnp.swapaxes(b_bar, 1, 2).reshape(S5_CHUNKS, gc, S5_GROUP, p)
        cm = jnp.swapaxes(c_mat, 1, 2).reshape(S5_CHUNKS, gc, p, S5_GROUP)
        bm = jnp.stack([_block_diag(bm[k]) for k in range(S5_CHUNKS)])
        cm = jnp.stack([_block_diag(cm[k]) for k in range(S5_CHUNKS)])
        pows = jnp.stack([jnp.exp(big_a * dt * float(n)) for n in range(1, SUBLANES + 1)]).reshape(SUBLANES, g * p)
        row = jnp.arange(SUBLANES)[:, None]
        dbl = jnp.stack([jnp.where((row < SUBLANES - s) if d == 1 else (row >= s), pows[s - 1][None, :], 0.0)
                         for s in (1, 2, 4)])
        pw = pows[::-1] if d == 1 else pows
        for lst, val in zip(outs, (jnp.real(bm), jnp.imag(bm), jnp.real(cm), jnp.imag(cm),
                                   jnp.stack([jnp.real(dbl), jnp.imag(dbl)]), jnp.stack([jnp.real(pw), jnp.imag(pw)]))):
            lst.append(val)
    bre, bim, cre, cim, dbl, pw = (jnp.stack(o) for o in outs)
    return bre.astype(BF16), bim.astype(BF16), cre.astype(BF16), cim.astype(BF16), dbl, pw


def _to_col_major(t, nb, seq):
    rows = seq // GRID_W
    lat = t[:nb * seq].reshape(nb, rows, GRID_W, -1).transpose(0, 2, 1, 3).reshape(nb * seq, -1)
    return jnp.concatenate([lat, t[nb * seq:]], axis=0)


def _to_row_major(t, nb, seq):
    rows = seq // GRID_W
    lat = t[:nb * seq].reshape(nb, GRID_W, rows, -1).transpose(0, 2, 1, 3).reshape(nb * seq, -1)
    return jnp.concatenate([lat, t[nb * seq:]], axis=0)


def kernel(x, c, ctx, c_ctx, w_ada, b_ada, norm_gain, w_in, lru_conv_w, lru_conv_b, lru_w_a, lru_b_a, lru_w_i,
           lru_b_i, lru_lambda, s5_a_re, s5_a_im, s5_log_dt, s5_b_re, s5_b_im, s5_c_re, s5_c_im, s5_d, s5_w_glu,
           gla_w_alpha, gla_b_alpha, gla_norm, hgrn_lb_logits, hgrn_norm, w_branch, w_out, peer_w_q, peer_keys,
           peer_u, peer_v):
    nb, seq, d = x.shape
    nctx = ctx.shape[1]
    depth = w_ada.shape[0]
    mix = d // 4
    n_lat = nb * seq
    n_tok = n_lat + nb * nctx
    assert nb + 1 <= SUBLANES and seq % GRID_W == 0
    gla_dk, gla_dv = mix // 2 // GLA_HEADS, mix // GLA_HEADS
    hg_dk = hg_dv = mix // HGRN_HEADS
    gla_rank = gla_w_alpha.shape[2]
    tm = _tile(math.gcd(seq, nb * nctx), (512, 256, 128))
    tr = min(tm, 256)
    tt = _tile(math.gcd(seq, nctx), (256, 128, 64, 32))
    ch = 32

    w_in_t = jnp.swapaxes(w_in, 1, 2)
    o_c = 3 * mix
    o_ca = o_c + 2 * GLA_HEADS * gla_dk + GLA_HEADS * gla_dv + mix
    o_dg = 2 * HGRN_HEADS * hg_dk + HGRN_HEADS * hg_dk + HGRN_HEADS * hg_dv
    o_gt = o_dg + mix
    za_w = 128

    p_lb = jax.nn.softmax(hgrn_lb_logits.astype(F32), axis=0)
    lower = jnp.cumsum(p_lb, axis=0) - p_lb[0]
    cvec = jnp.zeros((SUBLANES, d), F32).at[:nb].set(c).at[nb].set(c_ctx)
    b_ada3 = b_ada[:, None, :]

    xs = jnp.concatenate([x.reshape(n_lat, d), ctx.reshape(nb * nctx, d)], axis=0)
    for l in range(depth):
        last = l == depth - 1
        m_out = n_lat if last else n_tok
        mod = _adaln(cvec, w_ada, b_ada3, l).reshape(SUBLANES, N_MOD, d)
        h = _normmod(xs, norm_gain, mod, l, seq, nb, tr)

        mm_in = lambda off, n, shift, tn, name: _mmt(h, w_in_t, l, off, n, shift, tn, n_tok, tm, name)
        pa = mm_in(0, 2 * mix, 0, 512, "proj_lru")
        pb = mm_in(2 * mix, mix, 0, 512, "proj_s5")
        pc = mm_in(o_c, o_ca - o_c - mix, 0, 512, "proj_gla")
        pcg = mm_in(o_ca - mix, mix, 0, 512, "proj_gla_gate")
        za = mm_in(o_ca, za_w, 0, za_w, "proj_gla_rank")
        pd = mm_in(o_ca, o_gt, gla_rank, 512, "proj_hgrn")
        pg = mm_in(o_ca + o_gt, N_BRANCH * d, gla_rank, 512, "proj_gate")

        wd, bd, sp = _lru_params(lru_w_a[l], lru_b_a[l], lru_w_i[l], lru_b_i[l], lru_lambda[l])
        hf, hb = _lru(pa, lru_conv_w[l], lru_conv_b[l][None, :], wd, bd, sp, nb, seq, nctx, tt)
        ya = _lru_out(pa, hf, hb, tm)

        s5p = _s5_params(s5_a_re[l], s5_a_im[l], s5_log_dt[l], s5_b_re[l], s5_b_im[l], s5_c_re[l], s5_c_im[l])
        ub = _to_col_major(pb, nb, seq)
        yf, ybk = _s5(ub, *s5p, nb, seq, nctx, tt)
        yb = _to_row_major(_s5_glu(ub, yf, ybk, s5_d[:, None, :], s5_w_glu, l, tm), nb, seq)

        wa = jnp.zeros((2, za_w, GLA_HEADS * gla_dk), F32).at[:, :gla_rank].set(gla_w_alpha[l]).astype(BF16)
        of, ob = _gla(pc, za, wa, gla_b_alpha[l][:, None, :], nb, seq, nctx, tt, ch, GLA_HEADS, gla_dk, gla_dv)
        yc = _headnorm(of, ob, pcg, 0, gla_norm[:, None, :], l, GLA_HEADS, tm)

        pdc = _to_col_major(pd, nb, seq)
        of, ob = _hgrn(pdc, lower[l][None, :], nb, seq, nctx, tt, ch, HGRN_HEADS, hg_dk, hg_dv)
        yd = _to_row_major(_headnorm(of, ob, pdc, o_dg // mix, hgrn_norm[:, None, :], l, HGRN_HEADS, tm), nb, seq)

        zm = _merge((ya, yb, yc, yd), pg, w_branch, l, m_out, tm)
        mo = _mm(zm, w_out, l, 0, d, F32, m_out, tm, "proj_out")
        xs, f = _resid(xs, mo, norm_gain, mod, l, m_out, seq, nb, tr, 2, 1, True)

        q = _mm(f, peer_w_q, l, 0, peer_w_q.shape[2], F32, m_out, tm, "peer_query")
        s1, e1, s2, e2, tau = _route(q, peer_keys, l, _tile(m_out, (256, 128)))
        yt = _peer(f.T, peer_u[l].astype(BF16), peer_v[l].T.astype(BF16), s1, e1, s2, e2, tau,
                   _tile(m_out, (512, 256, 128)), 512)
        xs = _resid(xs, yt.T, norm_gain, mod, l, m_out, seq, nb, tr, 5, 3, False)[0]
    return xs[:n_lat].reshape(nb, seq, d)
```

```python
import functools
import math

import jax
import jax.numpy as jnp
from jax import lax
from jax.experimental import pallas as pl
from jax.experimental.pallas import tpu as pltpu

F32 = jnp.float32
BF16 = jnp.bfloat16
EPS = 1e-6
GRID_W = 64
N_MOD = 6
N_BRANCH = 4
LRU_BLOCKS = 16
LRU_C = 8.0
S5_GROUP = 16
S5_STATE = 64
S5_CHUNKS = 4
GLA_HEADS = 4
GLA_TAU = 16.0
HGRN_HEADS = 8
PEER_HEADS = 8
N_KEYS = 128
PEER_TOPK = 16
SUBLANES = 8
LANES = 128
VMEM_LIMIT = 56 * 1024 * 1024

NT_DIMS = (((1,), (1,)), ((), ()))
TN_DIMS = (((0,), (0,)), ((), ()))


def _params(n_axes, vmem=VMEM_LIMIT):
    return pltpu.CompilerParams(dimension_semantics=("arbitrary",) * n_axes, vmem_limit_bytes=vmem)


def _tile(n, prefs):
    for p in prefs:
        if n % p == 0:
            return p
    raise ValueError(f"no tile for {n} in {prefs}")


def _rms(x):
    return x * lax.rsqrt(jnp.mean(x * x, axis=-1, keepdims=True) + EPS)


def _ada_body(c_ref, w_ref, b_ref, o_ref):
    c = c_ref[...]
    s = (c * jax.nn.sigmoid(c)).astype(BF16)
    o_ref[...] = jnp.dot(s, w_ref[...].astype(BF16), preferred_element_type=F32) + b_ref[...]


def _adaln(cvec, w_ada, b_ada3, l):
    rows, d = cvec.shape
    n = w_ada.shape[2]
    tn = 512
    return pl.pallas_call(
        _ada_body,
        grid=(n // tn,),
        in_specs=[
            pl.BlockSpec((rows, d), lambda j: (0, 0)),
            pl.BlockSpec((None, d, tn), lambda j: (l, 0, j)),
            pl.BlockSpec((None, 1, tn), lambda j: (l, 0, j)),
        ],
        out_specs=pl.BlockSpec((rows, tn), lambda j: (0, j)),
        out_shape=jax.ShapeDtypeStruct((rows, n), F32),
        compiler_params=_params(1),
        name="adaln",
    )(cvec, w_ada, b_ada3)


def _normmod_body(x_ref, g_ref, mod_ref, o_ref, *, gi, shift_i, scale_i):
    y = _rms(x_ref[...]) * g_ref[gi:gi + 1, :]
    o_ref[...] = (y * (1.0 + mod_ref[scale_i:scale_i + 1, :]) + mod_ref[shift_i:shift_i + 1, :]).astype(o_ref.dtype)


def _mod_row(i, tm, seq, nb):
    return jnp.minimum((i * tm) // seq, nb)


def _normmod(x, gain, mod, l, seq, nb, tm):
    t, d = x.shape
    return pl.pallas_call(
        functools.partial(_normmod_body, gi=0, shift_i=0, scale_i=1),
        grid=(t // tm,),
        in_specs=[
            pl.BlockSpec((tm, d), lambda i: (i, 0)),
            pl.BlockSpec((None, 4, d), lambda i: (l, 0, 0)),
            pl.BlockSpec((None, N_MOD, d), lambda i: (_mod_row(i, tm, seq, nb), 0, 0)),
        ],
        out_specs=pl.BlockSpec((tm, d), lambda i: (i, 0)),
        out_shape=jax.ShapeDtypeStruct((t, d), BF16),
        compiler_params=_params(1),
        name="normmod",
    )(x, gain, mod)


def _mm_body(a_ref, w_ref, o_ref, wb_ref):
    @pl.when(pl.program_id(1) == 0)
    def _():
        wb_ref[...] = w_ref[...].astype(BF16)

    o_ref[...] = jnp.dot(a_ref[...], wb_ref[...], preferred_element_type=F32).astype(o_ref.dtype)


def _mm(a, w, l, ncols, out_dtype, m, tm, tn, name):
    k = a.shape[1]
    w_spec = pl.BlockSpec((None, k, tn), lambda j, i: (l, 0, j))
    return pl.pallas_call(
        _mm_body,
        grid=(ncols // tn, m // tm),
        in_specs=[pl.BlockSpec((tm, k), lambda j, i: (i, 0)), w_spec],
        out_specs=pl.BlockSpec((tm, tn), lambda j, i: (i, j)),
        out_shape=jax.ShapeDtypeStruct((m, ncols), out_dtype),
        scratch_shapes=[pltpu.VMEM((k, tn), BF16)],
        compiler_params=_params(2),
        name=name,
    )(a, w)


def _mmt_body(a_ref, w_ref, *rest, shift):
    if shift:
        tail_ref, o_ref, wb_ref = rest
    else:
        o_ref, wb_ref = rest
    tn = wb_ref.shape[0]

    @pl.when(pl.program_id(1) == 0)
    def _():
        if shift:
            wb_ref[0:tn - shift, :] = w_ref[shift:tn, :].astype(BF16)
            wb_ref[tn - shift:tn, :] = tail_ref[...].astype(BF16)
        else:
            wb_ref[...] = w_ref[...].astype(BF16)

    o_ref[...] = lax.dot_general(a_ref[...], wb_ref[...], NT_DIMS, preferred_element_type=F32).astype(o_ref.dtype)


def _mmt(a, wt, l, row_off, ncols, shift, tn, m, tm, name):
    k = a.shape[1]
    base = row_off // tn
    in_specs = [pl.BlockSpec((tm, k), lambda j, i: (i, 0)), pl.BlockSpec((None, tn, k), lambda j, i: (l, base + j, 0))]
    args = [a, wt]
    if shift:
        per = tn // shift
        in_specs.append(pl.BlockSpec((None, shift, k), lambda j, i: (l, (base + j + 1) * per, 0)))
        args.append(wt)
    return pl.pallas_call(
        functools.partial(_mmt_body, shift=shift),
        grid=(ncols // tn, m // tm),
        in_specs=in_specs,
        out_specs=pl.BlockSpec((tm, tn), lambda j, i: (i, j)),
        out_shape=jax.ShapeDtypeStruct((m, ncols), F32),
        scratch_shapes=[pltpu.VMEM((tn, k), BF16)],
        compiler_params=_params(2),
        name=name,
    )(*args)


def _seq_block(b, i, rev, nb, seq, ctx, tt):
    nctx, nlat = ctx // tt, seq // tt
    ic = (nctx - 1 - i) if rev else i
    il = (nlat - 1 - (i - nctx)) if rev else (i - nctx)
    return jnp.where(i < nctx, (nb * seq) // tt + b * nctx + ic, b * nlat + il)


def _seq_spec(cols, col_blk, rev, nb, seq, ctx, tt):
    return pl.BlockSpec((tt, cols), lambda b, i: (_seq_block(b, i, rev, nb, seq, ctx, tt), col_blk))


def _halo_spec(cols, col_blk, rev, side, nb, seq, ctx, tt, total):
    per = tt // SUBLANES
    last = total // SUBLANES - 1

    def index(b, i):
        blk = _seq_block(b, i, rev, nb, seq, ctx, tt) * per
        blk = blk - 1 if side < 0 else blk + per
        return jnp.clip(blk, 0, last), col_blk

    return pl.BlockSpec((SUBLANES, cols), index)


def _stream_pos(i, rev, seq, ctx, tt):
    nctx, nlat = ctx // tt, seq // tt
    is_ctx = i < nctx
    ii = jnp.where(is_ctx, (nctx - 1 - i) if rev else i, (nlat - 1 - (i - nctx)) if rev else (i - nctx))
    n = jnp.where(is_ctx, nctx, nlat)
    return ii == 0, ii == n - 1


def _scan_tile_real(a, u, carry, rev):
    row = lax.broadcasted_iota(jnp.int32, a.shape, 0)
    for s in (1, 2, 4):
        if rev:
            a_sh, u_sh = pltpu.roll(a, SUBLANES - s, 0), pltpu.roll(u, SUBLANES - s, 0)
            ok = row < SUBLANES - s
        else:
            a_sh, u_sh = pltpu.roll(a, s, 0), pltpu.roll(u, s, 0)
            ok = row >= s
        u = jnp.where(ok, a * u_sh + u, u)
        a = jnp.where(ok, a * a_sh, a)
    h = u + a * carry
    return h, (h[0:1] if rev else h[SUBLANES - 1:SUBLANES])


def _lru_body(xf_ref, pf_ref, nf_ref, xb_ref, pb_ref, nb_ref, cw_ref, cb_ref, wd_ref, bd_ref, sp_ref,
              hf_ref, hb_ref, a_s, u_s, carry_s, *, seq, ctx, tt, c):
    i = pl.program_id(1)

    @pl.when(i == 0)
    def _():
        carry_s[...] = jnp.zeros_like(carry_s)

    row = lax.broadcasted_iota(jnp.int32, (tt, c), 0)
    for d, (x_ref, p_ref, n_ref) in enumerate(((xf_ref, pf_ref, nf_ref), (xb_ref, pb_ref, nb_ref))):
        first, last = _stream_pos(i, d == 1, seq, ctx, tt)
        x = x_ref[...]
        prev = jnp.where(first, 0.0, p_ref[SUBLANES - 1:SUBLANES, :])
        nx1 = jnp.where(last, 0.0, n_ref[0:1, :])
        nx2 = jnp.where(last, 0.0, n_ref[1:2, :])
        xm1 = jnp.where(row == 0, prev, pltpu.roll(x, 1, 0))
        xp1 = jnp.where(row == tt - 1, nx1, pltpu.roll(x, tt - 1, 0))
        xp2 = jnp.where(row == tt - 1, nx2, jnp.where(row == tt - 2, nx1, pltpu.roll(x, tt - 2, 0)))
        xl = xm1 * cw_ref[0:1, :] + x * cw_ref[1:2, :] + xp1 * cw_ref[2:3, :] + xp2 * cw_ref[3:4, :] + cb_ref[...]
        z = jnp.dot(xl.astype(BF16), wd_ref[d], preferred_element_type=F32) + bd_ref[d]
        r = jax.nn.sigmoid(z[:, :c])
        gi = jax.nn.sigmoid(z[:, c:])
        log_a = -LRU_C * r * sp_ref[d]
        a = jnp.exp(log_a)
        a_s[d] = a
        u_s[d] = jnp.sqrt(-jnp.tanh(log_a) * (a * a + 1.0)) * gi * xl

    ntile = tt // SUBLANES

    def step(k, carry):
        cf, cb = carry
        rows_f = pl.ds(pl.multiple_of(k * SUBLANES, SUBLANES), SUBLANES)
        rows_b = pl.ds(pl.multiple_of((ntile - 1 - k) * SUBLANES, SUBLANES), SUBLANES)
        h, cf = _scan_tile_real(a_s[0, rows_f, :], u_s[0, rows_f, :], cf, False)
        hf_ref[rows_f, :] = h
        h, cb = _scan_tile_real(a_s[1, rows_b, :], u_s[1, rows_b, :], cb, True)
        hb_ref[rows_b, :] = h
        return cf, cb

    cf, cb = lax.fori_loop(0, ntile, step, (carry_s[0, 0:1, :], carry_s[1, 0:1, :]))
    carry_s[0, 0:1, :] = cf
    carry_s[1, 0:1, :] = cb


def _lru(pa, cw, cb, wd, bd, sp, nb, seq, ctx, tt):
    t = pa.shape[0]
    c = cw.shape[1]
    geo = (nb, seq, ctx, tt)
    full = lambda shape: pl.BlockSpec(shape, lambda b, i: (0,) * len(shape))
    return pl.pallas_call(
        functools.partial(_lru_body, seq=seq, ctx=ctx, tt=tt, c=c),
        grid=(nb, (seq + ctx) // tt),
        in_specs=[
            _seq_spec(c, 0, False, *geo), _halo_spec(c, 0, False, -1, *geo, t), _halo_spec(c, 0, False, 1, *geo, t),
            _seq_spec(c, 0, True, *geo), _halo_spec(c, 0, True, -1, *geo, t), _halo_spec(c, 0, True, 1, *geo, t),
            full(cw.shape), full(cb.shape), full(wd.shape), full(bd.shape), full(sp.shape),
        ],
        out_specs=[_seq_spec(c, 0, False, *geo), _seq_spec(c, 0, True, *geo)],
        out_shape=[jax.ShapeDtypeStruct((t, c), F32)] * 2,
        scratch_shapes=[pltpu.VMEM((2, tt, c), F32), pltpu.VMEM((2, tt, c), F32), pltpu.VMEM((2, SUBLANES, c), F32)],
        compiler_params=_params(2),
        name="rglru_scan",
    )(pa, pa, pa, pa, pa, pa, cw, cb, wd, bd, sp)


def _lru_out_body(ay_ref, hf_ref, hb_ref, o_ref):
    o_ref[...] = (jax.nn.gelu(ay_ref[...]) * (hf_ref[...] + hb_ref[...])).astype(o_ref.dtype)


def _lru_out(pa, hf, hb, tm):
    t, c = hf.shape
    return pl.pallas_call(
        _lru_out_body,
        grid=(t // tm,),
        in_specs=[pl.BlockSpec((tm, c), lambda i: (i, 1)), pl.BlockSpec((tm, c), lambda i: (i, 0)),
                  pl.BlockSpec((tm, c), lambda i: (i, 0))],
        out_specs=pl.BlockSpec((tm, c), lambda i: (i, 0)),
        out_shape=jax.ShapeDtypeStruct((t, c), BF16),
        compiler_params=_params(1),
        name="rglru_out",
    )(pa, hf, hb)


def _scan_tile_cplx(xr, xi, dbl_ref, pw_ref, d, lanes, cr, ci, rev):
    for n, s in enumerate((1, 2, 4)):
        ar = dbl_ref[d, 0, n, :, lanes]
        ai = dbl_ref[d, 1, n, :, lanes]
        shift = SUBLANES - s if rev else s
        sr, si = pltpu.roll(xr, shift, 0), pltpu.roll(xi, shift, 0)
        xr, xi = xr + ar * sr - ai * si, xi + ar * si + ai * sr
    pr = pw_ref[d, 0, :, lanes]
    pi = pw_ref[d, 1, :, lanes]
    hr = xr + pr * cr - pi * ci
    hi = xi + pr * ci + pi * cr
    sel = slice(0, 1) if rev else slice(SUBLANES - 1, SUBLANES)
    return hr, hi, hr[sel], hi[sel]


def _s5_body(uf_ref, ub_ref, bre_ref, bim_ref, cre_ref, cim_ref, dbl_ref, pw_ref, yf_ref, yb_ref,
             hr_s, hi_s, carry_s, *, tt, gw, sw):
    i = pl.program_id(1)

    @pl.when(i == 0)
    def _():
        carry_s[...] = jnp.zeros_like(carry_s)

    ntile = tt // SUBLANES
    for ch in range(S5_CHUNKS):
        cols = slice(ch * gw, (ch + 1) * gw)
        lanes = slice(ch * sw, (ch + 1) * sw)
        for d, u_ref in enumerate((uf_ref, ub_ref)):
            u = u_ref[:, cols].astype(BF16)
            hr_s[d] = jnp.dot(u, bre_ref[d, ch], preferred_element_type=F32)
            hi_s[d] = jnp.dot(u, bim_ref[d, ch], preferred_element_type=F32)

        def step(k, carry):
            crf, cif, crb, cib = carry
            rows_f = pl.ds(pl.multiple_of(k * SUBLANES, SUBLANES), SUBLANES)
            rows_b = pl.ds(pl.multiple_of((ntile - 1 - k) * SUBLANES, SUBLANES), SUBLANES)
            hr, hi, crf, cif = _scan_tile_cplx(hr_s[0, rows_f, :], hi_s[0, rows_f, :], dbl_ref, pw_ref, 0, lanes,
                                               crf, cif, False)
            hr_s[0, rows_f, :] = hr
            hi_s[0, rows_f, :] = hi
            hr, hi, crb, cib = _scan_tile_cplx(hr_s[1, rows_b, :], hi_s[1, rows_b, :], dbl_ref, pw_ref, 1, lanes,
                                               crb, cib, True)
            hr_s[1, rows_b, :] = hr
            hi_s[1, rows_b, :] = hi
            return crf, cif, crb, cib

        init = tuple(carry_s[n, 0:1, lanes] for n in range(4))
        fin = lax.fori_loop(0, ntile, step, init)
        for n in range(4):
            carry_s[n, 0:1, lanes] = fin[n]
        for d, y_ref in enumerate((yf_ref, yb_ref)):
            y_ref[:, cols] = (jnp.dot(hr_s[d].astype(BF16), cre_ref[d, ch], preferred_element_type=F32)
                              - jnp.dot(hi_s[d].astype(BF16), cim_ref[d, ch], preferred_element_type=F32))


def _s5(u, bre, bim, cre, cim, dbl, pw, nb, seq, ctx, tt):
    t, c = u.shape
    gw = c // S5_CHUNKS
    sw = bre.shape[-1]
    geo = (nb, seq, ctx, tt)
    full = lambda shape: pl.BlockSpec(shape, lambda b, i: (0,) * len(shape))
    return pl.pallas_call(
        functools.partial(_s5_body, tt=tt, gw=gw, sw=sw),
        grid=(nb, (seq + ctx) // tt),
        in_specs=[_seq_spec(c, 0, False, *geo), _seq_spec(c, 0, True, *geo),
                  full(bre.shape), full(bim.shape), full(cre.shape), full(cim.shape), full(dbl.shape), full(pw.shape)],
        out_specs=[_seq_spec(c, 0, False, *geo), _seq_spec(c, 0, True, *geo)],
        out_shape=[jax.ShapeDtypeStruct((t, c), F32)] * 2,
        scratch_shapes=[pltpu.VMEM((2, tt, sw), F32), pltpu.VMEM((2, tt, sw), F32),
                        pltpu.VMEM((4, SUBLANES, sw * S5_CHUNKS), F32)],
        compiler_params=_params(2),
        name="s5_scan",
    )(u, u, bre, bim, cre, cim, dbl, pw)


def _s5_glu_body(u_ref, yf_ref, yb_ref, d_ref, w_ref, o_ref, wb_ref, *, c):
    @pl.when(pl.program_id(0) == 0)
    def _():
        wb_ref[...] = w_ref[...].astype(BF16)

    y = jax.nn.gelu(u_ref[...] * d_ref[...] + yf_ref[...] + yb_ref[...])
    z = jnp.dot(y.astype(BF16), wb_ref[...], preferred_element_type=F32)
    o_ref[...] = (z[:, :c] * jax.nn.sigmoid(z[:, c:])).astype(o_ref.dtype)


def _s5_glu(u, yf, yb, dskip, w_glu, l, tm):
    t, c = u.shape
    return pl.pallas_call(
        functools.partial(_s5_glu_body, c=c),
        grid=(t // tm,),
        in_specs=[pl.BlockSpec((tm, c), lambda i: (i, 0))] * 3
        + [pl.BlockSpec((None, 1, c), lambda i: (l, 0, 0)), pl.BlockSpec((None, c, 2 * c), lambda i: (l, 0, 0))],
        out_specs=pl.BlockSpec((tm, c), lambda i: (i, 0)),
        out_shape=jax.ShapeDtypeStruct((t, c), BF16),
        scratch_shapes=[pltpu.VMEM((c, 2 * c), BF16)],
        compiler_params=_params(1),
        name="s5_glu",
    )(u, yf, yb, dskip, w_glu)


def _chunk_cumsum(x, rev, ch):
    row = lax.broadcasted_iota(jnp.int32, x.shape, 0)
    s = 1
    while s < ch:
        if rev:
            x = x + jnp.where(row < ch - s, pltpu.roll(x, ch - s, 0), 0.0)
        else:
            x = x + jnp.where(row >= s, pltpu.roll(x, s, 0), 0.0)
        s *= 2
    return x


def _chunk_scan(q_s, k_s, v_s, lf_s, st_s, o_refs, *, tt, ch, heads, dk, dv):
    nchunk = tt // ch
    tri_r = lax.broadcasted_iota(jnp.int32, (ch, ch), 0)
    tri_c = lax.broadcasted_iota(jnp.int32, (ch, ch), 1)

    def step(n, _):
        for d in range(2):
            rev = d == 1
            cc = (nchunk - 1 - n) if rev else n
            rows = pl.ds(pl.multiple_of(cc * ch, ch), ch)
            b = _chunk_cumsum(lf_s[d, rows, :], rev, ch)
            piv = ch // 2 if rev else ch // 2 - 1
            end = 0 if rev else ch - 1
            m = b[piv:piv + 1, :]
            bl = b[end:end + 1, :]
            qm = q_s[d, rows, :] * jnp.exp(b - m)
            km = k_s[d, rows, :] * jnp.exp(m - b)
            qg = (qm * jnp.exp(m)).astype(BF16)
            kg = (km * jnp.exp(bl - m)).astype(BF16)
            dec = jnp.exp(bl)
            qm = qm.astype(BF16)
            km = km.astype(BF16)
            v = v_s[d, rows, :].astype(BF16)
            keep = (tri_r <= tri_c) if rev else (tri_r >= tri_c)
            for h in range(heads):
                ks = slice(h * dk, (h + 1) * dk)
                vs = slice(h * dv, (h + 1) * dv)
                sc = lax.dot_general(qm[:, ks], km[:, ks], NT_DIMS, preferred_element_type=F32)
                sc = jnp.where(keep, sc, 0.0).astype(BF16)
                st = st_s[d, h]
                o = jnp.dot(sc, v[:, vs], preferred_element_type=F32)
                o = o + lax.dot_general(qg[:, ks], st.astype(BF16), NT_DIMS, preferred_element_type=F32)
                st_s[d, h] = st * dec[:, ks] + lax.dot_general(v[:, vs], kg[:, ks], TN_DIMS,
                                                               preferred_element_type=F32)
                o_refs[d][rows, vs] = o
        return 0

    lax.fori_loop(0, nchunk, step, 0)


def _gla_body(qf_ref, zf_ref, qb_ref, zb_ref, wa_ref, ba_ref, of_ref, ob_ref, q_s, k_s, v_s, lf_s, st_s,
              *, tt, ch, heads, dk, dv):
    @pl.when(pl.program_id(1) == 0)
    def _():
        st_s[...] = jnp.zeros_like(st_s)

    hk = heads * dk
    for d, (x_ref, z_ref) in enumerate(((qf_ref, zf_ref), (qb_ref, zb_ref))):
        q_s[d] = x_ref[:, :hk] * dk ** -0.5
        k_s[d] = x_ref[:, hk:2 * hk]
        v_s[d] = x_ref[:, 2 * hk:]
        la = jnp.dot(z_ref[...].astype(BF16), wa_ref[d], preferred_element_type=F32) + ba_ref[d]
        lf_s[d] = jax.nn.log_sigmoid(la) / GLA_TAU
    _chunk_scan(q_s, k_s, v_s, lf_s, st_s, (of_ref, ob_ref), tt=tt, ch=ch, heads=heads, dk=dk, dv=dv)


def _hgrn_body(xf_ref, xb_ref, lb_ref, of_ref, ob_ref, q_s, k_s, v_s, lf_s, st_s, *, tt, ch, heads, dk, dv):
    @pl.when(pl.program_id(1) == 0)
    def _():
        st_s[...] = jnp.zeros_like(st_s)

    hk = heads * dk
    lb = lb_ref[...]
    for d, x_ref in enumerate((xf_ref, xb_ref)):
        z = x_ref[:, d * hk:(d + 1) * hk]
        q_s[d] = jax.nn.silu(x_ref[:, 2 * hk:3 * hk]) * dk ** -0.5
        v_s[d] = x_ref[:, 3 * hk:]
        k_s[d] = (1.0 - lb) * jax.nn.sigmoid(-z)
        lf_s[d] = jnp.log(lb + (1.0 - lb) * jax.nn.sigmoid(z))
    _chunk_scan(q_s, k_s, v_s, lf_s, st_s, (of_ref, ob_ref), tt=tt, ch=ch, heads=heads, dk=dk, dv=dv)


def _chunk_scratch(tt, heads, dk, dv):
    return [pltpu.VMEM((2, tt, heads * dk), F32), pltpu.VMEM((2, tt, heads * dk), F32),
            pltpu.VMEM((2, tt, heads * dv), F32), pltpu.VMEM((2, tt, heads * dk), F32),
            pltpu.VMEM((2, heads, dv, dk), F32)]


def _gla(pc, za, wa, ba, nb, seq, ctx, tt, ch, heads, dk, dv):
    t = pc.shape[0]
    geo = (nb, seq, ctx, tt)
    wide = 2 * heads * dk + heads * dv
    full = lambda shape: pl.BlockSpec(shape, lambda b, i: (0,) * len(shape))
    return pl.pallas_call(
        functools.partial(_gla_body, tt=tt, ch=ch, heads=heads, dk=dk, dv=dv),
        grid=(nb, (seq + ctx) // tt),
        in_specs=[_seq_spec(wide, 0, False, *geo), _seq_spec(za.shape[1], 0, False, *geo),
                  _seq_spec(wide, 0, True, *geo), _seq_spec(za.shape[1], 0, True, *geo),
                  full(wa.shape), full(ba.shape)],
        out_specs=[_seq_spec(heads * dv, 0, False, *geo), _seq_spec(heads * dv, 0, True, *geo)],
        out_shape=[jax.ShapeDtypeStruct((t, heads * dv), F32)] * 2,
        scratch_shapes=_chunk_scratch(tt, heads, dk, dv),
        compiler_params=_params(2),
        name="gla_scan",
    )(pc, za, pc, za, wa, ba)


def _hgrn(pd, lb, nb, seq, ctx, tt, ch, heads, dk, dv):
    t = pd.shape[0]
    geo = (nb, seq, ctx, tt)
    wide = 3 * heads * dk + heads * dv
    return pl.pallas_call(
        functools.partial(_hgrn_body, tt=tt, ch=ch, heads=heads, dk=dk, dv=dv),
        grid=(nb, (seq + ctx) // tt),
        in_specs=[_seq_spec(wide, 0, False, *geo), _seq_spec(wide, 0, True, *geo),
                  pl.BlockSpec(lb.shape, lambda b, i: (0, 0))],
        out_specs=[_seq_spec(heads * dv, 0, False, *geo), _seq_spec(heads * dv, 0, True, *geo)],
        out_shape=[jax.ShapeDtypeStruct((t, heads * dv), F32)] * 2,
        scratch_shapes=_chunk_scratch(tt, heads, dk, dv),
        compiler_params=_params(2),
        name="hgrn_scan",
    )(pd, pd, lb)


def _headnorm_body(of_ref, ob_ref, g_ref, n_ref, o_ref, *, heads, dv):
    o = of_ref[...] + ob_ref[...]
    parts = [_rms(o[:, h * dv:(h + 1) * dv]) for h in range(heads)]
    y = jnp.concatenate(parts, axis=-1) * n_ref[...]
    o_ref[...] = (y * jax.nn.silu(g_ref[...])).astype(o_ref.dtype)


def _headnorm(of, ob, gate_arr, gate_blk, gain, l, heads, tm):
    t, c = of.shape
    return pl.pallas_call(
        functools.partial(_headnorm_body, heads=heads, dv=c // heads),
        grid=(t // tm,),
        in_specs=[pl.BlockSpec((tm, c), lambda i: (i, 0)), pl.BlockSpec((tm, c), lambda i: (i, 0)),
                  pl.BlockSpec((tm, c), lambda i: (i, gate_blk)), pl.BlockSpec((None, 1, c), lambda i: (l, 0, 0))],
        out_specs=pl.BlockSpec((tm, c), lambda i: (i, 0)),
        out_shape=jax.ShapeDtypeStruct((t, c), BF16),
        compiler_params=_params(1),
        name="headnorm_gate",
    )(of, ob, gate_arr, gain)


def _merge_body(ya_ref, yb_ref, yc_ref, yd_ref, g0_ref, g1_ref, g2_ref, g3_ref, w_ref, o_ref, wb_ref):
    @pl.when(pl.program_id(1) == 0)
    def _():
        wb_ref[...] = w_ref[...].astype(BF16)

    acc = None
    for k, (y_ref, g_ref) in enumerate(((ya_ref, g0_ref), (yb_ref, g1_ref), (yc_ref, g2_ref), (yd_ref, g3_ref))):
        term = jax.nn.sigmoid(g_ref[...]) * jnp.dot(y_ref[...], wb_ref[k], preferred_element_type=F32)
        acc = term if acc is None else acc + term
    o_ref[...] = acc.astype(o_ref.dtype)


def _merge(ys, pg, w_branch, l, m, tm):
    c = ys[0].shape[1]
    d = w_branch.shape[3]
    tn = 512
    nj = d // tn
    y_spec = pl.BlockSpec((tm, c), lambda j, i: (i, 0))
    g_specs = [pl.BlockSpec((tm, tn), functools.partial(lambda j, i, k: (i, k * nj + j), k=k)) for k in range(N_BRANCH)]
    return pl.pallas_call(
        _merge_body,
        grid=(nj, m // tm),
        in_specs=[y_spec] * N_BRANCH + g_specs + [pl.BlockSpec((None, N_BRANCH, c, tn), lambda j, i: (l, 0, 0, j))],
        out_specs=pl.BlockSpec((tm, tn), lambda j, i: (i, j)),
        out_shape=jax.ShapeDtypeStruct((m, d), BF16),
        scratch_shapes=[pltpu.VMEM((N_BRANCH, c, tn), BF16)],
        compiler_params=_params(2),
        name="branch_merge",
    )(*ys, pg, pg, pg, pg, w_branch)


def _resid_body(x_ref, y_ref, g_ref, mod_ref, xo_ref, *f_refs, gate_i, gy, gf, shift_i, scale_i, y_transposed):
    y = y_ref[...].T if y_transposed else y_ref[...]
    xn = x_ref[...] + mod_ref[gate_i:gate_i + 1, :] * (_rms(y) * g_ref[gy:gy + 1, :])
    xo_ref[...] = xn
    if f_refs:
        f = _rms(xn) * g_ref[gf:gf + 1, :]
        f = f * (1.0 + mod_ref[scale_i:scale_i + 1, :]) + mod_ref[shift_i:shift_i + 1, :]
        f_refs[0][...] = f.astype(BF16)
        f_refs[1][...] = f.T.astype(BF16)


def _resid(x, y, gain, mod, l, m, seq, nb, tm, gate_i, gy, with_f, y_transposed):
    d = x.shape[1]
    row = pl.BlockSpec((tm, d), lambda i: (i, 0))
    col = pl.BlockSpec((d, tm), lambda i: (0, i))
    out_shape = [jax.ShapeDtypeStruct((m, d), F32)]
    out_specs = [row]
    if with_f:
        out_shape += [jax.ShapeDtypeStruct((m, d), BF16), jax.ShapeDtypeStruct((d, m), BF16)]
        out_specs += [row, col]
    return pl.pallas_call(
        functools.partial(_resid_body, gate_i=gate_i, gy=gy, gf=2, shift_i=3, scale_i=4, y_transposed=y_transposed),
        grid=(m // tm,),
        in_specs=[row, col if y_transposed else row, pl.BlockSpec((None, 4, d), lambda i: (l, 0, 0)),
                  pl.BlockSpec((None, N_MOD, d), lambda i: (_mod_row(i, tm, seq, nb), 0, 0))],
        out_specs=out_specs,
        out_shape=out_shape,
        compiler_params=_params(1),
        name="residual",
    )(x, y, gain, mod)


def _top_rows(x, n_rows, vals_ref):
    iota = lax.broadcasted_iota(jnp.int32, x.shape, 0)

    def step(r, x):
        mx = jnp.max(x, axis=0, keepdims=True)
        vals_ref[pl.ds(r, 1), :] = mx
        first = jnp.min(jnp.where(x == mx, iota, n_rows), axis=0, keepdims=True)
        return jnp.where(iota == first, -jnp.inf, x)

    lax.fori_loop(0, PEER_TOPK, step, x)


def _route_body(q_ref, keys_ref, s1_ref, e1_ref, s2_ref, e2_ref, tau_ref, va_s, vb_s, vc_s, *, dq):
    for h in range(PEER_HEADS):
        tops = []
        scores = []
        for p, v_s in enumerate((va_s, vb_s)):
            qh = q_ref[:, h * 2 * dq + p * dq:h * 2 * dq + (p + 1) * dq].astype(BF16)
            s = lax.dot_general(keys_ref[h, p].astype(BF16), qh, NT_DIMS, preferred_element_type=F32)
            _top_rows(s, N_KEYS, v_s)
            scores.append(s)
            tops.append(v_s[...])
        a, b = tops
        half = PEER_TOPK // 2
        cand = jnp.concatenate([a[0:1, :] + b] + [a[r:r + 1, :] + b[0:half, :] for r in range(1, half)]
                               + [a[half:, :] + b[0:1, :]], axis=0)
        _top_rows(cand, cand.shape[0], vc_s)
        best = vc_s[...]
        z = jnp.sum(jnp.exp(best - best[0:1, :]), axis=0, keepdims=True)
        s1_ref[h] = scores[0]
        s2_ref[h] = scores[1]
        e1_ref[h] = jnp.exp(scores[0] - a[0:1, :]) / z
        e2_ref[h] = jnp.exp(scores[1] - b[0:1, :])
        tau_ref[h:h + 1, :] = best[PEER_TOPK - 1:PEER_TOPK, :]


def _route(q, keys, l, tm):
    t = q.shape[0]
    dq = keys.shape[-1]
    big = jax.ShapeDtypeStruct((PEER_HEADS, N_KEYS, t), F32)
    big_spec = pl.BlockSpec((PEER_HEADS, N_KEYS, tm), lambda i: (0, 0, i))
    return pl.pallas_call(
        functools.partial(_route_body, dq=dq),
        grid=(t // tm,),
        in_specs=[pl.BlockSpec((tm, q.shape[1]), lambda i: (i, 0)),
                  pl.BlockSpec((None,) + keys.shape[1:], lambda i: (l, 0, 0, 0, 0))],
        out_specs=[big_spec] * 4 + [pl.BlockSpec((PEER_HEADS, tm), lambda i: (0, i))],
        out_shape=[big] * 4 + [jax.ShapeDtypeStruct((PEER_HEADS, t), F32)],
        scratch_shapes=[pltpu.VMEM((PEER_TOPK, tm), F32)] * 3,
        compiler_params=_params(1),
        name="peer_route",
    )(q, keys)


def _peer_body(ft_ref, u_ref, vt_ref, s1_ref, e1_ref, s2_ref, e2_ref, tau_ref, o_ref, g_s, act_s, *, te, nj):
    j = pl.program_id(1)

    @pl.when(j == 0)
    def _():
        o_ref[...] = jnp.zeros_like(o_ref)
        act_s[...] = jnp.zeros_like(act_s)

    tile = jnp.maximum(j - 1, 0)
    tm = act_s.shape[1]
    for r in range(te // N_KEYS):
        i1 = tile * (te // N_KEYS) + r
        rows = slice(r * N_KEYS, (r + 1) * N_KEYS)
        s1_rows = [s1_ref[h, pl.ds(i1, 1), :] for h in range(PEER_HEADS)]
        e1_rows = [e1_ref[h, pl.ds(i1, 1), :] for h in range(PEER_HEADS)]
        for cb in range(tm // LANES):
            cs = slice(cb * LANES, (cb + 1) * LANES)
            w = None
            for h in range(PEER_HEADS):
                c = s2_ref[h, :, cs] + s1_rows[h][:, cs]
                term = jnp.where(c >= tau_ref[h:h + 1, cs], e2_ref[h, :, cs] * e1_rows[h][:, cs], 0.0)
                w = term if w is None else w + term
            g_s[rows, cs] = (w * act_s[rows, cs]).astype(BF16)

    o_ref[...] += jnp.dot(vt_ref[...], g_s[...], preferred_element_type=F32)
    act_s[...] = jax.nn.gelu(jnp.dot(u_ref[...], ft_ref[...], preferred_element_type=F32))


def _peer(ft, u, vt, l, s1, e1, s2, e2, tau, tm, te):
    d, t = ft.shape
    nj = u.shape[1] // te
    once = pl.Buffered(1)
    big_spec = pl.BlockSpec((PEER_HEADS, N_KEYS, tm), lambda i, j: (0, 0, i), pipeline_mode=once)
    return pl.pallas_call(
        functools.partial(_peer_body, te=te, nj=nj),
        grid=(t // tm, nj + 1),
        in_specs=[pl.BlockSpec((d, tm), lambda i, j: (0, i), pipeline_mode=once),
                  pl.BlockSpec((None, te, d), lambda i, j: (l, jnp.minimum(j, nj - 1), 0)),
                  pl.BlockSpec((None, d, te), lambda i, j: (l, 0, jnp.maximum(j - 1, 0))),
                  big_spec, big_spec, big_spec, big_spec,
                  pl.BlockSpec((PEER_HEADS, tm), lambda i, j: (0, i), pipeline_mode=once)],
        out_specs=pl.BlockSpec((d, tm), lambda i, j: (0, i)),
        out_shape=jax.ShapeDtypeStruct((d, t), F32),
        scratch_shapes=[pltpu.VMEM((te, tm), BF16), pltpu.VMEM((te, tm), F32)],
        compiler_params=_params(2),
        name="peer_experts",
    )(ft, u, vt, s1, e1, s2, e2, tau)


def _block_diag(w):
    n, r, c = w.shape
    eye = jnp.eye(n, dtype=w.dtype)
    return (w[:, :, None, :] * eye[:, None, :, None]).reshape(n * r, n * c)


def _lru_params(w_a, b_a, w_i, b_i, lam):
    wd = jnp.stack([jnp.concatenate([_block_diag(w_a[d]), _block_diag(w_i[d])], axis=1) for d in range(2)])
    bd = jnp.stack([jnp.concatenate([b_a[d], b_i[d]])[None, :] for d in range(2)])
    sp = jax.nn.softplus(-lam)[:, None, :]
    return wd.astype(BF16), bd, sp


def _s5_params(a_re, a_im, log_dt, b_re, b_im, c_re, c_im):
    g, p = a_re.shape[1:]
    gc = g // S5_CHUNKS
    outs = [[] for _ in range(6)]
    for d in range(2):
        big_a = lax.complex(a_re[d], a_im[d])
        dt = jnp.exp(log_dt[d])[:, None]
        a_bar = jnp.exp(big_a * dt)
        b_bar = ((a_bar - 1.0) / big_a)[..., None] * lax.complex(b_re[d], b_im[d])
        c_mat = lax.complex(c_re[d], c_im[d])
        bm = jnp.swapaxes(b_bar, 1, 2).reshape(S5_CHUNKS, gc, S5_GROUP, p)
        cm = jnp.swapaxes(c_mat, 1, 2).reshape(S5_CHUNKS, gc, p, S5_GROUP)
        bm = jnp.stack([_block_diag(bm[k]) for k in range(S5_CHUNKS)])
        cm = jnp.stack([_block_diag(cm[k]) for k in range(S5_CHUNKS)])
        pows = jnp.stack([jnp.exp(big_a * dt * float(n)) for n in range(1, SUBLANES + 1)]).reshape(SUBLANES, g * p)
        row = jnp.arange(SUBLANES)[:, None]
        dbl = jnp.stack([jnp.where((row < SUBLANES - s) if d == 1 else (row >= s), pows[s - 1][None, :], 0.0)
                         for s in (1, 2, 4)])
        pw = pows[::-1] if d == 1 else pows
        for lst, val in zip(outs, (jnp.real(bm), jnp.imag(bm), jnp.real(cm), jnp.imag(cm),
                                   jnp.stack([jnp.real(dbl), jnp.imag(dbl)]), jnp.stack([jnp.real(pw), jnp.imag(pw)]))):
            lst.append(val)
    bre, bim, cre, cim, dbl, pw = (jnp.stack(o) for o in outs)
    return bre.astype(BF16), bim.astype(BF16), cre.astype(BF16), cim.astype(BF16), dbl, pw


def _to_col_major(t, nb, seq):
    rows = seq // GRID_W
    lat = t[:nb * seq].reshape(nb, rows, GRID_W, -1).transpose(0, 2, 1, 3).reshape(nb * seq, -1)
    return jnp.concatenate([lat, t[nb * seq:]], axis=0)


def _to_row_major(t, nb, seq):
    rows = seq // GRID_W
    lat = t[:nb * seq].reshape(nb, GRID_W, rows, -1).transpose(0, 2, 1, 3).reshape(nb * seq, -1)
    return jnp.concatenate([lat, t[nb * seq:]], axis=0)


def kernel(x, c, ctx, c_ctx, w_ada, b_ada, norm_gain, w_in, lru_conv_w, lru_conv_b, lru_w_a, lru_b_a, lru_w_i,
           lru_b_i, lru_lambda, s5_a_re, s5_a_im, s5_log_dt, s5_b_re, s5_b_im, s5_c_re, s5_c_im, s5_d, s5_w_glu,
           gla_w_alpha, gla_b_alpha, gla_norm, hgrn_lb_logits, hgrn_norm, w_branch, w_out, peer_w_q, peer_keys,
           peer_u, peer_v):
    nb, seq, d = x.shape
    nctx = ctx.shape[1]
    depth = w_ada.shape[0]
    mix = d // 4
    n_lat = nb * seq
    n_tok = n_lat + nb * nctx
    assert nb + 1 <= SUBLANES and seq % GRID_W == 0
    gla_dk, gla_dv = mix // 2 // GLA_HEADS, mix // GLA_HEADS
    hg_dk = hg_dv = mix // HGRN_HEADS
    gla_rank = gla_w_alpha.shape[2]
    tm = _tile(math.gcd(seq, nb * nctx), (512, 256, 128))
    tr = min(tm, 256)
    tw = min(tm, 256)
    tt = _tile(math.gcd(seq, nctx), (256, 128, 64, 32))
    ch = 32

    w_in_t = jnp.swapaxes(w_in, 1, 2)
    o_c = 3 * mix
    o_ca = o_c + 2 * GLA_HEADS * gla_dk + GLA_HEADS * gla_dv + mix
    o_dg = 2 * HGRN_HEADS * hg_dk + HGRN_HEADS * hg_dk + HGRN_HEADS * hg_dv
    o_gt = o_dg + mix
    za_w = 128

    p_lb = jax.nn.softmax(hgrn_lb_logits.astype(F32), axis=0)
    lower = jnp.cumsum(p_lb, axis=0) - p_lb[0]
    cvec = jnp.zeros((SUBLANES, d), F32).at[:nb].set(c).at[nb].set(c_ctx)
    b_ada3 = b_ada[:, None, :]
    u_tab = peer_u.astype(BF16)
    vt_tab = jnp.swapaxes(peer_v, 1, 2).astype(BF16)

    xs = jnp.concatenate([x.reshape(n_lat, d), ctx.reshape(nb * nctx, d)], axis=0)
    for l in range(depth):
        last = l == depth - 1
        m_out = n_lat if last else n_tok
        mod = _adaln(cvec, w_ada, b_ada3, l).reshape(SUBLANES, N_MOD, d)
        h = _normmod(xs, norm_gain, mod, l, seq, nb, tr)

        mm_in = lambda off, n, shift, tn, name: _mmt(h, w_in_t, l, off, n, shift, tn, n_tok, tw, name)
        pa = mm_in(0, 2 * mix, 0, mix, "proj_lru")
        pb = mm_in(2 * mix, mix, 0, mix, "proj_s5")
        pc = mm_in(o_c, o_ca - o_c - mix, 0, mix, "proj_gla")
        pcg = mm_in(o_ca - mix, mix, 0, mix, "proj_gla_gate")
        za = mm_in(o_ca, za_w, 0, za_w, "proj_gla_rank")
        pd = mm_in(o_ca, o_gt, gla_rank, mix, "proj_hgrn")
        pg = mm_in(o_ca + o_gt, N_BRANCH * d, gla_rank, mix, "proj_gate")

        wd, bd, sp = _lru_params(lru_w_a[l], lru_b_a[l], lru_w_i[l], lru_b_i[l], lru_lambda[l])
        hf, hb = _lru(pa, lru_conv_w[l], lru_conv_b[l][None, :], wd, bd, sp, nb, seq, nctx, tt)
        ya = _lru_out(pa, hf, hb, tm)

        s5p = _s5_params(s5_a_re[l], s5_a_im[l], s5_log_dt[l], s5_b_re[l], s5_b_im[l], s5_c_re[l], s5_c_im[l])
        ub = _to_col_major(pb, nb, seq)
        yf, ybk = _s5(ub, *s5p, nb, seq, nctx, tt)
        yb = _to_row_major(_s5_glu(ub, yf, ybk, s5_d[:, None, :], s5_w_glu, l, tm), nb, seq)

        wa = jnp.zeros((2, za_w, GLA_HEADS * gla_dk), F32).at[:, :gla_rank].set(gla_w_alpha[l]).astype(BF16)
        of, ob = _gla(pc, za, wa, gla_b_alpha[l][:, None, :], nb, seq, nctx, tt, ch, GLA_HEADS, gla_dk, gla_dv)
        yc = _headnorm(of, ob, pcg, 0, gla_norm[:, None, :], l, GLA_HEADS, tm)

        pdc = _to_col_major(pd, nb, seq)
        of, ob = _hgrn(pdc, lower[l][None, :], nb, seq, nctx, tt, ch, HGRN_HEADS, hg_dk, hg_dv)
        yd = _to_row_major(_headnorm(of, ob, pdc, o_dg // mix, hgrn_norm[:, None, :], l, HGRN_HEADS, tm), nb, seq)

        zm = _merge((ya, yb, yc, yd), pg, w_branch, l, m_out, tm)
        mo = _mm(zm, w_out, l, d, F32, m_out, tw, mix, "proj_out")
        xs, f, ft = _resid(xs, mo, norm_gain, mod, l, m_out, seq, nb, tr, 2, 1, True, False)

        q = _mm(f, peer_w_q, l, peer_w_q.shape[2], F32, m_out, tw, mix, "peer_query")
        s1, e1, s2, e2, tau = _route(q, peer_keys, l, _tile(m_out, (256, 128)))
        yt = _peer(ft, u_tab, vt_tab, l, s1, e1, s2, e2, tau, _tile(m_out, (512, 256, 128)), 512)
        xs = _resid(xs, yt, norm_gain, mod, l, m_out, seq, nb, tr, 5, 3, False, True)[0]
    return xs[:n_lat].reshape(nb, seq, d)
```

```python
import functools
import math

import jax
import jax.numpy as jnp
from jax import lax
from jax.experimental import pallas as pl
from jax.experimental.pallas import tpu as pltpu

F32 = jnp.float32
BF16 = jnp.bfloat16
EPS = 1e-6
GRID_W = 64
N_MOD = 6
N_BRANCH = 4
LRU_BLOCKS = 16
LRU_C = 8.0
S5_GROUP = 16
S5_STATE = 64
S5_CHUNKS = 4
GLA_HEADS = 4
GLA_TAU = 16.0
HGRN_HEADS = 8
PEER_HEADS = 8
N_KEYS = 128
PEER_TOPK = 16
NO_RANK = 255.0
SUBLANES = 8
LANES = 128
VMEM_LIMIT = 56 * 1024 * 1024

NT_DIMS = (((1,), (1,)), ((), ()))
TN_DIMS = (((0,), (0,)), ((), ()))


def _params(n_axes, vmem=VMEM_LIMIT):
    return pltpu.CompilerParams(dimension_semantics=("arbitrary",) * n_axes, vmem_limit_bytes=vmem)


def _tile(n, prefs):
    for p in prefs:
        if n % p == 0:
            return p
    raise ValueError(f"no tile for {n} in {prefs}")


def _rms(x):
    return x * lax.rsqrt(jnp.mean(x * x, axis=-1, keepdims=True) + EPS)


def _ada_body(c_ref, w_ref, b_ref, o_ref):
    c = c_ref[...]
    s = (c * jax.nn.sigmoid(c)).astype(BF16)
    o_ref[...] = jnp.dot(s, w_ref[...].astype(BF16), preferred_element_type=F32) + b_ref[...]


def _adaln(cvec, w_ada, b_ada3, l):
    rows, d = cvec.shape
    n = w_ada.shape[2]
    tn = 512
    return pl.pallas_call(
        _ada_body,
        grid=(n // tn,),
        in_specs=[
            pl.BlockSpec((rows, d), lambda j: (0, 0)),
            pl.BlockSpec((None, d, tn), lambda j: (l, 0, j)),
            pl.BlockSpec((None, 1, tn), lambda j: (l, 0, j)),
        ],
        out_specs=pl.BlockSpec((rows, tn), lambda j: (0, j)),
        out_shape=jax.ShapeDtypeStruct((rows, n), F32),
        compiler_params=_params(1),
        name="adaln",
    )(cvec, w_ada, b_ada3)


def _normmod_body(x_ref, g_ref, mod_ref, o_ref, *, gi, shift_i, scale_i):
    y = _rms(x_ref[...]) * g_ref[gi:gi + 1, :]
    o_ref[...] = (y * (1.0 + mod_ref[scale_i:scale_i + 1, :]) + mod_ref[shift_i:shift_i + 1, :]).astype(o_ref.dtype)


def _mod_row(i, tm, seq, nb):
    return jnp.minimum((i * tm) // seq, nb)


def _normmod(x, gain, mod, l, seq, nb, tm):
    t, d = x.shape
    return pl.pallas_call(
        functools.partial(_normmod_body, gi=0, shift_i=0, scale_i=1),
        grid=(t // tm,),
        in_specs=[
            pl.BlockSpec((tm, d), lambda i: (i, 0)),
            pl.BlockSpec((None, 4, d), lambda i: (l, 0, 0)),
            pl.BlockSpec((None, N_MOD, d), lambda i: (_mod_row(i, tm, seq, nb), 0, 0)),
        ],
        out_specs=pl.BlockSpec((tm, d), lambda i: (i, 0)),
        out_shape=jax.ShapeDtypeStruct((t, d), BF16),
        compiler_params=_params(1),
        name="normmod",
    )(x, gain, mod)


def _mm_body(a_ref, w_ref, o_ref, wb_ref):
    @pl.when(pl.program_id(1) == 0)
    def _():
        wb_ref[...] = w_ref[...].astype(BF16)

    o_ref[...] = jnp.dot(a_ref[...], wb_ref[...], preferred_element_type=F32).astype(o_ref.dtype)


def _mm(a, w, l, ncols, out_dtype, m, tm, tn, name):
    k = a.shape[1]
    w_spec = pl.BlockSpec((None, k, tn), lambda j, i: (l, 0, j))
    return pl.pallas_call(
        _mm_body,
        grid=(ncols // tn, m // tm),
        in_specs=[pl.BlockSpec((tm, k), lambda j, i: (i, 0)), w_spec],
        out_specs=pl.BlockSpec((tm, tn), lambda j, i: (i, j)),
        out_shape=jax.ShapeDtypeStruct((m, ncols), out_dtype),
        scratch_shapes=[pltpu.VMEM((k, tn), BF16)],
        compiler_params=_params(2),
        name=name,
    )(a, w)


def _mmt_body(a_ref, w_ref, *rest, shift):
    if shift:
        tail_ref, o_ref, wb_ref = rest
    else:
        o_ref, wb_ref = rest
    tn = wb_ref.shape[0]

    @pl.when(pl.program_id(1) == 0)
    def _():
        if shift:
            wb_ref[0:tn - shift, :] = w_ref[shift:tn, :].astype(BF16)
            wb_ref[tn - shift:tn, :] = tail_ref[...].astype(BF16)
        else:
            wb_ref[...] = w_ref[...].astype(BF16)

    o_ref[...] = lax.dot_general(a_ref[...], wb_ref[...], NT_DIMS, preferred_element_type=F32).astype(o_ref.dtype)


def _mmt(a, wt, l, row_off, ncols, shift, tn, m, tm, name):
    k = a.shape[1]
    base = row_off // tn
    in_specs = [pl.BlockSpec((tm, k), lambda j, i: (i, 0)), pl.BlockSpec((None, tn, k), lambda j, i: (l, base + j, 0))]
    args = [a, wt]
    if shift:
        per = tn // shift
        in_specs.append(pl.BlockSpec((None, shift, k), lambda j, i: (l, (base + j + 1) * per, 0)))
        args.append(wt)
    return pl.pallas_call(
        functools.partial(_mmt_body, shift=shift),
        grid=(ncols // tn, m // tm),
        in_specs=in_specs,
        out_specs=pl.BlockSpec((tm, tn), lambda j, i: (i, j)),
        out_shape=jax.ShapeDtypeStruct((m, ncols), F32),
        scratch_shapes=[pltpu.VMEM((tn, k), BF16)],
        compiler_params=_params(2),
        name=name,
    )(*args)


def _seq_block(b, i, rev, nb, seq, ctx, tt):
    nctx, nlat = ctx // tt, seq // tt
    ic = (nctx - 1 - i) if rev else i
    il = (nlat - 1 - (i - nctx)) if rev else (i - nctx)
    return jnp.where(i < nctx, (nb * seq) // tt + b * nctx + ic, b * nlat + il)


def _seq_spec(cols, col_blk, rev, nb, seq, ctx, tt):
    return pl.BlockSpec((tt, cols), lambda b, i: (_seq_block(b, i, rev, nb, seq, ctx, tt), col_blk))


def _halo_spec(cols, col_blk, rev, side, nb, seq, ctx, tt, total):
    per = tt // SUBLANES
    last = total // SUBLANES - 1

    def index(b, i):
        blk = _seq_block(b, i, rev, nb, seq, ctx, tt) * per
        blk = blk - 1 if side < 0 else blk + per
        return jnp.clip(blk, 0, last), col_blk

    return pl.BlockSpec((SUBLANES, cols), index)


def _stream_pos(i, rev, seq, ctx, tt):
    nctx, nlat = ctx // tt, seq // tt
    is_ctx = i < nctx
    ii = jnp.where(is_ctx, (nctx - 1 - i) if rev else i, (nlat - 1 - (i - nctx)) if rev else (i - nctx))
    n = jnp.where(is_ctx, nctx, nlat)
    return ii == 0, ii == n - 1


def _scan_tile_real(a, u, carry, rev):
    row = lax.broadcasted_iota(jnp.int32, a.shape, 0)
    for s in (1, 2, 4):
        if rev:
            a_sh, u_sh = pltpu.roll(a, SUBLANES - s, 0), pltpu.roll(u, SUBLANES - s, 0)
            ok = row < SUBLANES - s
        else:
            a_sh, u_sh = pltpu.roll(a, s, 0), pltpu.roll(u, s, 0)
            ok = row >= s
        u = jnp.where(ok, a * u_sh + u, u)
        a = jnp.where(ok, a * a_sh, a)
    h = u + a * carry
    return h, (h[0:1] if rev else h[SUBLANES - 1:SUBLANES])


def _lru_body(xf_ref, pf_ref, nf_ref, xb_ref, pb_ref, nb_ref, cw_ref, cb_ref, wd_ref, bd_ref, sp_ref,
              hf_ref, hb_ref, a_s, u_s, carry_s, *, seq, ctx, tt, c):
    i = pl.program_id(1)

    @pl.when(i == 0)
    def _():
        carry_s[...] = jnp.zeros_like(carry_s)

    row = lax.broadcasted_iota(jnp.int32, (tt, c), 0)
    for d, (x_ref, p_ref, n_ref) in enumerate(((xf_ref, pf_ref, nf_ref), (xb_ref, pb_ref, nb_ref))):
        first, last = _stream_pos(i, d == 1, seq, ctx, tt)
        x = x_ref[...]
        prev = jnp.where(first, 0.0, p_ref[SUBLANES - 1:SUBLANES, :])
        nx1 = jnp.where(last, 0.0, n_ref[0:1, :])
        nx2 = jnp.where(last, 0.0, n_ref[1:2, :])
        xm1 = jnp.where(row == 0, prev, pltpu.roll(x, 1, 0))
        xp1 = jnp.where(row == tt - 1, nx1, pltpu.roll(x, tt - 1, 0))
        xp2 = jnp.where(row == tt - 1, nx2, jnp.where(row == tt - 2, nx1, pltpu.roll(x, tt - 2, 0)))
        xl = xm1 * cw_ref[0:1, :] + x * cw_ref[1:2, :] + xp1 * cw_ref[2:3, :] + xp2 * cw_ref[3:4, :] + cb_ref[...]
        z = jnp.dot(xl.astype(BF16), wd_ref[d], preferred_element_type=F32) + bd_ref[d]
        r = jax.nn.sigmoid(z[:, :c])
        gi = jax.nn.sigmoid(z[:, c:])
        log_a = -LRU_C * r * sp_ref[d]
        a = jnp.exp(log_a)
        a_s[d] = a
        u_s[d] = jnp.sqrt(-jnp.tanh(log_a) * (a * a + 1.0)) * gi * xl

    ntile = tt // SUBLANES

    def step(k, carry):
        cf, cb = carry
        rows_f = pl.ds(pl.multiple_of(k * SUBLANES, SUBLANES), SUBLANES)
        rows_b = pl.ds(pl.multiple_of((ntile - 1 - k) * SUBLANES, SUBLANES), SUBLANES)
        h, cf = _scan_tile_real(a_s[0, rows_f, :], u_s[0, rows_f, :], cf, False)
        hf_ref[rows_f, :] = h
        h, cb = _scan_tile_real(a_s[1, rows_b, :], u_s[1, rows_b, :], cb, True)
        hb_ref[rows_b, :] = h
        return cf, cb

    cf, cb = lax.fori_loop(0, ntile, step, (carry_s[0, 0:1, :], carry_s[1, 0:1, :]))
    carry_s[0, 0:1, :] = cf
    carry_s[1, 0:1, :] = cb


def _lru(pa, cw, cb, wd, bd, sp, nb, seq, ctx, tt):
    t = pa.shape[0]
    c = cw.shape[1]
    geo = (nb, seq, ctx, tt)
    full = lambda shape: pl.BlockSpec(shape, lambda b, i: (0,) * len(shape))
    return pl.pallas_call(
        functools.partial(_lru_body, seq=seq, ctx=ctx, tt=tt, c=c),
        grid=(nb, (seq + ctx) // tt),
        in_specs=[
            _seq_spec(c, 0, False, *geo), _halo_spec(c, 0, False, -1, *geo, t), _halo_spec(c, 0, False, 1, *geo, t),
            _seq_spec(c, 0, True, *geo), _halo_spec(c, 0, True, -1, *geo, t), _halo_spec(c, 0, True, 1, *geo, t),
            full(cw.shape), full(cb.shape), full(wd.shape), full(bd.shape), full(sp.shape),
        ],
        out_specs=[_seq_spec(c, 0, False, *geo), _seq_spec(c, 0, True, *geo)],
        out_shape=[jax.ShapeDtypeStruct((t, c), F32)] * 2,
        scratch_shapes=[pltpu.VMEM((2, tt, c), F32), pltpu.VMEM((2, tt, c), F32), pltpu.VMEM((2, SUBLANES, c), F32)],
        compiler_params=_params(2),
        name="rglru_scan",
    )(pa, pa, pa, pa, pa, pa, cw, cb, wd, bd, sp)


def _lru_out_body(ay_ref, hf_ref, hb_ref, o_ref):
    o_ref[...] = (jax.nn.gelu(ay_ref[...]) * (hf_ref[...] + hb_ref[...])).astype(o_ref.dtype)


def _lru_out(pa, hf, hb, tm):
    t, c = hf.shape
    return pl.pallas_call(
        _lru_out_body,
        grid=(t // tm,),
        in_specs=[pl.BlockSpec((tm, c), lambda i: (i, 1)), pl.BlockSpec((tm, c), lambda i: (i, 0)),
                  pl.BlockSpec((tm, c), lambda i: (i, 0))],
        out_specs=pl.BlockSpec((tm, c), lambda i: (i, 0)),
        out_shape=jax.ShapeDtypeStruct((t, c), BF16),
        compiler_params=_params(1),
        name="rglru_out",
    )(pa, hf, hb)


def _scan_tile_cplx(xr, xi, dbl_ref, pw_ref, d, lanes, cr, ci, rev):
    for n, s in enumerate((1, 2, 4)):
        ar = dbl_ref[d, 0, n, :, lanes]
        ai = dbl_ref[d, 1, n, :, lanes]
        shift = SUBLANES - s if rev else s
        sr, si = pltpu.roll(xr, shift, 0), pltpu.roll(xi, shift, 0)
        xr, xi = xr + ar * sr - ai * si, xi + ar * si + ai * sr
    pr = pw_ref[d, 0, :, lanes]
    pi = pw_ref[d, 1, :, lanes]
    hr = xr + pr * cr - pi * ci
    hi = xi + pr * ci + pi * cr
    sel = slice(0, 1) if rev else slice(SUBLANES - 1, SUBLANES)
    return hr, hi, hr[sel], hi[sel]


def _s5_body(uf_ref, ub_ref, bre_ref, bim_ref, cre_ref, cim_ref, dbl_ref, pw_ref, yf_ref, yb_ref,
             hr_s, hi_s, carry_s, *, tt, gw, sw):
    i = pl.program_id(1)

    @pl.when(i == 0)
    def _():
        carry_s[...] = jnp.zeros_like(carry_s)

    ntile = tt // SUBLANES
    for ch in range(S5_CHUNKS):
        cols = slice(ch * gw, (ch + 1) * gw)
        lanes = slice(ch * sw, (ch + 1) * sw)
        for d, u_ref in enumerate((uf_ref, ub_ref)):
            u = u_ref[:, cols].astype(BF16)
            hr_s[d] = jnp.dot(u, bre_ref[d, ch], preferred_element_type=F32)
            hi_s[d] = jnp.dot(u, bim_ref[d, ch], preferred_element_type=F32)

        def step(k, carry):
            crf, cif, crb, cib = carry
            rows_f = pl.ds(pl.multiple_of(k * SUBLANES, SUBLANES), SUBLANES)
            rows_b = pl.ds(pl.multiple_of((ntile - 1 - k) * SUBLANES, SUBLANES), SUBLANES)
            hr, hi, crf, cif = _scan_tile_cplx(hr_s[0, rows_f, :], hi_s[0, rows_f, :], dbl_ref, pw_ref, 0, lanes,
                                               crf, cif, False)
            hr_s[0, rows_f, :] = hr
            hi_s[0, rows_f, :] = hi
            hr, hi, crb, cib = _scan_tile_cplx(hr_s[1, rows_b, :], hi_s[1, rows_b, :], dbl_ref, pw_ref, 1, lanes,
                                               crb, cib, True)
            hr_s[1, rows_b, :] = hr
            hi_s[1, rows_b, :] = hi
            return crf, cif, crb, cib

        init = tuple(carry_s[n, 0:1, lanes] for n in range(4))
        fin = lax.fori_loop(0, ntile, step, init)
        for n in range(4):
            carry_s[n, 0:1, lanes] = fin[n]
        for d, y_ref in enumerate((yf_ref, yb_ref)):
            y_ref[:, cols] = (jnp.dot(hr_s[d].astype(BF16), cre_ref[d, ch], preferred_element_type=F32)
                              - jnp.dot(hi_s[d].astype(BF16), cim_ref[d, ch], preferred_element_type=F32))


def _s5(u, bre, bim, cre, cim, dbl, pw, nb, seq, ctx, tt):
    t, c = u.shape
    gw = c // S5_CHUNKS
    sw = bre.shape[-1]
    geo = (nb, seq, ctx, tt)
    full = lambda shape: pl.BlockSpec(shape, lambda b, i: (0,) * len(shape))
    return pl.pallas_call(
        functools.partial(_s5_body, tt=tt, gw=gw, sw=sw),
        grid=(nb, (seq + ctx) // tt),
        in_specs=[_seq_spec(c, 0, False, *geo), _seq_spec(c, 0, True, *geo),
                  full(bre.shape), full(bim.shape), full(cre.shape), full(cim.shape), full(dbl.shape), full(pw.shape)],
        out_specs=[_seq_spec(c, 0, False, *geo), _seq_spec(c, 0, True, *geo)],
        out_shape=[jax.ShapeDtypeStruct((t, c), F32)] * 2,
        scratch_shapes=[pltpu.VMEM((2, tt, sw), F32), pltpu.VMEM((2, tt, sw), F32),
                        pltpu.VMEM((4, SUBLANES, sw * S5_CHUNKS), F32)],
        compiler_params=_params(2),
        name="s5_scan",
    )(u, u, bre, bim, cre, cim, dbl, pw)


def _s5_glu_body(u_ref, yf_ref, yb_ref, d_ref, w_ref, o_ref, wb_ref, *, c):
    @pl.when(pl.program_id(0) == 0)
    def _():
        wb_ref[...] = w_ref[...].astype(BF16)

    y = jax.nn.gelu(u_ref[...] * d_ref[...] + yf_ref[...] + yb_ref[...])
    z = jnp.dot(y.astype(BF16), wb_ref[...], preferred_element_type=F32)
    o_ref[...] = (z[:, :c] * jax.nn.sigmoid(z[:, c:])).astype(o_ref.dtype)


def _s5_glu(u, yf, yb, dskip, w_glu, l, tm):
    t, c = u.shape
    return pl.pallas_call(
        functools.partial(_s5_glu_body, c=c),
        grid=(t // tm,),
        in_specs=[pl.BlockSpec((tm, c), lambda i: (i, 0))] * 3
        + [pl.BlockSpec((None, 1, c), lambda i: (l, 0, 0)), pl.BlockSpec((None, c, 2 * c), lambda i: (l, 0, 0))],
        out_specs=pl.BlockSpec((tm, c), lambda i: (i, 0)),
        out_shape=jax.ShapeDtypeStruct((t, c), BF16),
        scratch_shapes=[pltpu.VMEM((c, 2 * c), BF16)],
        compiler_params=_params(1),
        name="s5_glu",
    )(u, yf, yb, dskip, w_glu)


def _chunk_cumsum(x, rev, ch):
    row = lax.broadcasted_iota(jnp.int32, x.shape, 0)
    s = 1
    while s < ch:
        if rev:
            x = x + jnp.where(row < ch - s, pltpu.roll(x, ch - s, 0), 0.0)
        else:
            x = x + jnp.where(row >= s, pltpu.roll(x, s, 0), 0.0)
        s *= 2
    return x


def _chunk_scan(q_s, k_s, v_s, lf_s, st_s, o_refs, *, tt, ch, heads, dk, dv):
    nchunk = tt // ch
    tri_r = lax.broadcasted_iota(jnp.int32, (ch, ch), 0)
    tri_c = lax.broadcasted_iota(jnp.int32, (ch, ch), 1)

    def step(n, _):
        for d in range(2):
            rev = d == 1
            cc = (nchunk - 1 - n) if rev else n
            rows = pl.ds(pl.multiple_of(cc * ch, ch), ch)
            b = _chunk_cumsum(lf_s[d, rows, :], rev, ch)
            piv = ch // 2 if rev else ch // 2 - 1
            end = 0 if rev else ch - 1
            m = b[piv:piv + 1, :]
            bl = b[end:end + 1, :]
            qm = q_s[d, rows, :] * jnp.exp(b - m)
            km = k_s[d, rows, :] * jnp.exp(m - b)
            qg = (qm * jnp.exp(m)).astype(BF16)
            kg = (km * jnp.exp(bl - m)).astype(BF16)
            dec = jnp.exp(bl)
            qm = qm.astype(BF16)
            km = km.astype(BF16)
            v = v_s[d, rows, :].astype(BF16)
            keep = (tri_r <= tri_c) if rev else (tri_r >= tri_c)
            for h in range(heads):
                ks = slice(h * dk, (h + 1) * dk)
                vs = slice(h * dv, (h + 1) * dv)
                sc = lax.dot_general(qm[:, ks], km[:, ks], NT_DIMS, preferred_element_type=F32)
                sc = jnp.where(keep, sc, 0.0).astype(BF16)
                st = st_s[d, h]
                o = jnp.dot(sc, v[:, vs], preferred_element_type=F32)
                o = o + lax.dot_general(qg[:, ks], st.astype(BF16), NT_DIMS, preferred_element_type=F32)
                st_s[d, h] = st * dec[:, ks] + lax.dot_general(v[:, vs], kg[:, ks], TN_DIMS,
                                                               preferred_element_type=F32)
                o_refs[d][rows, vs] = o
        return 0

    lax.fori_loop(0, nchunk, step, 0)


def _gla_body(qf_ref, zf_ref, qb_ref, zb_ref, wa_ref, ba_ref, of_ref, ob_ref, q_s, k_s, v_s, lf_s, st_s,
              *, tt, ch, heads, dk, dv):
    @pl.when(pl.program_id(1) == 0)
    def _():
        st_s[...] = jnp.zeros_like(st_s)

    hk = heads * dk
    for d, (x_ref, z_ref) in enumerate(((qf_ref, zf_ref), (qb_ref, zb_ref))):
        q_s[d] = x_ref[:, :hk] * dk ** -0.5
        k_s[d] = x_ref[:, hk:2 * hk]
        v_s[d] = x_ref[:, 2 * hk:]
        la = jnp.dot(z_ref[...].astype(BF16), wa_ref[d], preferred_element_type=F32) + ba_ref[d]
        lf_s[d] = jax.nn.log_sigmoid(la) / GLA_TAU
    _chunk_scan(q_s, k_s, v_s, lf_s, st_s, (of_ref, ob_ref), tt=tt, ch=ch, heads=heads, dk=dk, dv=dv)


def _hgrn_body(xf_ref, xb_ref, lb_ref, of_ref, ob_ref, q_s, k_s, v_s, lf_s, st_s, *, tt, ch, heads, dk, dv):
    @pl.when(pl.program_id(1) == 0)
    def _():
        st_s[...] = jnp.zeros_like(st_s)

    hk = heads * dk
    lb = lb_ref[...]
    for d, x_ref in enumerate((xf_ref, xb_ref)):
        z = x_ref[:, d * hk:(d + 1) * hk]
        q_s[d] = jax.nn.silu(x_ref[:, 2 * hk:3 * hk]) * dk ** -0.5
        v_s[d] = x_ref[:, 3 * hk:]
        k_s[d] = (1.0 - lb) * jax.nn.sigmoid(-z)
        lf_s[d] = jnp.log(lb + (1.0 - lb) * jax.nn.sigmoid(z))
    _chunk_scan(q_s, k_s, v_s, lf_s, st_s, (of_ref, ob_ref), tt=tt, ch=ch, heads=heads, dk=dk, dv=dv)


def _chunk_scratch(tt, heads, dk, dv):
    return [pltpu.VMEM((2, tt, heads * dk), F32), pltpu.VMEM((2, tt, heads * dk), F32),
            pltpu.VMEM((2, tt, heads * dv), F32), pltpu.VMEM((2, tt, heads * dk), F32),
            pltpu.VMEM((2, heads, dv, dk), F32)]


def _gla(pc, za, wa, ba, nb, seq, ctx, tt, ch, heads, dk, dv):
    t = pc.shape[0]
    geo = (nb, seq, ctx, tt)
    wide = 2 * heads * dk + heads * dv
    full = lambda shape: pl.BlockSpec(shape, lambda b, i: (0,) * len(shape))
    return pl.pallas_call(
        functools.partial(_gla_body, tt=tt, ch=ch, heads=heads, dk=dk, dv=dv),
        grid=(nb, (seq + ctx) // tt),
        in_specs=[_seq_spec(wide, 0, False, *geo), _seq_spec(za.shape[1], 0, False, *geo),
                  _seq_spec(wide, 0, True, *geo), _seq_spec(za.shape[1], 0, True, *geo),
                  full(wa.shape), full(ba.shape)],
        out_specs=[_seq_spec(heads * dv, 0, False, *geo), _seq_spec(heads * dv, 0, True, *geo)],
        out_shape=[jax.ShapeDtypeStruct((t, heads * dv), F32)] * 2,
        scratch_shapes=_chunk_scratch(tt, heads, dk, dv),
        compiler_params=_params(2),
        name="gla_scan",
    )(pc, za, pc, za, wa, ba)


def _hgrn(pd, lb, nb, seq, ctx, tt, ch, heads, dk, dv):
    t = pd.shape[0]
    geo = (nb, seq, ctx, tt)
    wide = 3 * heads * dk + heads * dv
    return pl.pallas_call(
        functools.partial(_hgrn_body, tt=tt, ch=ch, heads=heads, dk=dk, dv=dv),
        grid=(nb, (seq + ctx) // tt),
        in_specs=[_seq_spec(wide, 0, False, *geo), _seq_spec(wide, 0, True, *geo),
                  pl.BlockSpec(lb.shape, lambda b, i: (0, 0))],
        out_specs=[_seq_spec(heads * dv, 0, False, *geo), _seq_spec(heads * dv, 0, True, *geo)],
        out_shape=[jax.ShapeDtypeStruct((t, heads * dv), F32)] * 2,
        scratch_shapes=_chunk_scratch(tt, heads, dk, dv),
        compiler_params=_params(2),
        name="hgrn_scan",
    )(pd, pd, lb)


def _headnorm_body(of_ref, ob_ref, g_ref, n_ref, o_ref, *, heads, dv):
    o = of_ref[...] + ob_ref[...]
    parts = [_rms(o[:, h * dv:(h + 1) * dv]) for h in range(heads)]
    y = jnp.concatenate(parts, axis=-1) * n_ref[...]
    o_ref[...] = (y * jax.nn.silu(g_ref[...])).astype(o_ref.dtype)


def _headnorm(of, ob, gate_arr, gate_blk, gain, l, heads, tm):
    t, c = of.shape
    return pl.pallas_call(
        functools.partial(_headnorm_body, heads=heads, dv=c // heads),
        grid=(t // tm,),
        in_specs=[pl.BlockSpec((tm, c), lambda i: (i, 0)), pl.BlockSpec((tm, c), lambda i: (i, 0)),
                  pl.BlockSpec((tm, c), lambda i: (i, gate_blk)), pl.BlockSpec((None, 1, c), lambda i: (l, 0, 0))],
        out_specs=pl.BlockSpec((tm, c), lambda i: (i, 0)),
        out_shape=jax.ShapeDtypeStruct((t, c), BF16),
        compiler_params=_params(1),
        name="headnorm_gate",
    )(of, ob, gate_arr, gain)


def _merge_body(ya_ref, yb_ref, yc_ref, yd_ref, g0_ref, g1_ref, g2_ref, g3_ref, w_ref, o_ref, wb_ref):
    @pl.when(pl.program_id(1) == 0)
    def _():
        wb_ref[...] = w_ref[...].astype(BF16)

    acc = None
    for k, (y_ref, g_ref) in enumerate(((ya_ref, g0_ref), (yb_ref, g1_ref), (yc_ref, g2_ref), (yd_ref, g3_ref))):
        term = jax.nn.sigmoid(g_ref[...]) * jnp.dot(y_ref[...], wb_ref[k], preferred_element_type=F32)
        acc = term if acc is None else acc + term
    o_ref[...] = acc.astype(o_ref.dtype)


def _merge(ys, pg, w_branch, l, m, tm):
    c = ys[0].shape[1]
    d = w_branch.shape[3]
    tn = 512
    nj = d // tn
    y_spec = pl.BlockSpec((tm, c), lambda j, i: (i, 0))
    g_specs = [pl.BlockSpec((tm, tn), functools.partial(lambda j, i, k: (i, k * nj + j), k=k)) for k in range(N_BRANCH)]
    return pl.pallas_call(
        _merge_body,
        grid=(nj, m // tm),
        in_specs=[y_spec] * N_BRANCH + g_specs + [pl.BlockSpec((None, N_BRANCH, c, tn), lambda j, i: (l, 0, 0, j))],
        out_specs=pl.BlockSpec((tm, tn), lambda j, i: (i, j)),
        out_shape=jax.ShapeDtypeStruct((m, d), BF16),
        scratch_shapes=[pltpu.VMEM((N_BRANCH, c, tn), BF16)],
        compiler_params=_params(2),
        name="branch_merge",
    )(*ys, pg, pg, pg, pg, w_branch)


def _resid_body(x_ref, y_ref, g_ref, mod_ref, xo_ref, *f_refs, gate_i, gy, gf, shift_i, scale_i, y_transposed):
    y = y_ref[...].T if y_transposed else y_ref[...]
    xn = x_ref[...] + mod_ref[gate_i:gate_i + 1, :] * (_rms(y) * g_ref[gy:gy + 1, :])
    xo_ref[...] = xn
    if f_refs:
        f = _rms(xn) * g_ref[gf:gf + 1, :]
        f = f * (1.0 + mod_ref[scale_i:scale_i + 1, :]) + mod_ref[shift_i:shift_i + 1, :]
        f_refs[0][...] = f.astype(BF16)
        f_refs[1][...] = f.T.astype(BF16)


def _resid(x, y, gain, mod, l, m, seq, nb, tm, gate_i, gy, with_f, y_transposed):
    d = x.shape[1]
    row = pl.BlockSpec((tm, d), lambda i: (i, 0))
    col = pl.BlockSpec((d, tm), lambda i: (0, i))
    out_shape = [jax.ShapeDtypeStruct((m, d), F32)]
    out_specs = [row]
    if with_f:
        out_shape += [jax.ShapeDtypeStruct((m, d), BF16), jax.ShapeDtypeStruct((d, m), BF16)]
        out_specs += [row, col]
    return pl.pallas_call(
        functools.partial(_resid_body, gate_i=gate_i, gy=gy, gf=2, shift_i=3, scale_i=4, y_transposed=y_transposed),
        grid=(m // tm,),
        in_specs=[row, col if y_transposed else row, pl.BlockSpec((None, 4, d), lambda i: (l, 0, 0)),
                  pl.BlockSpec((None, N_MOD, d), lambda i: (_mod_row(i, tm, seq, nb), 0, 0))],
        out_specs=out_specs,
        out_shape=out_shape,
        compiler_params=_params(1),
        name="residual",
    )(x, y, gain, mod)


def _top_rows(x, vals_ref, idx_ref, want_rank):
    n_rows = x.shape[0]
    iota = lax.broadcasted_iota(jnp.int32, x.shape, 0)

    def step(r, carry):
        x = carry[0]
        mx = jnp.max(x, axis=0, keepdims=True)
        vals_ref[pl.ds(r, 1), :] = mx
        first = jnp.min(jnp.where(x == mx, iota, n_rows), axis=0, keepdims=True)
        idx_ref[pl.ds(r, 1), :] = first
        hit = iota == first
        x = jnp.where(hit, -jnp.inf, x)
        return (x, jnp.where(hit, r.astype(F32), carry[1])) if want_rank else (x,)

    init = (x, jnp.full(x.shape, NO_RANK, F32)) if want_rank else (x,)
    return lax.fori_loop(0, PEER_TOPK, step, init)[-1]


def _route_body(q_ref, keys_ref, e1_ref, cnt_ref, rank2_ref, e2_ref, va_s, vb_s, vc_s, ia_s, ib_s, *, dq):
    half = PEER_TOPK // 2
    iota = lax.broadcasted_iota(jnp.int32, (N_KEYS, q_ref.shape[0]), 0)
    for h in range(PEER_HEADS):
        scores = []
        for p in range(2):
            qh = q_ref[:, h * 2 * dq + p * dq:h * 2 * dq + (p + 1) * dq].astype(BF16)
            scores.append(lax.dot_general(keys_ref[h, p].astype(BF16), qh, NT_DIMS, preferred_element_type=F32))
        _top_rows(scores[0], va_s, ia_s, False)
        rank2 = _top_rows(scores[1], vb_s, ib_s, True)
        a, b = va_s[...], vb_s[...]
        cand = jnp.concatenate([a[0:1, :] + b] + [a[r:r + 1, :] + b[0:half, :] for r in range(1, half)]
                               + [a[half:, :] + b[0:1, :]], axis=0)
        _top_rows(cand, vc_s, ib_s, False)
        best = vc_s[...]
        z = jnp.sum(jnp.exp(best - best[0:1, :]), axis=0, keepdims=True)
        row = ib_s[...]
        sel_i = jnp.where(row < PEER_TOPK, 0,
                          jnp.where(row < PEER_TOPK + half * (half - 1), (row - half) // half, row - half * half))
        key_a = ia_s[...]
        cnt = jnp.zeros(scores[0].shape, F32)
        for i in range(PEER_TOPK):
            n_i = jnp.sum((sel_i == i).astype(F32), axis=0, keepdims=True)
            cnt = jnp.where(iota == key_a[i:i + 1, :], n_i, cnt)
        e1_ref[h] = jnp.exp(scores[0] - a[0:1, :]) / z
        cnt_ref[h] = cnt
        rank2_ref[h] = rank2.astype(BF16)
        e2_ref[h] = jnp.exp(scores[1] - b[0:1, :]).astype(BF16)


def _route(q, keys, l, tm):
    t = q.shape[0]
    dq = keys.shape[-1]
    spec = pl.BlockSpec((PEER_HEADS, N_KEYS, tm), lambda i: (0, 0, i))
    return pl.pallas_call(
        functools.partial(_route_body, dq=dq),
        grid=(t // tm,),
        in_specs=[pl.BlockSpec((tm, q.shape[1]), lambda i: (i, 0)),
                  pl.BlockSpec((None,) + keys.shape[1:], lambda i: (l, 0, 0, 0, 0))],
        out_specs=[spec] * 4,
        out_shape=[jax.ShapeDtypeStruct((PEER_HEADS, N_KEYS, t), dt) for dt in (F32, F32, BF16, BF16)],
        scratch_shapes=[pltpu.VMEM((PEER_TOPK, tm), F32)] * 3 + [pltpu.VMEM((PEER_TOPK, tm), jnp.int32)] * 2,
        compiler_params=_params(1),
        name="peer_route",
    )(q, keys)


def _peer_body(ft_ref, u_ref, vt_ref, e1_ref, cnt_ref, rank2_ref, e2_ref, o_ref, g_s, act_s, *, te, nj):
    j = pl.program_id(1)

    @pl.when(j == 0)
    def _():
        o_ref[...] = jnp.zeros_like(o_ref)
        act_s[...] = jnp.zeros_like(act_s)

    tile = jnp.maximum(j - 1, 0)
    tm = act_s.shape[1]
    zero = jnp.zeros((), BF16)
    for r in range(te // N_KEYS):
        i1 = tile * (te // N_KEYS) + r
        rows = slice(r * N_KEYS, (r + 1) * N_KEYS)
        e1_rows = [e1_ref[h, pl.ds(i1, 1), :].astype(BF16) for h in range(PEER_HEADS)]
        cnt_rows = [cnt_ref[h, pl.ds(i1, 1), :].astype(BF16) for h in range(PEER_HEADS)]
        for cb in range(tm // LANES):
            cs = slice(cb * LANES, (cb + 1) * LANES)
            w = None
            for h in range(PEER_HEADS):
                term = jnp.where(rank2_ref[h, :, cs] < cnt_rows[h][:, cs], e2_ref[h, :, cs] * e1_rows[h][:, cs], zero)
                w = term if w is None else w + term
            g_s[rows, cs] = w * act_s[rows, cs].astype(BF16)

    o_ref[...] += jnp.dot(vt_ref[...], g_s[...], preferred_element_type=F32)
    act_s[...] = jax.nn.gelu(jnp.dot(u_ref[...], ft_ref[...], preferred_element_type=F32))


def _peer(ft, u, vt, l, e1, cnt, rank2, e2, tm, te):
    d, t = ft.shape
    nj = u.shape[1] // te
    once = pl.Buffered(1)
    big_spec = pl.BlockSpec((PEER_HEADS, N_KEYS, tm), lambda i, j: (0, 0, i), pipeline_mode=once)
    return pl.pallas_call(
        functools.partial(_peer_body, te=te, nj=nj),
        grid=(t // tm, nj + 1),
        in_specs=[pl.BlockSpec((d, tm), lambda i, j: (0, i), pipeline_mode=once),
                  pl.BlockSpec((None, te, d), lambda i, j: (l, jnp.minimum(j, nj - 1), 0)),
                  pl.BlockSpec((None, d, te), lambda i, j: (l, 0, jnp.maximum(j - 1, 0))),
                  big_spec, big_spec, big_spec, big_spec],
        out_specs=pl.BlockSpec((d, tm), lambda i, j: (0, i)),
        out_shape=jax.ShapeDtypeStruct((d, t), F32),
        scratch_shapes=[pltpu.VMEM((te, tm), BF16), pltpu.VMEM((te, tm), F32)],
        compiler_params=_params(2),
        name="peer_experts",
    )(ft, u, vt, e1, cnt, rank2, e2)


def _block_diag(w):
    n, r, c = w.shape
    eye = jnp.eye(n, dtype=w.dtype)
    return (w[:, :, None, :] * eye[:, None, :, None]).reshape(n * r, n * c)


def _lru_params(w_a, b_a, w_i, b_i, lam):
    wd = jnp.stack([jnp.concatenate([_block_diag(w_a[d]), _block_diag(w_i[d])], axis=1) for d in range(2)])
    bd = jnp.stack([jnp.concatenate([b_a[d], b_i[d]])[None, :] for d in range(2)])
    sp = jax.nn.softplus(-lam)[:, None, :]
    return wd.astype(BF16), bd, sp


def _s5_params(a_re, a_im, log_dt, b_re, b_im, c_re, c_im):
    g, p = a_re.shape[1:]
    gc = g // S5_CHUNKS
    outs = [[] for _ in range(6)]
    for d in range(2):
        big_a = lax.complex(a_re[d], a_im[d])
        dt = jnp.exp(log_dt[d])[:, None]
        a_bar = jnp.exp(big_a * dt)
        b_bar = ((a_bar - 1.0) / big_a)[..., None] * lax.complex(b_re[d], b_im[d])
        c_mat = lax.complex(c_re[d], c_im[d])
        bm = jnp.swapaxes(b_bar, 1, 2).reshape(S5_CHUNKS, gc, S5_GROUP, p)
        cm = jnp.swapaxes(c_mat, 1, 2).reshape(S5_CHUNKS, gc, p, S5_GROUP)
        bm = jnp.stack([_block_diag(bm[k]) for k in range(S5_CHUNKS)])
        cm = jnp.stack([_block_diag(cm[k]) for k in range(S5_CHUNKS)])
        pows = jnp.stack([jnp.exp(big_a * dt * float(n)) for n in range(1, SUBLANES + 1)]).reshape(SUBLANES, g * p)
        row = jnp.arange(SUBLANES)[:, None]
        dbl = jnp.stack([jnp.where((row < SUBLANES - s) if d == 1 else (row >= s), pows[s - 1][None, :], 0.0)
                         for s in (1, 2, 4)])
        pw = pows[::-1] if d == 1 else pows
        for lst, val in zip(outs, (jnp.real(bm), jnp.imag(bm), jnp.real(cm), jnp.imag(cm),
                                   jnp.stack([jnp.real(dbl), jnp.imag(dbl)]), jnp.stack([jnp.real(pw), jnp.imag(pw)]))):
            lst.append(val)
    bre, bim, cre, cim, dbl, pw = (jnp.stack(o) for o in outs)
    return bre.astype(BF16), bim.astype(BF16), cre.astype(BF16), cim.astype(BF16), dbl, pw


def _to_col_major(t, nb, seq):
    rows = seq // GRID_W
    lat = t[:nb * seq].reshape(nb, rows, GRID_W, -1).transpose(0, 2, 1, 3).reshape(nb * seq, -1)
    return jnp.concatenate([lat, t[nb * seq:]], axis=0)


def _to_row_major(t, nb, seq):
    rows = seq // GRID_W
    lat = t[:nb * seq].reshape(nb, GRID_W, rows, -1).transpose(0, 2, 1, 3).reshape(nb * seq, -1)
    return jnp.concatenate([lat, t[nb * seq:]], axis=0)


def kernel(x, c, ctx, c_ctx, w_ada, b_ada, norm_gain, w_in, lru_conv_w, lru_conv_b, lru_w_a, lru_b_a, lru_w_i,
           lru_b_i, lru_lambda, s5_a_re, s5_a_im, s5_log_dt, s5_b_re, s5_b_im, s5_c_re, s5_c_im, s5_d, s5_w_glu,
           gla_w_alpha, gla_b_alpha, gla_norm, hgrn_lb_logits, hgrn_norm, w_branch, w_out, peer_w_q, peer_keys,
           peer_u, peer_v):
    nb, seq, d = x.shape
    nctx = ctx.shape[1]
    depth = w_ada.shape[0]
    mix = d // 4
    n_lat = nb * seq
    n_tok = n_lat + nb * nctx
    assert nb + 1 <= SUBLANES and seq % GRID_W == 0
    gla_dk, gla_dv = mix // 2 // GLA_HEADS, mix // GLA_HEADS
    hg_dk = hg_dv = mix // HGRN_HEADS
    gla_rank = gla_w_alpha.shape[2]
    tm = _tile(math.gcd(seq, nb * nctx), (512, 256, 128))
    tr = min(tm, 256)
    tw = min(tm, 256)
    tt = _tile(math.gcd(seq, nctx), (256, 128, 64, 32))
    ch = 64

    w_in_t = jnp.swapaxes(w_in, 1, 2)
    o_c = 3 * mix
    o_ca = o_c + 2 * GLA_HEADS * gla_dk + GLA_HEADS * gla_dv + mix
    o_dg = 2 * HGRN_HEADS * hg_dk + HGRN_HEADS * hg_dk + HGRN_HEADS * hg_dv
    o_gt = o_dg + mix
    za_w = 128

    p_lb = jax.nn.softmax(hgrn_lb_logits.astype(F32), axis=0)
    lower = jnp.cumsum(p_lb, axis=0) - p_lb[0]
    cvec = jnp.zeros((SUBLANES, d), F32).at[:nb].set(c).at[nb].set(c_ctx)
    b_ada3 = b_ada[:, None, :]
    u_tab = peer_u.astype(BF16)
    vt_tab = jnp.swapaxes(peer_v, 1, 2).astype(BF16)

    xs = jnp.concatenate([x.reshape(n_lat, d), ctx.reshape(nb * nctx, d)], axis=0)
    for l in range(depth):
        last = l == depth - 1
        m_out = n_lat if last else n_tok
        mod = _adaln(cvec, w_ada, b_ada3, l).reshape(SUBLANES, N_MOD, d)
        h = _normmod(xs, norm_gain, mod, l, seq, nb, tr)

        mm_in = lambda off, n, shift, tn, name: _mmt(h, w_in_t, l, off, n, shift, tn, n_tok, tw, name)
        pa = mm_in(0, 2 * mix, 0, mix, "proj_lru")
        pb = mm_in(2 * mix, mix, 0, mix, "proj_s5")
        pc = mm_in(o_c, o_ca - o_c - mix, 0, mix, "proj_gla")
        pcg = mm_in(o_ca - mix, mix, 0, mix, "proj_gla_gate")
        za = mm_in(o_ca, za_w, 0, za_w, "proj_gla_rank")
        pd = mm_in(o_ca, o_gt, gla_rank, mix, "proj_hgrn")
        pg = mm_in(o_ca + o_gt, N_BRANCH * d, gla_rank, mix, "proj_gate")

        wd, bd, sp = _lru_params(lru_w_a[l], lru_b_a[l], lru_w_i[l], lru_b_i[l], lru_lambda[l])
        hf, hb = _lru(pa, lru_conv_w[l], lru_conv_b[l][None, :], wd, bd, sp, nb, seq, nctx, tt)
        ya = _lru_out(pa, hf, hb, tm)

        s5p = _s5_params(s5_a_re[l], s5_a_im[l], s5_log_dt[l], s5_b_re[l], s5_b_im[l], s5_c_re[l], s5_c_im[l])
        ub = _to_col_major(pb, nb, seq)
        yf, ybk = _s5(ub, *s5p, nb, seq, nctx, tt)
        yb = _to_row_major(_s5_glu(ub, yf, ybk, s5_d[:, None, :], s5_w_glu, l, tm), nb, seq)

        wa = jnp.zeros((2, za_w, GLA_HEADS * gla_dk), F32).at[:, :gla_rank].set(gla_w_alpha[l]).astype(BF16)
        of, ob = _gla(pc, za, wa, gla_b_alpha[l][:, None, :], nb, seq, nctx, tt, ch, GLA_HEADS, gla_dk, gla_dv)
        yc = _headnorm(of, ob, pcg, 0, gla_norm[:, None, :], l, GLA_HEADS, tm)

        pdc = _to_col_major(pd, nb, seq)
        of, ob = _hgrn(pdc, lower[l][None, :], nb, seq, nctx, tt, ch, HGRN_HEADS, hg_dk, hg_dv)
        yd = _to_row_major(_headnorm(of, ob, pdc, o_dg // mix, hgrn_norm[:, None, :], l, HGRN_HEADS, tm), nb, seq)

        zm = _merge((ya, yb, yc, yd), pg, w_branch, l, m_out, tm)
        mo = _mm(zm, w_out, l, d, F32, m_out, tw, mix, "proj_out")
        xs, f, ft = _resid(xs, mo, norm_gain, mod, l, m_out, seq, nb, tr, 2, 1, True, False)

        q = _mm(f, peer_w_q, l, peer_w_q.shape[2], F32, m_out, tw, mix, "peer_query")
        e1, cnt, rank2, e2 = _route(q, peer_keys, l, _tile(m_out, (256, 128)))
        yt = _peer(ft, u_tab, vt_tab, l, e1, cnt, rank2, e2, _tile(m_out, (512, 256, 128)), 512)
        xs = _resid(xs, yt, norm_gain, mod, l, m_out, seq, nb, tr, 5, 3, False, True)[0]
    return xs[:n_lat].reshape(nb, seq, d)
```

```python
import functools
import math

import jax
import jax.numpy as jnp
from jax import lax
from jax.experimental import pallas as pl
from jax.experimental.pallas import tpu as pltpu

F32 = jnp.float32
BF16 = jnp.bfloat16
EPS = 1e-6
GRID_W = 64
N_MOD = 6
N_BRANCH = 4
LRU_BLOCKS = 16
LRU_C = 8.0
S5_GROUP = 16
S5_STATE = 64
S5_CHUNKS = 4
GLA_HEADS = 4
GLA_TAU = 16.0
HGRN_HEADS = 8
PEER_HEADS = 8
N_KEYS = 128
PEER_TOPK = 16
NO_RANK = 255.0
SUBLANES = 8
LANES = 128
VMEM_LIMIT = 56 * 1024 * 1024

NT_DIMS = (((1,), (1,)), ((), ()))
TN_DIMS = (((0,), (0,)), ((), ()))


def _params(n_axes, vmem=VMEM_LIMIT):
    return pltpu.CompilerParams(dimension_semantics=("arbitrary",) * n_axes, vmem_limit_bytes=vmem)


def _tile(n, prefs):
    for p in prefs:
        if n % p == 0:
            return p
    raise ValueError(f"no tile for {n} in {prefs}")


def _rms(x):
    return x * lax.rsqrt(jnp.mean(x * x, axis=-1, keepdims=True) + EPS)


def _ada_body(c_ref, w_ref, b_ref, o_ref):
    c = c_ref[...]
    s = (c * jax.nn.sigmoid(c)).astype(BF16)
    o_ref[...] = jnp.dot(s, w_ref[...].astype(BF16), preferred_element_type=F32) + b_ref[...]


def _adaln(cvec, w_ada, b_ada3, l):
    rows, d = cvec.shape
    n = w_ada.shape[2]
    tn = 512
    return pl.pallas_call(
        _ada_body,
        grid=(n // tn,),
        in_specs=[
            pl.BlockSpec((rows, d), lambda j: (0, 0)),
            pl.BlockSpec((None, d, tn), lambda j: (l, 0, j)),
            pl.BlockSpec((None, 1, tn), lambda j: (l, 0, j)),
        ],
        out_specs=pl.BlockSpec((rows, tn), lambda j: (0, j)),
        out_shape=jax.ShapeDtypeStruct((rows, n), F32),
        compiler_params=_params(1),
        name="adaln",
    )(cvec, w_ada, b_ada3)


def _normmod_body(x_ref, g_ref, mod_ref, o_ref, *, gi, shift_i, scale_i):
    y = _rms(x_ref[...]) * g_ref[gi:gi + 1, :]
    o_ref[...] = (y * (1.0 + mod_ref[scale_i:scale_i + 1, :]) + mod_ref[shift_i:shift_i + 1, :]).astype(o_ref.dtype)


def _mod_row(i, tm, seq, nb):
    return jnp.minimum((i * tm) // seq, nb)


def _normmod(x, gain, mod, l, seq, nb, tm):
    t, d = x.shape
    return pl.pallas_call(
        functools.partial(_normmod_body, gi=0, shift_i=0, scale_i=1),
        grid=(t // tm,),
        in_specs=[
            pl.BlockSpec((tm, d), lambda i: (i, 0)),
            pl.BlockSpec((None, 4, d), lambda i: (l, 0, 0)),
            pl.BlockSpec((None, N_MOD, d), lambda i: (_mod_row(i, tm, seq, nb), 0, 0)),
        ],
        out_specs=pl.BlockSpec((tm, d), lambda i: (i, 0)),
        out_shape=jax.ShapeDtypeStruct((t, d), BF16),
        compiler_params=_params(1),
        name="normmod",
    )(x, gain, mod)


def _mm_body(a_ref, w_ref, o_ref, wb_ref):
    @pl.when(pl.program_id(1) == 0)
    def _():
        wb_ref[...] = w_ref[...].astype(BF16)

    o_ref[...] = jnp.dot(a_ref[...], wb_ref[...], preferred_element_type=F32).astype(o_ref.dtype)


def _mm(a, w, l, ncols, out_dtype, m, tm, tn, name):
    k = a.shape[1]
    w_spec = pl.BlockSpec((None, k, tn), lambda j, i: (l, 0, j))
    return pl.pallas_call(
        _mm_body,
        grid=(ncols // tn, m // tm),
        in_specs=[pl.BlockSpec((tm, k), lambda j, i: (i, 0)), w_spec],
        out_specs=pl.BlockSpec((tm, tn), lambda j, i: (i, j)),
        out_shape=jax.ShapeDtypeStruct((m, ncols), out_dtype),
        scratch_shapes=[pltpu.VMEM((k, tn), BF16)],
        compiler_params=_params(2),
        name=name,
    )(a, w)


def _mmt_body(a_ref, w_ref, *rest, shift):
    if shift:
        tail_ref, o_ref, wb_ref = rest
    else:
        o_ref, wb_ref = rest
    tn = wb_ref.shape[0]

    @pl.when(pl.program_id(1) == 0)
    def _():
        if shift:
            wb_ref[0:tn - shift, :] = w_ref[shift:tn, :].astype(BF16)
            wb_ref[tn - shift:tn, :] = tail_ref[...].astype(BF16)
        else:
            wb_ref[...] = w_ref[...].astype(BF16)

    o_ref[...] = lax.dot_general(a_ref[...], wb_ref[...], NT_DIMS, preferred_element_type=F32).astype(o_ref.dtype)


def _mmt(a, wt, l, row_off, ncols, shift, tn, m, tm, name):
    k = a.shape[1]
    base = row_off // tn
    in_specs = [pl.BlockSpec((tm, k), lambda j, i: (i, 0)), pl.BlockSpec((None, tn, k), lambda j, i: (l, base + j, 0))]
    args = [a, wt]
    if shift:
        per = tn // shift
        in_specs.append(pl.BlockSpec((None, shift, k), lambda j, i: (l, (base + j + 1) * per, 0)))
        args.append(wt)
    return pl.pallas_call(
        functools.partial(_mmt_body, shift=shift),
        grid=(ncols // tn, m // tm),
        in_specs=in_specs,
        out_specs=pl.BlockSpec((tm, tn), lambda j, i: (i, j)),
        out_shape=jax.ShapeDtypeStruct((m, ncols), F32),
        scratch_shapes=[pltpu.VMEM((tn, k), BF16)],
        compiler_params=_params(2),
        name=name,
    )(*args)


def _seq_block(b, i, rev, nb, seq, ctx, tt):
    nctx, nlat = ctx // tt, seq // tt
    ic = (nctx - 1 - i) if rev else i
    il = (nlat - 1 - (i - nctx)) if rev else (i - nctx)
    return jnp.where(i < nctx, (nb * seq) // tt + b * nctx + ic, b * nlat + il)


def _seq_spec(cols, col_blk, rev, nb, seq, ctx, tt):
    return pl.BlockSpec((tt, cols), lambda b, i: (_seq_block(b, i, rev, nb, seq, ctx, tt), col_blk))


def _grid_view(arr):
    t, c = arr.shape
    return arr.reshape(t // GRID_W, GRID_W * c)


def _col_tokens(blk, c):
    return jnp.concatenate([blk[:, j * c:(j + 1) * c] for j in range(blk.shape[1] // c)], axis=0)


def _col_lat_spec(cols, rev, seq, ctx, tt):
    rows = seq // GRID_W
    nctx, nlat = ctx // tt, seq // tt

    def index(b, i):
        il = (nlat - 1 - (i - nctx)) if rev else (i - nctx)
        return b, jnp.clip(il, 0, nlat - 1)

    return pl.BlockSpec((rows, (tt // rows) * cols), index)


def _ctx_spec(cols, col_blk, rev, nb, seq, ctx, tt):
    nctx = ctx // tt

    def index(b, i):
        ic = (nctx - 1 - i) if rev else i
        return (nb * seq) // tt + b * nctx + jnp.clip(ic, 0, nctx - 1), col_blk

    return pl.BlockSpec((tt, cols), index)


def _halo_spec(cols, col_blk, rev, side, nb, seq, ctx, tt, total):
    per = tt // SUBLANES
    last = total // SUBLANES - 1

    def index(b, i):
        blk = _seq_block(b, i, rev, nb, seq, ctx, tt) * per
        blk = blk - 1 if side < 0 else blk + per
        return jnp.clip(blk, 0, last), col_blk

    return pl.BlockSpec((SUBLANES, cols), index)


def _stream_pos(i, rev, seq, ctx, tt):
    nctx, nlat = ctx // tt, seq // tt
    is_ctx = i < nctx
    ii = jnp.where(is_ctx, (nctx - 1 - i) if rev else i, (nlat - 1 - (i - nctx)) if rev else (i - nctx))
    n = jnp.where(is_ctx, nctx, nlat)
    return ii == 0, ii == n - 1


def _scan_tile_real(a, u, carry, rev):
    row = lax.broadcasted_iota(jnp.int32, a.shape, 0)
    for s in (1, 2, 4):
        if rev:
            a_sh, u_sh = pltpu.roll(a, SUBLANES - s, 0), pltpu.roll(u, SUBLANES - s, 0)
            ok = row < SUBLANES - s
        else:
            a_sh, u_sh = pltpu.roll(a, s, 0), pltpu.roll(u, s, 0)
            ok = row >= s
        u = jnp.where(ok, a * u_sh + u, u)
        a = jnp.where(ok, a * a_sh, a)
    h = u + a * carry
    return h, (h[0:1] if rev else h[SUBLANES - 1:SUBLANES])


def _lru_body(xf_ref, pf_ref, nf_ref, xb_ref, pb_ref, nb_ref, cw_ref, cb_ref, wd_ref, bd_ref, sp_ref,
              hf_ref, hb_ref, a_s, u_s, carry_s, *, seq, ctx, tt, c):
    i = pl.program_id(1)

    @pl.when(i == 0)
    def _():
        carry_s[...] = jnp.zeros_like(carry_s)

    row = lax.broadcasted_iota(jnp.int32, (tt, c), 0)
    for d, (x_ref, p_ref, n_ref) in enumerate(((xf_ref, pf_ref, nf_ref), (xb_ref, pb_ref, nb_ref))):
        first, last = _stream_pos(i, d == 1, seq, ctx, tt)
        x = x_ref[...]
        prev = jnp.where(first, 0.0, p_ref[SUBLANES - 1:SUBLANES, :])
        nx1 = jnp.where(last, 0.0, n_ref[0:1, :])
        nx2 = jnp.where(last, 0.0, n_ref[1:2, :])
        xm1 = jnp.where(row == 0, prev, pltpu.roll(x, 1, 0))
        xp1 = jnp.where(row == tt - 1, nx1, pltpu.roll(x, tt - 1, 0))
        xp2 = jnp.where(row == tt - 1, nx2, jnp.where(row == tt - 2, nx1, pltpu.roll(x, tt - 2, 0)))
        xl = xm1 * cw_ref[0:1, :] + x * cw_ref[1:2, :] + xp1 * cw_ref[2:3, :] + xp2 * cw_ref[3:4, :] + cb_ref[...]
        z = jnp.dot(xl.astype(BF16), wd_ref[d], preferred_element_type=F32) + bd_ref[d]
        r = jax.nn.sigmoid(z[:, :c])
        gi = jax.nn.sigmoid(z[:, c:])
        log_a = -LRU_C * r * sp_ref[d]
        a = jnp.exp(log_a)
        a_s[d] = a
        u_s[d] = jnp.sqrt(-jnp.tanh(log_a) * (a * a + 1.0)) * gi * xl

    ntile = tt // SUBLANES

    def step(k, carry):
        cf, cb = carry
        rows_f = pl.ds(pl.multiple_of(k * SUBLANES, SUBLANES), SUBLANES)
        rows_b = pl.ds(pl.multiple_of((ntile - 1 - k) * SUBLANES, SUBLANES), SUBLANES)
        h, cf = _scan_tile_real(a_s[0, rows_f, :], u_s[0, rows_f, :], cf, False)
        hf_ref[rows_f, :] = h
        h, cb = _scan_tile_real(a_s[1, rows_b, :], u_s[1, rows_b, :], cb, True)
        hb_ref[rows_b, :] = h
        return cf, cb

    cf, cb = lax.fori_loop(0, ntile, step, (carry_s[0, 0:1, :], carry_s[1, 0:1, :]))
    carry_s[0, 0:1, :] = cf
    carry_s[1, 0:1, :] = cb


def _lru(pa, cw, cb, wd, bd, sp, nb, seq, ctx, tt):
    t = pa.shape[0]
    c = cw.shape[1]
    geo = (nb, seq, ctx, tt)
    full = lambda shape: pl.BlockSpec(shape, lambda b, i: (0,) * len(shape))
    return pl.pallas_call(
        functools.partial(_lru_body, seq=seq, ctx=ctx, tt=tt, c=c),
        grid=(nb, (seq + ctx) // tt),
        in_specs=[
            _seq_spec(c, 0, False, *geo), _halo_spec(c, 0, False, -1, *geo, t), _halo_spec(c, 0, False, 1, *geo, t),
            _seq_spec(c, 0, True, *geo), _halo_spec(c, 0, True, -1, *geo, t), _halo_spec(c, 0, True, 1, *geo, t),
            full(cw.shape), full(cb.shape), full(wd.shape), full(bd.shape), full(sp.shape),
        ],
        out_specs=[_seq_spec(c, 0, False, *geo), _seq_spec(c, 0, True, *geo)],
        out_shape=[jax.ShapeDtypeStruct((t, c), F32)] * 2,
        scratch_shapes=[pltpu.VMEM((2, tt, c), F32), pltpu.VMEM((2, tt, c), F32), pltpu.VMEM((2, SUBLANES, c), F32)],
        compiler_params=_params(2),
        name="rglru_scan",
    )(pa, pa, pa, pa, pa, pa, cw, cb, wd, bd, sp)


def _lru_out_body(ay_ref, hf_ref, hb_ref, o_ref):
    o_ref[...] = (jax.nn.gelu(ay_ref[...]) * (hf_ref[...] + hb_ref[...])).astype(o_ref.dtype)


def _lru_out(pa, hf, hb, tm):
    t, c = hf.shape
    return pl.pallas_call(
        _lru_out_body,
        grid=(t // tm,),
        in_specs=[pl.BlockSpec((tm, c), lambda i: (i, 1)), pl.BlockSpec((tm, c), lambda i: (i, 0)),
                  pl.BlockSpec((tm, c), lambda i: (i, 0))],
        out_specs=pl.BlockSpec((tm, c), lambda i: (i, 0)),
        out_shape=jax.ShapeDtypeStruct((t, c), BF16),
        compiler_params=_params(1),
        name="rglru_out",
    )(pa, hf, hb)


def _scan_tile_cplx(xr, xi, dbl_ref, pw_ref, d, lanes, cr, ci, rev):
    for n, s in enumerate((1, 2, 4)):
        ar = dbl_ref[d, 0, n, :, lanes]
        ai = dbl_ref[d, 1, n, :, lanes]
        shift = SUBLANES - s if rev else s
        sr, si = pltpu.roll(xr, shift, 0), pltpu.roll(xi, shift, 0)
        xr, xi = xr + ar * sr - ai * si, xi + ar * si + ai * sr
    pr = pw_ref[d, 0, :, lanes]
    pi = pw_ref[d, 1, :, lanes]
    hr = xr + pr * cr - pi * ci
    hi = xi + pr * ci + pi * cr
    sel = slice(0, 1) if rev else slice(SUBLANES - 1, SUBLANES)
    return hr, hi, hr[sel], hi[sel]


def _load_seq_block(dst_ref, d, lat_ref, ctx_ref, is_ctx, c):
    @pl.when(is_ctx)
    def _():
        dst_ref[d] = ctx_ref[...]

    @pl.when(jnp.logical_not(is_ctx))
    def _():
        dst_ref[d] = _col_tokens(lat_ref[...], c)


def _s5_body(ulf_ref, ucf_ref, ulb_ref, ucb_ref, bre_ref, bim_ref, cre_ref, cim_ref, dbl_ref, pw_ref, yf_ref, yb_ref,
             u_s, hr_s, hi_s, carry_s, *, tt, gw, sw, nctx):
    i = pl.program_id(1)

    @pl.when(i == 0)
    def _():
        carry_s[...] = jnp.zeros_like(carry_s)

    for d, (lat_ref, ctx_ref) in enumerate(((ulf_ref, ucf_ref), (ulb_ref, ucb_ref))):
        _load_seq_block(u_s, d, lat_ref, ctx_ref, i < nctx, gw * S5_CHUNKS)

    ntile = tt // SUBLANES
    for ch in range(S5_CHUNKS):
        cols = slice(ch * gw, (ch + 1) * gw)
        lanes = slice(ch * sw, (ch + 1) * sw)
        for d in range(2):
            u = u_s[d, :, cols].astype(BF16)
            hr_s[d] = jnp.dot(u, bre_ref[d, ch], preferred_element_type=F32)
            hi_s[d] = jnp.dot(u, bim_ref[d, ch], preferred_element_type=F32)

        def step(k, carry):
            crf, cif, crb, cib = carry
            rows_f = pl.ds(pl.multiple_of(k * SUBLANES, SUBLANES), SUBLANES)
            rows_b = pl.ds(pl.multiple_of((ntile - 1 - k) * SUBLANES, SUBLANES), SUBLANES)
            hr, hi, crf, cif = _scan_tile_cplx(hr_s[0, rows_f, :], hi_s[0, rows_f, :], dbl_ref, pw_ref, 0, lanes,
                                               crf, cif, False)
            hr_s[0, rows_f, :] = hr
            hi_s[0, rows_f, :] = hi
            hr, hi, crb, cib = _scan_tile_cplx(hr_s[1, rows_b, :], hi_s[1, rows_b, :], dbl_ref, pw_ref, 1, lanes,
                                               crb, cib, True)
            hr_s[1, rows_b, :] = hr
            hi_s[1, rows_b, :] = hi
            return crf, cif, crb, cib

        init = tuple(carry_s[n, 0:1, lanes] for n in range(4))
        fin = lax.fori_loop(0, ntile, step, init)
        for n in range(4):
            carry_s[n, 0:1, lanes] = fin[n]
        for d, y_ref in enumerate((yf_ref, yb_ref)):
            y_ref[:, cols] = (jnp.dot(hr_s[d].astype(BF16), cre_ref[d, ch], preferred_element_type=F32)
                              - jnp.dot(hi_s[d].astype(BF16), cim_ref[d, ch], preferred_element_type=F32))


def _s5(u, bre, bim, cre, cim, dbl, pw, nb, seq, ctx, tt):
    t, c = u.shape
    gw = c // S5_CHUNKS
    sw = bre.shape[-1]
    geo = (nb, seq, ctx, tt)
    full = lambda shape: pl.BlockSpec(shape, lambda b, i: (0,) * len(shape))
    lat = lambda rev: _col_lat_spec(c, rev, seq, ctx, tt)
    cx = lambda rev: _ctx_spec(c, 0, rev, *geo)
    uv = _grid_view(u)
    return pl.pallas_call(
        functools.partial(_s5_body, tt=tt, gw=gw, sw=sw, nctx=ctx // tt),
        grid=(nb, (seq + ctx) // tt),
        in_specs=[lat(False), cx(False), lat(True), cx(True),
                  full(bre.shape), full(bim.shape), full(cre.shape), full(cim.shape), full(dbl.shape), full(pw.shape)],
        out_specs=[_seq_spec(c, 0, False, *geo), _seq_spec(c, 0, True, *geo)],
        out_shape=[jax.ShapeDtypeStruct((t, c), F32)] * 2,
        scratch_shapes=[pltpu.VMEM((2, tt, c), F32), pltpu.VMEM((2, tt, sw), F32), pltpu.VMEM((2, tt, sw), F32),
                        pltpu.VMEM((4, SUBLANES, sw * S5_CHUNKS), F32)],
        compiler_params=_params(2),
        name="s5_scan",
    )(uv, u, uv, u, bre, bim, cre, cim, dbl, pw)


def _tile_specs(cols, col_blk, n_lat, seq, tm):
    rows = seq // GRID_W
    nlat, per = n_lat // tm, seq // tm

    def lat_index(i):
        il = jnp.minimum(i, nlat - 1)
        return il // per, il % per

    return (pl.BlockSpec((rows, (tm // rows) * cols), lat_index),
            pl.BlockSpec((tm, cols), lambda i: (jnp.maximum(i, nlat), col_blk)))


def _load_tile(dst_ref, lat_ref, ctx_ref, is_ctx, c):
    @pl.when(is_ctx)
    def _():
        dst_ref[...] = ctx_ref[...]

    @pl.when(jnp.logical_not(is_ctx))
    def _():
        dst_ref[...] = _col_tokens(lat_ref[...], c)


def _s5_glu_body(ul_ref, uc_ref, yf_ref, yb_ref, d_ref, w_ref, o_ref, wb_ref, u_s, *, c, nlat):
    @pl.when(pl.program_id(0) == 0)
    def _():
        wb_ref[...] = w_ref[...].astype(BF16)

    _load_tile(u_s, ul_ref, uc_ref, pl.program_id(0) >= nlat, c)
    y = jax.nn.gelu(u_s[...] * d_ref[...] + yf_ref[...] + yb_ref[...])
    z = jnp.dot(y.astype(BF16), wb_ref[...], preferred_element_type=F32)
    o_ref[...] = (z[:, :c] * jax.nn.sigmoid(z[:, c:])).astype(o_ref.dtype)


def _s5_glu(u, yf, yb, dskip, w_glu, l, n_lat, seq, tm):
    t, c = u.shape
    lat_spec, ctx_spec = _tile_specs(c, 0, n_lat, seq, tm)
    return pl.pallas_call(
        functools.partial(_s5_glu_body, c=c, nlat=n_lat // tm),
        grid=(t // tm,),
        in_specs=[lat_spec, ctx_spec] + [pl.BlockSpec((tm, c), lambda i: (i, 0))] * 2
        + [pl.BlockSpec((None, 1, c), lambda i: (l, 0, 0)), pl.BlockSpec((None, c, 2 * c), lambda i: (l, 0, 0))],
        out_specs=pl.BlockSpec((tm, c), lambda i: (i, 0)),
        out_shape=jax.ShapeDtypeStruct((t, c), BF16),
        scratch_shapes=[pltpu.VMEM((c, 2 * c), BF16), pltpu.VMEM((tm, c), F32)],
        compiler_params=_params(1),
        name="s5_glu",
    )(_grid_view(u), u, yf, yb, dskip, w_glu)


def _chunk_cumsum(x, rev, ch):
    row = lax.broadcasted_iota(jnp.int32, x.shape, 0)
    s = 1
    while s < ch:
        if rev:
            x = x + jnp.where(row < ch - s, pltpu.roll(x, ch - s, 0), 0.0)
        else:
            x = x + jnp.where(row >= s, pltpu.roll(x, s, 0), 0.0)
        s *= 2
    return x


def _chunk_scan(q_s, k_s, v_s, lf_s, st_s, o_refs, *, tt, ch, heads, dk, dv):
    nchunk = tt // ch
    tri_r = lax.broadcasted_iota(jnp.int32, (ch, ch), 0)
    tri_c = lax.broadcasted_iota(jnp.int32, (ch, ch), 1)

    def step(n, _):
        for d in range(2):
            rev = d == 1
            cc = (nchunk - 1 - n) if rev else n
            rows = pl.ds(pl.multiple_of(cc * ch, ch), ch)
            b = _chunk_cumsum(lf_s[d, rows, :], rev, ch)
            piv = ch // 2 if rev else ch // 2 - 1
            end = 0 if rev else ch - 1
            m = b[piv:piv + 1, :]
            bl = b[end:end + 1, :]
            qm = q_s[d, rows, :] * jnp.exp(b - m)
            km = k_s[d, rows, :] * jnp.exp(m - b)
            qg = (qm * jnp.exp(m)).astype(BF16)
            kg = (km * jnp.exp(bl - m)).astype(BF16)
            dec = jnp.exp(bl)
            qm = qm.astype(BF16)
            km = km.astype(BF16)
            v = v_s[d, rows, :].astype(BF16)
            keep = (tri_r <= tri_c) if rev else (tri_r >= tri_c)
            for h in range(heads):
                ks = slice(h * dk, (h + 1) * dk)
                vs = slice(h * dv, (h + 1) * dv)
                sc = lax.dot_general(qm[:, ks], km[:, ks], NT_DIMS, preferred_element_type=F32)
                sc = jnp.where(keep, sc, 0.0).astype(BF16)
                st = st_s[d, h]
                o = jnp.dot(sc, v[:, vs], preferred_element_type=F32)
                o = o + lax.dot_general(qg[:, ks], st.astype(BF16), NT_DIMS, preferred_element_type=F32)
                st_s[d, h] = st * dec[:, ks] + lax.dot_general(v[:, vs], kg[:, ks], TN_DIMS,
                                                               preferred_element_type=F32)
                o_refs[d][rows, vs] = o
        return 0

    lax.fori_loop(0, nchunk, step, 0)


def _gla_body(qf_ref, zf_ref, qb_ref, zb_ref, wa_ref, ba_ref, of_ref, ob_ref, q_s, k_s, v_s, lf_s, st_s,
              *, tt, ch, heads, dk, dv):
    @pl.when(pl.program_id(1) == 0)
    def _():
        st_s[...] = jnp.zeros_like(st_s)

    hk = heads * dk
    for d, (x_ref, z_ref) in enumerate(((qf_ref, zf_ref), (qb_ref, zb_ref))):
        q_s[d] = x_ref[:, :hk] * dk ** -0.5
        k_s[d] = x_ref[:, hk:2 * hk]
        v_s[d] = x_ref[:, 2 * hk:]
        la = jnp.dot(z_ref[...].astype(BF16), wa_ref[d], preferred_element_type=F32) + ba_ref[d]
        lf_s[d] = jax.nn.log_sigmoid(la) / GLA_TAU
    _chunk_scan(q_s, k_s, v_s, lf_s, st_s, (of_ref, ob_ref), tt=tt, ch=ch, heads=heads, dk=dk, dv=dv)


def _hgrn_body(xlf_ref, xcf_ref, xlb_ref, xcb_ref, lb_ref, of_ref, ob_ref, q_s, k_s, v_s, lf_s, st_s,
               *, tt, ch, heads, dk, dv, nctx):
    i = pl.program_id(1)

    @pl.when(i == 0)
    def _():
        st_s[...] = jnp.zeros_like(st_s)

    hk = heads * dk
    wide = xcf_ref.shape[1]
    lb = lb_ref[...]

    def prep(d, cols):
        z = cols(d * hk, (d + 1) * hk)
        q_s[d] = jax.nn.silu(cols(2 * hk, 3 * hk)) * dk ** -0.5
        v_s[d] = cols(3 * hk, wide)
        k_s[d] = (1.0 - lb) * jax.nn.sigmoid(-z)
        lf_s[d] = jnp.log(lb + (1.0 - lb) * jax.nn.sigmoid(z))

    for d, (lat_ref, ctx_ref) in enumerate(((xlf_ref, xcf_ref), (xlb_ref, xcb_ref))):
        @pl.when(i < nctx)
        def _():
            prep(d, lambda a, b: ctx_ref[:, a:b])

        @pl.when(i >= nctx)
        def _():
            prep(d, lambda a, b: jnp.concatenate(
                [lat_ref[:, j * wide + a:j * wide + b] for j in range(lat_ref.shape[1] // wide)], axis=0))

    _chunk_scan(q_s, k_s, v_s, lf_s, st_s, (of_ref, ob_ref), tt=tt, ch=ch, heads=heads, dk=dk, dv=dv)


def _chunk_scratch(tt, heads, dk, dv):
    return [pltpu.VMEM((2, tt, heads * dk), F32), pltpu.VMEM((2, tt, heads * dk), F32),
            pltpu.VMEM((2, tt, heads * dv), F32), pltpu.VMEM((2, tt, heads * dk), F32),
            pltpu.VMEM((2, heads, dv, dk), F32)]


def _gla(pc, za, wa, ba, nb, seq, ctx, tt, ch, heads, dk, dv):
    t = pc.shape[0]
    geo = (nb, seq, ctx, tt)
    wide = 2 * heads * dk + heads * dv
    full = lambda shape: pl.BlockSpec(shape, lambda b, i: (0,) * len(shape))
    return pl.pallas_call(
        functools.partial(_gla_body, tt=tt, ch=ch, heads=heads, dk=dk, dv=dv),
        grid=(nb, (seq + ctx) // tt),
        in_specs=[_seq_spec(wide, 0, False, *geo), _seq_spec(za.shape[1], 0, False, *geo),
                  _seq_spec(wide, 0, True, *geo), _seq_spec(za.shape[1], 0, True, *geo),
                  full(wa.shape), full(ba.shape)],
        out_specs=[_seq_spec(heads * dv, 0, False, *geo), _seq_spec(heads * dv, 0, True, *geo)],
        out_shape=[jax.ShapeDtypeStruct((t, heads * dv), F32)] * 2,
        scratch_shapes=_chunk_scratch(tt, heads, dk, dv),
        compiler_params=_params(2),
        name="gla_scan",
    )(pc, za, pc, za, wa, ba)


def _hgrn(pd, lb, nb, seq, ctx, tt, ch, heads, dk, dv):
    t, wide = pd.shape
    geo = (nb, seq, ctx, tt)
    lat = lambda rev: _col_lat_spec(wide, rev, seq, ctx, tt)
    cx = lambda rev: _ctx_spec(wide, 0, rev, *geo)
    pv = _grid_view(pd)
    return pl.pallas_call(
        functools.partial(_hgrn_body, tt=tt, ch=ch, heads=heads, dk=dk, dv=dv, nctx=ctx // tt),
        grid=(nb, (seq + ctx) // tt),
        in_specs=[lat(False), cx(False), lat(True), cx(True), pl.BlockSpec(lb.shape, lambda b, i: (0, 0))],
        out_specs=[_seq_spec(heads * dv, 0, False, *geo), _seq_spec(heads * dv, 0, True, *geo)],
        out_shape=[jax.ShapeDtypeStruct((t, heads * dv), F32)] * 2,
        scratch_shapes=_chunk_scratch(tt, heads, dk, dv),
        compiler_params=_params(2),
        name="hgrn_scan",
    )(pv, pd, pv, pd, lb)


def _headnorm_body(of_ref, ob_ref, *rest, heads, dv, nlat):
    if nlat is None:
        g_ref, n_ref, o_ref = rest
        gate = g_ref[...]
    else:
        gl_ref, gc_ref, n_ref, o_ref, g_s = rest
        _load_tile(g_s, gl_ref, gc_ref, pl.program_id(0) >= nlat, heads * dv)
        gate = g_s[...]
    o = of_ref[...] + ob_ref[...]
    parts = [_rms(o[:, h * dv:(h + 1) * dv]) for h in range(heads)]
    y = jnp.concatenate(parts, axis=-1) * n_ref[...]
    o_ref[...] = (y * jax.nn.silu(gate)).astype(o_ref.dtype)


def _headnorm(of, ob, gate_arr, gate_blk, gain, l, heads, tm, col_major=None):
    t, c = of.shape
    row = pl.BlockSpec((tm, c), lambda i: (i, 0))
    if col_major is None:
        gate_specs, gate_args, scratch, nlat = [pl.BlockSpec((tm, c), lambda i: (i, gate_blk))], [gate_arr], [], None
    else:
        n_lat, seq = col_major
        gate_specs = list(_tile_specs(c, gate_blk, n_lat, seq, tm))
        gate_args, scratch, nlat = [_grid_view(gate_arr), gate_arr], [pltpu.VMEM((tm, c), F32)], n_lat // tm
    return pl.pallas_call(
        functools.partial(_headnorm_body, heads=heads, dv=c // heads, nlat=nlat),
        grid=(t // tm,),
        in_specs=[row, row] + gate_specs + [pl.BlockSpec((None, 1, c), lambda i: (l, 0, 0))],
        out_specs=row,
        out_shape=jax.ShapeDtypeStruct((t, c), BF16),
        scratch_shapes=scratch,
        compiler_params=_params(1),
        name="headnorm_gate",
    )(of, ob, *gate_args, gain)


def _merge_body(ya_ref, yb_ref, yc_ref, yd_ref, g0_ref, g1_ref, g2_ref, g3_ref, w_ref, o_ref, wb_ref):
    @pl.when(pl.program_id(1) == 0)
    def _():
        wb_ref[...] = w_ref[...].astype(BF16)

    acc = None
    for k, (y_ref, g_ref) in enumerate(((ya_ref, g0_ref), (yb_ref, g1_ref), (yc_ref, g2_ref), (yd_ref, g3_ref))):
        term = jax.nn.sigmoid(g_ref[...]) * jnp.dot(y_ref[...], wb_ref[k], preferred_element_type=F32)
        acc = term if acc is None else acc + term
    o_ref[...] = acc.astype(o_ref.dtype)


def _merge(ys, pg, w_branch, l, m, tm):
    c = ys[0].shape[1]
    d = w_branch.shape[3]
    tn = 512
    nj = d // tn
    y_spec = pl.BlockSpec((tm, c), lambda j, i: (i, 0))
    g_specs = [pl.BlockSpec((tm, tn), functools.partial(lambda j, i, k: (i, k * nj + j), k=k)) for k in range(N_BRANCH)]
    return pl.pallas_call(
        _merge_body,
        grid=(nj, m // tm),
        in_specs=[y_spec] * N_BRANCH + g_specs + [pl.BlockSpec((None, N_BRANCH, c, tn), lambda j, i: (l, 0, 0, j))],
        out_specs=pl.BlockSpec((tm, tn), lambda j, i: (i, j)),
        out_shape=jax.ShapeDtypeStruct((m, d), BF16),
        scratch_shapes=[pltpu.VMEM((N_BRANCH, c, tn), BF16)],
        compiler_params=_params(2),
        name="branch_merge",
    )(*ys, pg, pg, pg, pg, w_branch)


def _resid_body(x_ref, y_ref, g_ref, mod_ref, xo_ref, *f_refs, gate_i, gy, gf, shift_i, scale_i, y_transposed):
    y = y_ref[...].T if y_transposed else y_ref[...]
    xn = x_ref[...] + mod_ref[gate_i:gate_i + 1, :] * (_rms(y) * g_ref[gy:gy + 1, :])
    xo_ref[...] = xn
    if f_refs:
        f = _rms(xn) * g_ref[gf:gf + 1, :]
        f = f * (1.0 + mod_ref[scale_i:scale_i + 1, :]) + mod_ref[shift_i:shift_i + 1, :]
        f_refs[0][...] = f.astype(BF16)
        f_refs[1][...] = f.T.astype(BF16)


def _resid(x, y, gain, mod, l, m, seq, nb, tm, gate_i, gy, with_f, y_transposed):
    d = x.shape[1]
    row = pl.BlockSpec((tm, d), lambda i: (i, 0))
    col = pl.BlockSpec((d, tm), lambda i: (0, i))
    out_shape = [jax.ShapeDtypeStruct((m, d), F32)]
    out_specs = [row]
    if with_f:
        out_shape += [jax.ShapeDtypeStruct((m, d), BF16), jax.ShapeDtypeStruct((d, m), BF16)]
        out_specs += [row, col]
    return pl.pallas_call(
        functools.partial(_resid_body, gate_i=gate_i, gy=gy, gf=2, shift_i=3, scale_i=4, y_transposed=y_transposed),
        grid=(m // tm,),
        in_specs=[row, col if y_transposed else row, pl.BlockSpec((None, 4, d), lambda i: (l, 0, 0)),
                  pl.BlockSpec((None, N_MOD, d), lambda i: (_mod_row(i, tm, seq, nb), 0, 0))],
        out_specs=out_specs,
        out_shape=out_shape,
        compiler_params=_params(1),
        name="residual",
    )(x, y, gain, mod)


def _top_rows(x, vals_ref, idx_ref, want_rank):
    n_rows = x.shape[0]
    iota = lax.broadcasted_iota(jnp.int32, x.shape, 0)

    def step(r, carry):
        x = carry[0]
        mx = jnp.max(x, axis=0, keepdims=True)
        vals_ref[pl.ds(r, 1), :] = mx
        first = jnp.min(jnp.where(x == mx, iota, n_rows), axis=0, keepdims=True)
        idx_ref[pl.ds(r, 1), :] = first
        hit = iota == first
        x = jnp.where(hit, -jnp.inf, x)
        return (x, jnp.where(hit, jnp.asarray(r, F32), carry[1])) if want_rank else (x,)

    init = (x, jnp.full(x.shape, NO_RANK, F32)) if want_rank else (x,)
    return lax.fori_loop(0, PEER_TOPK, step, init)[-1]


def _route_body(q_ref, keys_ref, e1_ref, cnt_ref, rank2_ref, e2_ref, va_s, vb_s, vc_s, ia_s, ib_s, *, dq):
    half = PEER_TOPK // 2
    iota = lax.broadcasted_iota(jnp.int32, (N_KEYS, q_ref.shape[0]), 0)
    for h in range(PEER_HEADS):
        scores = []
        for p in range(2):
            qh = q_ref[:, h * 2 * dq + p * dq:h * 2 * dq + (p + 1) * dq].astype(BF16)
            scores.append(lax.dot_general(keys_ref[h, p].astype(BF16), qh, NT_DIMS, preferred_element_type=F32))
        _top_rows(scores[0], va_s, ia_s, False)
        rank2 = _top_rows(scores[1], vb_s, ib_s, True)
        a, b = va_s[...], vb_s[...]
        cand = jnp.concatenate([a[0:1, :] + b] + [a[r:r + 1, :] + b[0:half, :] for r in range(1, half)]
                               + [a[half:, :] + b[0:1, :]], axis=0)
        _top_rows(cand, vc_s, ib_s, False)
        best = vc_s[...]
        z = jnp.sum(jnp.exp(best - best[0:1, :]), axis=0, keepdims=True)
        row = ib_s[...]
        sel_i = jnp.where(row < PEER_TOPK, 0,
                          jnp.where(row < PEER_TOPK + half * (half - 1), (row - half) // half, row - half * half))
        key_a = ia_s[...]
        cnt = jnp.zeros(scores[0].shape, F32)
        for i in range(PEER_TOPK):
            n_i = jnp.sum((sel_i == i).astype(F32), axis=0, keepdims=True)
            cnt = jnp.where(iota == key_a[i:i + 1, :], n_i, cnt)
        e1_ref[h] = jnp.exp(scores[0] - a[0:1, :]) / z
        cnt_ref[h] = cnt
        rank2_ref[h] = rank2.astype(BF16)
        e2_ref[h] = jnp.exp(scores[1] - b[0:1, :]).astype(BF16)


def _route(q, keys, l, tm):
    t = q.shape[0]
    dq = keys.shape[-1]
    spec = pl.BlockSpec((PEER_HEADS, N_KEYS, tm), lambda i: (0, 0, i))
    return pl.pallas_call(
        functools.partial(_route_body, dq=dq),
        grid=(t // tm,),
        in_specs=[pl.BlockSpec((tm, q.shape[1]), lambda i: (i, 0)),
                  pl.BlockSpec((None,) + keys.shape[1:], lambda i: (l, 0, 0, 0, 0))],
        out_specs=[spec] * 4,
        out_shape=[jax.ShapeDtypeStruct((PEER_HEADS, N_KEYS, t), dt) for dt in (F32, F32, BF16, BF16)],
        scratch_shapes=[pltpu.VMEM((PEER_TOPK, tm), F32)] * 3 + [pltpu.VMEM((PEER_TOPK, tm), jnp.int32)] * 2,
        compiler_params=_params(1),
        name="peer_route",
    )(q, keys)


def _peer_body(ft_ref, u_ref, vt_ref, e1_ref, cnt_ref, rank2_ref, e2_ref, o_ref, g_s, act_s, *, te, nj):
    j = pl.program_id(1)

    @pl.when(j == 0)
    def _():
        o_ref[...] = jnp.zeros_like(o_ref)
        act_s[...] = jnp.zeros_like(act_s)

    tile = jnp.maximum(j - 1, 0)
    tm = act_s.shape[1]
    zero = jnp.zeros((), BF16)
    for r in range(te // N_KEYS):
        i1 = tile * (te // N_KEYS) + r
        rows = slice(r * N_KEYS, (r + 1) * N_KEYS)
        e1_rows = [e1_ref[h, pl.ds(i1, 1), :].astype(BF16) for h in range(PEER_HEADS)]
        cnt_rows = [cnt_ref[h, pl.ds(i1, 1), :].astype(BF16) for h in range(PEER_HEADS)]
        for cb in range(tm // LANES):
            cs = slice(cb * LANES, (cb + 1) * LANES)
            w = None
            for h in range(PEER_HEADS):
                term = jnp.where(rank2_ref[h, :, cs] < cnt_rows[h][:, cs], e2_ref[h, :, cs] * e1_rows[h][:, cs], zero)
                w = term if w is None else w + term
            g_s[rows, cs] = w * act_s[rows, cs].astype(BF16)

    o_ref[...] += jnp.dot(vt_ref[...], g_s[...], preferred_element_type=F32)
    act_s[...] = jax.nn.gelu(jnp.dot(u_ref[...], ft_ref[...], preferred_element_type=F32))


def _peer(ft, u, vt, l, e1, cnt, rank2, e2, tm, te):
    d, t = ft.shape
    nj = u.shape[1] // te
    once = pl.Buffered(1)
    big_spec = pl.BlockSpec((PEER_HEADS, N_KEYS, tm), lambda i, j: (0, 0, i), pipeline_mode=once)
    return pl.pallas_call(
        functools.partial(_peer_body, te=te, nj=nj),
        grid=(t // tm, nj + 1),
        in_specs=[pl.BlockSpec((d, tm), lambda i, j: (0, i), pipeline_mode=once),
                  pl.BlockSpec((None, te, d), lambda i, j: (l, jnp.minimum(j, nj - 1), 0)),
                  pl.BlockSpec((None, d, te), lambda i, j: (l, 0, jnp.maximum(j - 1, 0))),
                  big_spec, big_spec, big_spec, big_spec],
        out_specs=pl.BlockSpec((d, tm), lambda i, j: (0, i)),
        out_shape=jax.ShapeDtypeStruct((d, t), F32),
        scratch_shapes=[pltpu.VMEM((te, tm), BF16), pltpu.VMEM((te, tm), F32)],
        compiler_params=_params(2),
        name="peer_experts",
    )(ft, u, vt, e1, cnt, rank2, e2)


def _block_diag(w):
    n, r, c = w.shape
    eye = jnp.eye(n, dtype=w.dtype)
    return (w[:, :, None, :] * eye[:, None, :, None]).reshape(n * r, n * c)


def _lru_params(w_a, b_a, w_i, b_i, lam):
    wd = jnp.stack([jnp.concatenate([_block_diag(w_a[d]), _block_diag(w_i[d])], axis=1) for d in range(2)])
    bd = jnp.stack([jnp.concatenate([b_a[d], b_i[d]])[None, :] for d in range(2)])
    sp = jax.nn.softplus(-lam)[:, None, :]
    return wd.astype(BF16), bd, sp


def _s5_params(a_re, a_im, log_dt, b_re, b_im, c_re, c_im):
    g, p = a_re.shape[1:]
    gc = g // S5_CHUNKS
    outs = [[] for _ in range(6)]
    for d in range(2):
        big_a = lax.complex(a_re[d], a_im[d])
        dt = jnp.exp(log_dt[d])[:, None]
        a_bar = jnp.exp(big_a * dt)
        b_bar = ((a_bar - 1.0) / big_a)[..., None] * lax.complex(b_re[d], b_im[d])
        c_mat = lax.complex(c_re[d], c_im[d])
        bm = jnp.swapaxes(b_bar, 1, 2).reshape(S5_CHUNKS, gc, S5_GROUP, p)
        cm = jnp.swapaxes(c_mat, 1, 2).reshape(S5_CHUNKS, gc, p, S5_GROUP)
        bm = jnp.stack([_block_diag(bm[k]) for k in range(S5_CHUNKS)])
        cm = jnp.stack([_block_diag(cm[k]) for k in range(S5_CHUNKS)])
        pows = jnp.stack([jnp.exp(big_a * dt * float(n)) for n in range(1, SUBLANES + 1)]).reshape(SUBLANES, g * p)
        row = jnp.arange(SUBLANES)[:, None]
        dbl = jnp.stack([jnp.where((row < SUBLANES - s) if d == 1 else (row >= s), pows[s - 1][None, :], 0.0)
                         for s in (1, 2, 4)])
        pw = pows[::-1] if d == 1 else pows
        for lst, val in zip(outs, (jnp.real(bm), jnp.imag(bm), jnp.real(cm), jnp.imag(cm),
                                   jnp.stack([jnp.real(dbl), jnp.imag(dbl)]), jnp.stack([jnp.real(pw), jnp.imag(pw)]))):
            lst.append(val)
    bre, bim, cre, cim, dbl, pw = (jnp.stack(o) for o in outs)
    return bre.astype(BF16), bim.astype(BF16), cre.astype(BF16), cim.astype(BF16), dbl, pw


def _to_row_major(t, nb, seq):
    rows = seq // GRID_W
    lat = t[:nb * seq].reshape(nb, GRID_W, rows, -1).transpose(0, 2, 1, 3).reshape(nb * seq, -1)
    return jnp.concatenate([lat, t[nb * seq:]], axis=0)


def kernel(x, c, ctx, c_ctx, w_ada, b_ada, norm_gain, w_in, lru_conv_w, lru_conv_b, lru_w_a, lru_b_a, lru_w_i,
           lru_b_i, lru_lambda, s5_a_re, s5_a_im, s5_log_dt, s5_b_re, s5_b_im, s5_c_re, s5_c_im, s5_d, s5_w_glu,
           gla_w_alpha, gla_b_alpha, gla_norm, hgrn_lb_logits, hgrn_norm, w_branch, w_out, peer_w_q, peer_keys,
           peer_u, peer_v):
    nb, seq, d = x.shape
    nctx = ctx.shape[1]
    depth = w_ada.shape[0]
    mix = d // 4
    n_lat = nb * seq
    n_tok = n_lat + nb * nctx
    assert nb + 1 <= SUBLANES and seq % GRID_W == 0
    gla_dk, gla_dv = mix // 2 // GLA_HEADS, mix // GLA_HEADS
    hg_dk = hg_dv = mix // HGRN_HEADS
    gla_rank = gla_w_alpha.shape[2]
    tm = _tile(math.gcd(seq, nb * nctx), (512, 256, 128))
    tr = min(tm, 256)
    tw = min(tm, 256)
    tt = _tile(math.gcd(seq, nctx), (256, 128, 64, 32))
    ch = 64

    w_in_t = jnp.swapaxes(w_in, 1, 2)
    o_c = 3 * mix
    o_ca = o_c + 2 * GLA_HEADS * gla_dk + GLA_HEADS * gla_dv + mix
    o_dg = 2 * HGRN_HEADS * hg_dk + HGRN_HEADS * hg_dk + HGRN_HEADS * hg_dv
    o_gt = o_dg + mix
    za_w = 128

    p_lb = jax.nn.softmax(hgrn_lb_logits.astype(F32), axis=0)
    lower = jnp.cumsum(p_lb, axis=0) - p_lb[0]
    cvec = jnp.zeros((SUBLANES, d), F32).at[:nb].set(c).at[nb].set(c_ctx)
    b_ada3 = b_ada[:, None, :]
    u_tab = peer_u.astype(BF16)
    vt_tab = jnp.swapaxes(peer_v, 1, 2).astype(BF16)

    xs = jnp.concatenate([x.reshape(n_lat, d), ctx.reshape(nb * nctx, d)], axis=0)
    for l in range(depth):
        last = l == depth - 1
        m_out = n_lat if last else n_tok
        mod = _adaln(cvec, w_ada, b_ada3, l).reshape(SUBLANES, N_MOD, d)
        h = _normmod(xs, norm_gain, mod, l, seq, nb, tr)

        mm_in = lambda off, n, shift, tn, name: _mmt(h, w_in_t, l, off, n, shift, tn, n_tok, tw, name)
        pa = mm_in(0, 2 * mix, 0, mix, "proj_lru")
        pb = mm_in(2 * mix, mix, 0, mix, "proj_s5")
        pc = mm_in(o_c, o_ca - o_c - mix, 0, mix, "proj_gla")
        pcg = mm_in(o_ca - mix, mix, 0, mix, "proj_gla_gate")
        za = mm_in(o_ca, za_w, 0, za_w, "proj_gla_rank")
        pd = mm_in(o_ca, o_dg, gla_rank, mix, "proj_hgrn")
        pdg = mm_in(o_ca + o_dg, mix, gla_rank, mix, "proj_hgrn_gate")
        pg = mm_in(o_ca + o_gt, N_BRANCH * d, gla_rank, mix, "proj_gate")

        wd, bd, sp = _lru_params(lru_w_a[l], lru_b_a[l], lru_w_i[l], lru_b_i[l], lru_lambda[l])
        hf, hb = _lru(pa, lru_conv_w[l], lru_conv_b[l][None, :], wd, bd, sp, nb, seq, nctx, tt)
        ya = _lru_out(pa, hf, hb, tm)

        s5p = _s5_params(s5_a_re[l], s5_a_im[l], s5_log_dt[l], s5_b_re[l], s5_b_im[l], s5_c_re[l], s5_c_im[l])
        yf, ybk = _s5(pb, *s5p, nb, seq, nctx, tt)
        yb = _to_row_major(_s5_glu(pb, yf, ybk, s5_d[:, None, :], s5_w_glu, l, n_lat, seq, tm), nb, seq)

        wa = jnp.zeros((2, za_w, GLA_HEADS * gla_dk), F32).at[:, :gla_rank].set(gla_w_alpha[l]).astype(BF16)
        of, ob = _gla(pc, za, wa, gla_b_alpha[l][:, None, :], nb, seq, nctx, tt, ch, GLA_HEADS, gla_dk, gla_dv)
        yc = _headnorm(of, ob, pcg, 0, gla_norm[:, None, :], l, GLA_HEADS, tm)

        of, ob = _hgrn(pd, lower[l][None, :], nb, seq, nctx, tt, ch, HGRN_HEADS, hg_dk, hg_dv)
        yd = _to_row_major(_headnorm(of, ob, pdg, 0, hgrn_norm[:, None, :], l, HGRN_HEADS, tm, (n_lat, seq)), nb, seq)

        zm = _merge((ya, yb, yc, yd), pg, w_branch, l, m_out, tm)
        mo = _mm(zm, w_out, l, d, F32, m_out, tw, mix, "proj_out")
        xs, f, ft = _resid(xs, mo, norm_gain, mod, l, m_out, seq, nb, tr, 2, 1, True, False)

        q = _mm(f, peer_w_q, l, peer_w_q.shape[2], F32, m_out, tw, mix, "peer_query")
        e1, cnt, rank2, e2 = _route(q, peer_keys, l, _tile(m_out, (256, 128)))
        yt = _peer(ft, u_tab, vt_tab, l, e1, cnt, rank2, e2, _tile(m_out, (512, 256, 128)), 512)
        xs = _resid(xs, yt, norm_gain, mod, l, m_out, seq, nb, tr, 5, 3, False, True)[0]
    return xs[:n_lat].reshape(nb, seq, d)
```

```python
import functools
import math

import jax
import jax.numpy as jnp
from jax import lax
from jax.experimental import pallas as pl
from jax.experimental.pallas import tpu as pltpu

F32 = jnp.float32
BF16 = jnp.bfloat16
EPS = 1e-6
GRID_W = 64
N_MOD = 6
N_BRANCH = 4
LRU_BLOCKS = 16
LRU_C = 8.0
S5_GROUP = 16
S5_STATE = 64
S5_CHUNKS = 4
GLA_HEADS = 4
GLA_TAU = 16.0
HGRN_HEADS = 8
PEER_HEADS = 8
N_KEYS = 128
PEER_TOPK = 16
NO_RANK = 255.0
SUBLANES = 8
LANES = 128
VMEM_LIMIT = 56 * 1024 * 1024

NT_DIMS = (((1,), (1,)), ((), ()))
TN_DIMS = (((0,), (0,)), ((), ()))


def _params(n_axes, vmem=VMEM_LIMIT):
    return pltpu.CompilerParams(dimension_semantics=("arbitrary",) * n_axes, vmem_limit_bytes=vmem)


def _tile(n, prefs):
    for p in prefs:
        if n % p == 0:
            return p
    raise ValueError(f"no tile for {n} in {prefs}")


def _rms(x):
    return x * lax.rsqrt(jnp.mean(x * x, axis=-1, keepdims=True) + EPS)


def _ada_body(c_ref, w_ref, b_ref, o_ref):
    c = c_ref[...]
    s = (c * jax.nn.sigmoid(c)).astype(BF16)
    o_ref[...] = jnp.dot(s, w_ref[...].astype(BF16), preferred_element_type=F32) + b_ref[...]


def _adaln(cvec, w_ada, b_ada3, l):
    rows, d = cvec.shape
    n = w_ada.shape[2]
    tn = 512
    return pl.pallas_call(
        _ada_body,
        grid=(n // tn,),
        in_specs=[
            pl.BlockSpec((rows, d), lambda j: (0, 0)),
            pl.BlockSpec((None, d, tn), lambda j: (l, 0, j)),
            pl.BlockSpec((None, 1, tn), lambda j: (l, 0, j)),
        ],
        out_specs=pl.BlockSpec((rows, tn), lambda j: (0, j)),
        out_shape=jax.ShapeDtypeStruct((rows, n), F32),
        compiler_params=_params(1),
        name="adaln",
    )(cvec, w_ada, b_ada3)


def _normmod_body(x_ref, g_ref, mod_ref, o_ref, *, gi, shift_i, scale_i):
    y = _rms(x_ref[...]) * g_ref[gi:gi + 1, :]
    o_ref[...] = (y * (1.0 + mod_ref[scale_i:scale_i + 1, :]) + mod_ref[shift_i:shift_i + 1, :]).astype(o_ref.dtype)


def _mod_row(i, tm, seq, nb):
    return jnp.minimum((i * tm) // seq, nb)


def _normmod(x, gain, mod, l, seq, nb, tm):
    t, d = x.shape
    return pl.pallas_call(
        functools.partial(_normmod_body, gi=0, shift_i=0, scale_i=1),
        grid=(t // tm,),
        in_specs=[
            pl.BlockSpec((tm, d), lambda i: (i, 0)),
            pl.BlockSpec((None, 4, d), lambda i: (l, 0, 0)),
            pl.BlockSpec((None, N_MOD, d), lambda i: (_mod_row(i, tm, seq, nb), 0, 0)),
        ],
        out_specs=pl.BlockSpec((tm, d), lambda i: (i, 0)),
        out_shape=jax.ShapeDtypeStruct((t, d), BF16),
        compiler_params=_params(1),
        name="normmod",
    )(x, gain, mod)


def _mm_body(a_ref, w_ref, o_ref, wb_ref):
    @pl.when(pl.program_id(1) == 0)
    def _():
        wb_ref[...] = w_ref[...].astype(BF16)

    o_ref[...] = jnp.dot(a_ref[...], wb_ref[...], preferred_element_type=F32).astype(o_ref.dtype)


def _mm(a, w, l, ncols, out_dtype, m, tm, tn, name):
    k = a.shape[1]
    w_spec = pl.BlockSpec((None, k, tn), lambda j, i: (l, 0, j))
    return pl.pallas_call(
        _mm_body,
        grid=(ncols // tn, m // tm),
        in_specs=[pl.BlockSpec((tm, k), lambda j, i: (i, 0)), w_spec],
        out_specs=pl.BlockSpec((tm, tn), lambda j, i: (i, j)),
        out_shape=jax.ShapeDtypeStruct((m, ncols), out_dtype),
        scratch_shapes=[pltpu.VMEM((k, tn), BF16)],
        compiler_params=_params(2),
        name=name,
    )(a, w)


def _mmt_body(a_ref, w_ref, *rest, shift):
    if shift:
        tail_ref, o_ref, wb_ref = rest
    else:
        o_ref, wb_ref = rest
    tn = wb_ref.shape[0]

    @pl.when(pl.program_id(1) == 0)
    def _():
        if shift:
            wb_ref[0:tn - shift, :] = w_ref[shift:tn, :].astype(BF16)
            wb_ref[tn - shift:tn, :] = tail_ref[...].astype(BF16)
        else:
            wb_ref[...] = w_ref[...].astype(BF16)

    o_ref[...] = lax.dot_general(a_ref[...], wb_ref[...], NT_DIMS, preferred_element_type=F32).astype(o_ref.dtype)


def _mmt(a, wt, l, row_off, ncols, shift, tn, m, tm, name):
    k = a.shape[1]
    base = row_off // tn
    in_specs = [pl.BlockSpec((tm, k), lambda j, i: (i, 0)), pl.BlockSpec((None, tn, k), lambda j, i: (l, base + j, 0))]
    args = [a, wt]
    if shift:
        per = tn // shift
        in_specs.append(pl.BlockSpec((None, shift, k), lambda j, i: (l, (base + j + 1) * per, 0)))
        args.append(wt)
    return pl.pallas_call(
        functools.partial(_mmt_body, shift=shift),
        grid=(ncols // tn, m // tm),
        in_specs=in_specs,
        out_specs=pl.BlockSpec((tm, tn), lambda j, i: (i, j)),
        out_shape=jax.ShapeDtypeStruct((m, ncols), F32),
        scratch_shapes=[pltpu.VMEM((tn, k), BF16)],
        compiler_params=_params(2),
        name=name,
    )(*args)


def _seq_block(b, i, rev, nb, seq, ctx, tt):
    nctx, nlat = ctx // tt, seq // tt
    ic = (nctx - 1 - i) if rev else i
    il = (nlat - 1 - (i - nctx)) if rev else (i - nctx)
    return jnp.where(i < nctx, (nb * seq) // tt + b * nctx + ic, b * nlat + il)


def _seq_spec(cols, col_blk, rev, nb, seq, ctx, tt):
    return pl.BlockSpec((tt, cols), lambda b, i: (_seq_block(b, i, rev, nb, seq, ctx, tt), col_blk))


def _grid_view(arr):
    t, c = arr.shape
    return arr.reshape(t // GRID_W, GRID_W, c)


def _col_tokens(lat_ref, first, k, a, b):
    return jnp.concatenate([lat_ref[:, first + j, a:b] for j in range(k)], axis=0)


def _lat_step(i, rev, seq, ctx, tt):
    nctx, nlat = ctx // tt, seq // tt
    il = (nlat - 1 - (i - nctx)) if rev else (i - nctx)
    return jnp.clip(il, 0, nlat - 1)


def _col_lat_spec(width, col_blk, rev, seq, ctx, tt):
    rows = seq // GRID_W
    per = SUBLANES // (tt // rows)
    return pl.BlockSpec((rows, SUBLANES, width), lambda b, i: (b, _lat_step(i, rev, seq, ctx, tt) // per, col_blk))


def _first_col(i, rev, seq, ctx, tt):
    k = tt // (seq // GRID_W)
    return (_lat_step(i, rev, seq, ctx, tt) % (SUBLANES // k)) * k


def _ctx_spec(cols, col_blk, rev, nb, seq, ctx, tt):
    nctx = ctx // tt

    def index(b, i):
        ic = (nctx - 1 - i) if rev else i
        return (nb * seq) // tt + b * nctx + jnp.clip(ic, 0, nctx - 1), col_blk

    return pl.BlockSpec((tt, cols), index)


def _halo_spec(cols, col_blk, rev, side, nb, seq, ctx, tt, total):
    per = tt // SUBLANES
    last = total // SUBLANES - 1

    def index(b, i):
        blk = _seq_block(b, i, rev, nb, seq, ctx, tt) * per
        blk = blk - 1 if side < 0 else blk + per
        return jnp.clip(blk, 0, last), col_blk

    return pl.BlockSpec((SUBLANES, cols), index)


def _stream_pos(i, rev, seq, ctx, tt):
    nctx, nlat = ctx // tt, seq // tt
    is_ctx = i < nctx
    ii = jnp.where(is_ctx, (nctx - 1 - i) if rev else i, (nlat - 1 - (i - nctx)) if rev else (i - nctx))
    n = jnp.where(is_ctx, nctx, nlat)
    return ii == 0, ii == n - 1


def _scan_tile_real(a, u, carry, rev):
    row = lax.broadcasted_iota(jnp.int32, a.shape, 0)
    for s in (1, 2, 4):
        if rev:
            a_sh, u_sh = pltpu.roll(a, SUBLANES - s, 0), pltpu.roll(u, SUBLANES - s, 0)
            ok = row < SUBLANES - s
        else:
            a_sh, u_sh = pltpu.roll(a, s, 0), pltpu.roll(u, s, 0)
            ok = row >= s
        u = jnp.where(ok, a * u_sh + u, u)
        a = jnp.where(ok, a * a_sh, a)
    h = u + a * carry
    return h, (h[0:1] if rev else h[SUBLANES - 1:SUBLANES])


def _lru_body(xf_ref, pf_ref, nf_ref, xb_ref, pb_ref, nb_ref, cw_ref, cb_ref, wd_ref, bd_ref, sp_ref,
              hf_ref, hb_ref, a_s, u_s, carry_s, *, seq, ctx, tt, c):
    i = pl.program_id(1)

    @pl.when(i == 0)
    def _():
        carry_s[...] = jnp.zeros_like(carry_s)

    row = lax.broadcasted_iota(jnp.int32, (tt, c), 0)
    for d, (x_ref, p_ref, n_ref) in enumerate(((xf_ref, pf_ref, nf_ref), (xb_ref, pb_ref, nb_ref))):
        first, last = _stream_pos(i, d == 1, seq, ctx, tt)
        x = x_ref[...]
        prev = jnp.where(first, 0.0, p_ref[SUBLANES - 1:SUBLANES, :])
        nx1 = jnp.where(last, 0.0, n_ref[0:1, :])
        nx2 = jnp.where(last, 0.0, n_ref[1:2, :])
        xm1 = jnp.where(row == 0, prev, pltpu.roll(x, 1, 0))
        xp1 = jnp.where(row == tt - 1, nx1, pltpu.roll(x, tt - 1, 0))
        xp2 = jnp.where(row == tt - 1, nx2, jnp.where(row == tt - 2, nx1, pltpu.roll(x, tt - 2, 0)))
        xl = xm1 * cw_ref[0:1, :] + x * cw_ref[1:2, :] + xp1 * cw_ref[2:3, :] + xp2 * cw_ref[3:4, :] + cb_ref[...]
        z = jnp.dot(xl.astype(BF16), wd_ref[d], preferred_element_type=F32) + bd_ref[d]
        r = jax.nn.sigmoid(z[:, :c])
        gi = jax.nn.sigmoid(z[:, c:])
        log_a = -LRU_C * r * sp_ref[d]
        a = jnp.exp(log_a)
        a_s[d] = a
        u_s[d] = jnp.sqrt(-jnp.tanh(log_a) * (a * a + 1.0)) * gi * xl

    ntile = tt // SUBLANES

    def step(k, carry):
        cf, cb = carry
        rows_f = pl.ds(pl.multiple_of(k * SUBLANES, SUBLANES), SUBLANES)
        rows_b = pl.ds(pl.multiple_of((ntile - 1 - k) * SUBLANES, SUBLANES), SUBLANES)
        h, cf = _scan_tile_real(a_s[0, rows_f, :], u_s[0, rows_f, :], cf, False)
        hf_ref[rows_f, :] = h
        h, cb = _scan_tile_real(a_s[1, rows_b, :], u_s[1, rows_b, :], cb, True)
        hb_ref[rows_b, :] = h
        return cf, cb

    cf, cb = lax.fori_loop(0, ntile, step, (carry_s[0, 0:1, :], carry_s[1, 0:1, :]))
    carry_s[0, 0:1, :] = cf
    carry_s[1, 0:1, :] = cb


def _lru(pa, cw, cb, wd, bd, sp, nb, seq, ctx, tt):
    t = pa.shape[0]
    c = cw.shape[1]
    geo = (nb, seq, ctx, tt)
    full = lambda shape: pl.BlockSpec(shape, lambda b, i: (0,) * len(shape))
    return pl.pallas_call(
        functools.partial(_lru_body, seq=seq, ctx=ctx, tt=tt, c=c),
        grid=(nb, (seq + ctx) // tt),
        in_specs=[
            _seq_spec(c, 0, False, *geo), _halo_spec(c, 0, False, -1, *geo, t), _halo_spec(c, 0, False, 1, *geo, t),
            _seq_spec(c, 0, True, *geo), _halo_spec(c, 0, True, -1, *geo, t), _halo_spec(c, 0, True, 1, *geo, t),
            full(cw.shape), full(cb.shape), full(wd.shape), full(bd.shape), full(sp.shape),
        ],
        out_specs=[_seq_spec(c, 0, False, *geo), _seq_spec(c, 0, True, *geo)],
        out_shape=[jax.ShapeDtypeStruct((t, c), F32)] * 2,
        scratch_shapes=[pltpu.VMEM((2, tt, c), F32), pltpu.VMEM((2, tt, c), F32), pltpu.VMEM((2, SUBLANES, c), F32)],
        compiler_params=_params(2),
        name="rglru_scan",
    )(pa, pa, pa, pa, pa, pa, cw, cb, wd, bd, sp)


def _lru_out_body(ay_ref, hf_ref, hb_ref, o_ref):
    o_ref[...] = (jax.nn.gelu(ay_ref[...]) * (hf_ref[...] + hb_ref[...])).astype(o_ref.dtype)


def _lru_out(pa, hf, hb, tm):
    t, c = hf.shape
    return pl.pallas_call(
        _lru_out_body,
        grid=(t // tm,),
        in_specs=[pl.BlockSpec((tm, c), lambda i: (i, 1)), pl.BlockSpec((tm, c), lambda i: (i, 0)),
                  pl.BlockSpec((tm, c), lambda i: (i, 0))],
        out_specs=pl.BlockSpec((tm, c), lambda i: (i, 0)),
        out_shape=jax.ShapeDtypeStruct((t, c), BF16),
        compiler_params=_params(1),
        name="rglru_out",
    )(pa, hf, hb)


def _scan_tile_cplx(xr, xi, dbl_ref, pw_ref, d, lanes, cr, ci, rev):
    for n, s in enumerate((1, 2, 4)):
        ar = dbl_ref[d, 0, n, :, lanes]
        ai = dbl_ref[d, 1, n, :, lanes]
        shift = SUBLANES - s if rev else s
        sr, si = pltpu.roll(xr, shift, 0), pltpu.roll(xi, shift, 0)
        xr, xi = xr + ar * sr - ai * si, xi + ar * si + ai * sr
    pr = pw_ref[d, 0, :, lanes]
    pi = pw_ref[d, 1, :, lanes]
    hr = xr + pr * cr - pi * ci
    hi = xi + pr * ci + pi * cr
    sel = slice(0, 1) if rev else slice(SUBLANES - 1, SUBLANES)
    return hr, hi, hr[sel], hi[sel]


def _load_seq_block(dst_ref, d, lat_ref, ctx_ref, is_ctx, first, k):
    @pl.when(is_ctx)
    def _():
        dst_ref[d] = ctx_ref[...]

    @pl.when(jnp.logical_not(is_ctx))
    def _():
        dst_ref[d] = _col_tokens(lat_ref, first, k, 0, lat_ref.shape[2])


def _s5_body(ulf_ref, ucf_ref, ulb_ref, ucb_ref, bre_ref, bim_ref, cre_ref, cim_ref, dbl_ref, pw_ref, yf_ref, yb_ref,
             u_s, hr_s, hi_s, carry_s, *, tt, gw, sw, seq, ctx):
    i = pl.program_id(1)

    @pl.when(i == 0)
    def _():
        carry_s[...] = jnp.zeros_like(carry_s)

    for d, (lat_ref, ctx_ref) in enumerate(((ulf_ref, ucf_ref), (ulb_ref, ucb_ref))):
        _load_seq_block(u_s, d, lat_ref, ctx_ref, i < ctx // tt, _first_col(i, d == 1, seq, ctx, tt),
                        tt // (seq // GRID_W))

    ntile = tt // SUBLANES
    for ch in range(S5_CHUNKS):
        cols = slice(ch * gw, (ch + 1) * gw)
        lanes = slice(ch * sw, (ch + 1) * sw)
        for d in range(2):
            u = u_s[d, :, cols].astype(BF16)
            hr_s[d] = jnp.dot(u, bre_ref[d, ch], preferred_element_type=F32)
            hi_s[d] = jnp.dot(u, bim_ref[d, ch], preferred_element_type=F32)

        def step(k, carry):
            crf, cif, crb, cib = carry
            rows_f = pl.ds(pl.multiple_of(k * SUBLANES, SUBLANES), SUBLANES)
            rows_b = pl.ds(pl.multiple_of((ntile - 1 - k) * SUBLANES, SUBLANES), SUBLANES)
            hr, hi, crf, cif = _scan_tile_cplx(hr_s[0, rows_f, :], hi_s[0, rows_f, :], dbl_ref, pw_ref, 0, lanes,
                                               crf, cif, False)
            hr_s[0, rows_f, :] = hr
            hi_s[0, rows_f, :] = hi
            hr, hi, crb, cib = _scan_tile_cplx(hr_s[1, rows_b, :], hi_s[1, rows_b, :], dbl_ref, pw_ref, 1, lanes,
                                               crb, cib, True)
            hr_s[1, rows_b, :] = hr
            hi_s[1, rows_b, :] = hi
            return crf, cif, crb, cib

        init = tuple(carry_s[n, 0:1, lanes] for n in range(4))
        fin = lax.fori_loop(0, ntile, step, init)
        for n in range(4):
            carry_s[n, 0:1, lanes] = fin[n]
        for d, y_ref in enumerate((yf_ref, yb_ref)):
            y_ref[:, cols] = (jnp.dot(hr_s[d].astype(BF16), cre_ref[d, ch], preferred_element_type=F32)
                              - jnp.dot(hi_s[d].astype(BF16), cim_ref[d, ch], preferred_element_type=F32))


def _s5(u, bre, bim, cre, cim, dbl, pw, nb, seq, ctx, tt):
    t, c = u.shape
    gw = c // S5_CHUNKS
    sw = bre.shape[-1]
    geo = (nb, seq, ctx, tt)
    full = lambda shape: pl.BlockSpec(shape, lambda b, i: (0,) * len(shape))
    lat = lambda rev: _col_lat_spec(c, 0, rev, seq, ctx, tt)
    cx = lambda rev: _ctx_spec(c, 0, rev, *geo)
    uv = _grid_view(u)
    return pl.pallas_call(
        functools.partial(_s5_body, tt=tt, gw=gw, sw=sw, seq=seq, ctx=ctx),
        grid=(nb, (seq + ctx) // tt),
        in_specs=[lat(False), cx(False), lat(True), cx(True),
                  full(bre.shape), full(bim.shape), full(cre.shape), full(cim.shape), full(dbl.shape), full(pw.shape)],
        out_specs=[_seq_spec(c, 0, False, *geo), _seq_spec(c, 0, True, *geo)],
        out_shape=[jax.ShapeDtypeStruct((t, c), F32)] * 2,
        scratch_shapes=[pltpu.VMEM((2, tt, c), F32), pltpu.VMEM((2, tt, sw), F32), pltpu.VMEM((2, tt, sw), F32),
                        pltpu.VMEM((4, SUBLANES, sw * S5_CHUNKS), F32)],
        compiler_params=_params(2),
        name="s5_scan",
    )(uv, u, uv, u, bre, bim, cre, cim, dbl, pw)


def _tile_specs(cols, col_blk, n_lat, seq, tm):
    rows = seq // GRID_W
    nlat, per = n_lat // tm, seq // tm

    def lat_index(i):
        il = jnp.minimum(i, nlat - 1)
        return il // per, il % per

    def lat_index3(i):
        b, blk = lat_index(i)
        return b, blk, col_blk

    return (pl.BlockSpec((rows, tm // rows, cols), lat_index3),
            pl.BlockSpec((tm, cols), lambda i: (jnp.maximum(i, nlat), col_blk)))


def _load_tile(dst_ref, lat_ref, ctx_ref, is_ctx):
    @pl.when(is_ctx)
    def _():
        dst_ref[...] = ctx_ref[...]

    @pl.when(jnp.logical_not(is_ctx))
    def _():
        dst_ref[...] = _col_tokens(lat_ref, 0, lat_ref.shape[1], 0, lat_ref.shape[2])


def _s5_glu_body(ul_ref, uc_ref, yf_ref, yb_ref, d_ref, w_ref, o_ref, wb_ref, u_s, *, c, nlat):
    @pl.when(pl.program_id(0) == 0)
    def _():
        wb_ref[...] = w_ref[...].astype(BF16)

    _load_tile(u_s, ul_ref, uc_ref, pl.program_id(0) >= nlat)
    y = jax.nn.gelu(u_s[...] * d_ref[...] + yf_ref[...] + yb_ref[...])
    z = jnp.dot(y.astype(BF16), wb_ref[...], preferred_element_type=F32)
    o_ref[...] = (z[:, :c] * jax.nn.sigmoid(z[:, c:])).astype(o_ref.dtype)


def _s5_glu(u, yf, yb, dskip, w_glu, l, n_lat, seq, tm):
    t, c = u.shape
    lat_spec, ctx_spec = _tile_specs(c, 0, n_lat, seq, tm)
    return pl.pallas_call(
        functools.partial(_s5_glu_body, c=c, nlat=n_lat // tm),
        grid=(t // tm,),
        in_specs=[lat_spec, ctx_spec] + [pl.BlockSpec((tm, c), lambda i: (i, 0))] * 2
        + [pl.BlockSpec((None, 1, c), lambda i: (l, 0, 0)), pl.BlockSpec((None, c, 2 * c), lambda i: (l, 0, 0))],
        out_specs=pl.BlockSpec((tm, c), lambda i: (i, 0)),
        out_shape=jax.ShapeDtypeStruct((t, c), BF16),
        scratch_shapes=[pltpu.VMEM((c, 2 * c), BF16), pltpu.VMEM((tm, c), F32)],
        compiler_params=_params(1),
        name="s5_glu",
    )(_grid_view(u), u, yf, yb, dskip, w_glu)


def _chunk_cumsum(x, rev, ch):
    row = lax.broadcasted_iota(jnp.int32, x.shape, 0)
    s = 1
    while s < ch:
        if rev:
            x = x + jnp.where(row < ch - s, pltpu.roll(x, ch - s, 0), 0.0)
        else:
            x = x + jnp.where(row >= s, pltpu.roll(x, s, 0), 0.0)
        s *= 2
    return x


def _chunk_scan(q_s, k_s, v_s, lf_s, st_s, o_refs, *, tt, ch, heads, dk, dv):
    nchunk = tt // ch
    tri_r = lax.broadcasted_iota(jnp.int32, (ch, ch), 0)
    tri_c = lax.broadcasted_iota(jnp.int32, (ch, ch), 1)

    def step(n, _):
        for d in range(2):
            rev = d == 1
            cc = (nchunk - 1 - n) if rev else n
            rows = pl.ds(pl.multiple_of(cc * ch, ch), ch)
            b = _chunk_cumsum(lf_s[d, rows, :], rev, ch)
            piv = ch // 2 if rev else ch // 2 - 1
            end = 0 if rev else ch - 1
            m = b[piv:piv + 1, :]
            bl = b[end:end + 1, :]
            qm = q_s[d, rows, :] * jnp.exp(b - m)
            km = k_s[d, rows, :] * jnp.exp(m - b)
            qg = (qm * jnp.exp(m)).astype(BF16)
            kg = (km * jnp.exp(bl - m)).astype(BF16)
            dec = jnp.exp(bl)
            qm = qm.astype(BF16)
            km = km.astype(BF16)
            v = v_s[d, rows, :].astype(BF16)
            keep = (tri_r <= tri_c) if rev else (tri_r >= tri_c)
            for h in range(heads):
                ks = slice(h * dk, (h + 1) * dk)
                vs = slice(h * dv, (h + 1) * dv)
                sc = lax.dot_general(qm[:, ks], km[:, ks], NT_DIMS, preferred_element_type=F32)
                sc = jnp.where(keep, sc, 0.0).astype(BF16)
                st = st_s[d, h]
                o = jnp.dot(sc, v[:, vs], preferred_element_type=F32)
                o = o + lax.dot_general(qg[:, ks], st.astype(BF16), NT_DIMS, preferred_element_type=F32)
                st_s[d, h] = st * dec[:, ks] + lax.dot_general(v[:, vs], kg[:, ks], TN_DIMS,
                                                               preferred_element_type=F32)
                o_refs[d][rows, vs] = o
        return 0

    lax.fori_loop(0, nchunk, step, 0)


def _gla_body(qf_ref, zf_ref, qb_ref, zb_ref, wa_ref, ba_ref, of_ref, ob_ref, q_s, k_s, v_s, lf_s, st_s,
              *, tt, ch, heads, dk, dv):
    @pl.when(pl.program_id(1) == 0)
    def _():
        st_s[...] = jnp.zeros_like(st_s)

    hk = heads * dk
    for d, (x_ref, z_ref) in enumerate(((qf_ref, zf_ref), (qb_ref, zb_ref))):
        q_s[d] = x_ref[:, :hk] * dk ** -0.5
        k_s[d] = x_ref[:, hk:2 * hk]
        v_s[d] = x_ref[:, 2 * hk:]
        la = jnp.dot(z_ref[...].astype(BF16), wa_ref[d], preferred_element_type=F32) + ba_ref[d]
        lf_s[d] = jax.nn.log_sigmoid(la) / GLA_TAU
    _chunk_scan(q_s, k_s, v_s, lf_s, st_s, (of_ref, ob_ref), tt=tt, ch=ch, heads=heads, dk=dk, dv=dv)


def _hgrn_body(*refs, tt, ch, heads, dk, dv, seq, ctx):
    lat_refs, ctx_refs = (refs[0:3], refs[6:9]), (refs[3:6], refs[9:12])
    lb_ref, of_ref, ob_ref, q_s, k_s, v_s, lf_s, st_s = refs[12:]
    i = pl.program_id(1)

    @pl.when(i == 0)
    def _():
        st_s[...] = jnp.zeros_like(st_s)

    lb = lb_ref[...]

    def prep(d, z, q, v):
        q_s[d] = jax.nn.silu(q) * dk ** -0.5
        v_s[d] = v
        k_s[d] = (1.0 - lb) * jax.nn.sigmoid(-z)
        lf_s[d] = jnp.log(lb + (1.0 - lb) * jax.nn.sigmoid(z))

    k_cols = tt // (seq // GRID_W)
    for d in range(2):
        @pl.when(i < ctx // tt)
        def _():
            prep(d, *(r[...] for r in ctx_refs[d]))

        @pl.when(i >= ctx // tt)
        def _():
            first = _first_col(i, d == 1, seq, ctx, tt)
            prep(d, *(_col_tokens(r, first, k_cols, 0, r.shape[2]) for r in lat_refs[d]))

    _chunk_scan(q_s, k_s, v_s, lf_s, st_s, (of_ref, ob_ref), tt=tt, ch=ch, heads=heads, dk=dk, dv=dv)


def _chunk_scratch(tt, heads, dk, dv):
    return [pltpu.VMEM((2, tt, heads * dk), F32), pltpu.VMEM((2, tt, heads * dk), F32),
            pltpu.VMEM((2, tt, heads * dv), F32), pltpu.VMEM((2, tt, heads * dk), F32),
            pltpu.VMEM((2, heads, dv, dk), F32)]


def _gla(pc, za, wa, ba, nb, seq, ctx, tt, ch, heads, dk, dv):
    t = pc.shape[0]
    geo = (nb, seq, ctx, tt)
    wide = 2 * heads * dk + heads * dv
    full = lambda shape: pl.BlockSpec(shape, lambda b, i: (0,) * len(shape))
    return pl.pallas_call(
        functools.partial(_gla_body, tt=tt, ch=ch, heads=heads, dk=dk, dv=dv),
        grid=(nb, (seq + ctx) // tt),
        in_specs=[_seq_spec(wide, 0, False, *geo), _seq_spec(za.shape[1], 0, False, *geo),
                  _seq_spec(wide, 0, True, *geo), _seq_spec(za.shape[1], 0, True, *geo),
                  full(wa.shape), full(ba.shape)],
        out_specs=[_seq_spec(heads * dv, 0, False, *geo), _seq_spec(heads * dv, 0, True, *geo)],
        out_shape=[jax.ShapeDtypeStruct((t, heads * dv), F32)] * 2,
        scratch_shapes=_chunk_scratch(tt, heads, dk, dv),
        compiler_params=_params(2),
        name="gla_scan",
    )(pc, za, pc, za, wa, ba)


def _hgrn(pd, lb, nb, seq, ctx, tt, ch, heads, dk, dv):
    t = pd.shape[0]
    geo = (nb, seq, ctx, tt)
    hk = heads * dk
    lat = lambda d: [_col_lat_spec(hk, blk, d == 1, seq, ctx, tt) for blk in (d, 2, 3)]
    cx = lambda d: [_ctx_spec(hk, blk, d == 1, *geo) for blk in (d, 2, 3)]
    pv = _grid_view(pd)
    return pl.pallas_call(
        functools.partial(_hgrn_body, tt=tt, ch=ch, heads=heads, dk=dk, dv=dv, seq=seq, ctx=ctx),
        grid=(nb, (seq + ctx) // tt),
        in_specs=lat(0) + cx(0) + lat(1) + cx(1) + [pl.BlockSpec(lb.shape, lambda b, i: (0, 0))],
        out_specs=[_seq_spec(heads * dv, 0, False, *geo), _seq_spec(heads * dv, 0, True, *geo)],
        out_shape=[jax.ShapeDtypeStruct((t, heads * dv), F32)] * 2,
        scratch_shapes=_chunk_scratch(tt, heads, dk, dv),
        compiler_params=_params(2),
        name="hgrn_scan",
    )(*([pv] * 3 + [pd] * 3) * 2, lb)


def _headnorm_body(of_ref, ob_ref, *rest, heads, dv, nlat):
    if nlat is None:
        g_ref, n_ref, o_ref = rest
        gate = g_ref[...]
    else:
        gl_ref, gc_ref, n_ref, o_ref, g_s = rest
        _load_tile(g_s, gl_ref, gc_ref, pl.program_id(0) >= nlat)
        gate = g_s[...]
    o = of_ref[...] + ob_ref[...]
    parts = [_rms(o[:, h * dv:(h + 1) * dv]) for h in range(heads)]
    y = jnp.concatenate(parts, axis=-1) * n_ref[...]
    o_ref[...] = (y * jax.nn.silu(gate)).astype(o_ref.dtype)


def _headnorm(of, ob, gate_arr, gate_blk, gain, l, heads, tm, col_major=None):
    t, c = of.shape
    row = pl.BlockSpec((tm, c), lambda i: (i, 0))
    if col_major is None:
        gate_specs, gate_args, scratch, nlat = [pl.BlockSpec((tm, c), lambda i: (i, gate_blk))], [gate_arr], [], None
    else:
        n_lat, seq = col_major
        gate_specs = list(_tile_specs(c, gate_blk, n_lat, seq, tm))
        gate_args, scratch, nlat = [_grid_view(gate_arr), gate_arr], [pltpu.VMEM((tm, c), F32)], n_lat // tm
    return pl.pallas_call(
        functools.partial(_headnorm_body, heads=heads, dv=c // heads, nlat=nlat),
        grid=(t // tm,),
        in_specs=[row, row] + gate_specs + [pl.BlockSpec((None, 1, c), lambda i: (l, 0, 0))],
        out_specs=row,
        out_shape=jax.ShapeDtypeStruct((t, c), BF16),
        scratch_shapes=scratch,
        compiler_params=_params(1),
        name="headnorm_gate",
    )(of, ob, *gate_args, gain)


def _merge_body(ya_ref, yb_ref, yc_ref, yd_ref, g0_ref, g1_ref, g2_ref, g3_ref, w_ref, o_ref, wb_ref):
    @pl.when(pl.program_id(1) == 0)
    def _():
        wb_ref[...] = w_ref[...].astype(BF16)

    acc = None
    for k, (y_ref, g_ref) in enumerate(((ya_ref, g0_ref), (yb_ref, g1_ref), (yc_ref, g2_ref), (yd_ref, g3_ref))):
        term = jax.nn.sigmoid(g_ref[...]) * jnp.dot(y_ref[...], wb_ref[k], preferred_element_type=F32)
        acc = term if acc is None else acc + term
    o_ref[...] = acc.astype(o_ref.dtype)


def _merge(ys, pg, w_branch, l, m, tm):
    c = ys[0].shape[1]
    d = w_branch.shape[3]
    tn = 512
    nj = d // tn
    y_spec = pl.BlockSpec((tm, c), lambda j, i: (i, 0))
    g_specs = [pl.BlockSpec((tm, tn), functools.partial(lambda j, i, k: (i, k * nj + j), k=k)) for k in range(N_BRANCH)]
    return pl.pallas_call(
        _merge_body,
        grid=(nj, m // tm),
        in_specs=[y_spec] * N_BRANCH + g_specs + [pl.BlockSpec((None, N_BRANCH, c, tn), lambda j, i: (l, 0, 0, j))],
        out_specs=pl.BlockSpec((tm, tn), lambda j, i: (i, j)),
        out_shape=jax.ShapeDtypeStruct((m, d), BF16),
        scratch_shapes=[pltpu.VMEM((N_BRANCH, c, tn), BF16)],
        compiler_params=_params(2),
        name="branch_merge",
    )(*ys, pg, pg, pg, pg, w_branch)


def _resid_body(x_ref, y_ref, g_ref, mod_ref, xo_ref, *f_refs, gate_i, gy, gf, shift_i, scale_i, y_transposed):
    y = y_ref[...].T if y_transposed else y_ref[...]
    xn = x_ref[...] + mod_ref[gate_i:gate_i + 1, :] * (_rms(y) * g_ref[gy:gy + 1, :])
    xo_ref[...] = xn
    if f_refs:
        f = _rms(xn) * g_ref[gf:gf + 1, :]
        f = f * (1.0 + mod_ref[scale_i:scale_i + 1, :]) + mod_ref[shift_i:shift_i + 1, :]
        f_refs[0][...] = f.astype(BF16)
        f_refs[1][...] = f.T.astype(BF16)


def _resid(x, y, gain, mod, l, m, seq, nb, tm, gate_i, gy, with_f, y_transposed):
    d = x.shape[1]
    row = pl.BlockSpec((tm, d), lambda i: (i, 0))
    col = pl.BlockSpec((d, tm), lambda i: (0, i))
    out_shape = [jax.ShapeDtypeStruct((m, d), F32)]
    out_specs = [row]
    if with_f:
        out_shape += [jax.ShapeDtypeStruct((m, d), BF16), jax.ShapeDtypeStruct((d, m), BF16)]
        out_specs += [row, col]
    return pl.pallas_call(
        functools.partial(_resid_body, gate_i=gate_i, gy=gy, gf=2, shift_i=3, scale_i=4, y_transposed=y_transposed),
        grid=(m // tm,),
        in_specs=[row, col if y_transposed else row, pl.BlockSpec((None, 4, d), lambda i: (l, 0, 0)),
                  pl.BlockSpec((None, N_MOD, d), lambda i: (_mod_row(i, tm, seq, nb), 0, 0))],
        out_specs=out_specs,
        out_shape=out_shape,
        compiler_params=_params(1),
        name="residual",
    )(x, y, gain, mod)


def _top_rows(x, vals_ref, idx_ref, want_rank):
    n_rows = x.shape[0]
    iota = lax.broadcasted_iota(jnp.int32, x.shape, 0)

    def step(r, carry):
        x = carry[0]
        mx = jnp.max(x, axis=0, keepdims=True)
        vals_ref[pl.ds(r, 1), :] = mx
        first = jnp.min(jnp.where(x == mx, iota, n_rows), axis=0, keepdims=True)
        idx_ref[pl.ds(r, 1), :] = first
        hit = iota == first
        x = jnp.where(hit, -jnp.inf, x)
        return (x, jnp.where(hit, jnp.asarray(r, F32), carry[1])) if want_rank else (x,)

    init = (x, jnp.full(x.shape, NO_RANK, F32)) if want_rank else (x,)
    return lax.fori_loop(0, PEER_TOPK, step, init)[-1]


def _route_body(q_ref, keys_ref, e1_ref, cnt_ref, rank2_ref, e2_ref, va_s, vb_s, vc_s, ia_s, ib_s, *, dq):
    half = PEER_TOPK // 2
    iota = lax.broadcasted_iota(jnp.int32, (N_KEYS, q_ref.shape[0]), 0)
    for h in range(PEER_HEADS):
        scores = []
        for p in range(2):
            qh = q_ref[:, h * 2 * dq + p * dq:h * 2 * dq + (p + 1) * dq].astype(BF16)
            scores.append(lax.dot_general(keys_ref[h, p].astype(BF16), qh, NT_DIMS, preferred_element_type=F32))
        _top_rows(scores[0], va_s, ia_s, False)
        rank2 = _top_rows(scores[1], vb_s, ib_s, True)
        a, b = va_s[...], vb_s[...]
        cand = jnp.concatenate([a[0:1, :] + b] + [a[r:r + 1, :] + b[0:half, :] for r in range(1, half)]
                               + [a[half:, :] + b[0:1, :]], axis=0)
        _top_rows(cand, vc_s, ib_s, False)
        best = vc_s[...]
        z = jnp.sum(jnp.exp(best - best[0:1, :]), axis=0, keepdims=True)
        row = ib_s[...]
        sel_i = jnp.where(row < PEER_TOPK, 0,
                          jnp.where(row < PEER_TOPK + half * (half - 1), (row - half) // half, row - half * half))
        key_a = ia_s[...]
        cnt = jnp.zeros(scores[0].shape, F32)
        for i in range(PEER_TOPK):
            n_i = jnp.sum((sel_i == i).astype(F32), axis=0, keepdims=True)
            cnt = jnp.where(iota == key_a[i:i + 1, :], n_i, cnt)
        e1_ref[h] = jnp.exp(scores[0] - a[0:1, :]) / z
        cnt_ref[h] = cnt
        rank2_ref[h] = rank2.astype(BF16)
        e2_ref[h] = jnp.exp(scores[1] - b[0:1, :]).astype(BF16)


def _route(q, keys, l, tm):
    t = q.shape[0]
    dq = keys.shape[-1]
    spec = pl.BlockSpec((PEER_HEADS, N_KEYS, tm), lambda i: (0, 0, i))
    return pl.pallas_call(
        functools.partial(_route_body, dq=dq),
        grid=(t // tm,),
        in_specs=[pl.BlockSpec((tm, q.shape[1]), lambda i: (i, 0)),
                  pl.BlockSpec((None,) + keys.shape[1:], lambda i: (l, 0, 0, 0, 0))],
        out_specs=[spec] * 4,
        out_shape=[jax.ShapeDtypeStruct((PEER_HEADS, N_KEYS, t), dt) for dt in (F32, F32, BF16, BF16)],
        scratch_shapes=[pltpu.VMEM((PEER_TOPK, tm), F32)] * 3 + [pltpu.VMEM((PEER_TOPK, tm), jnp.int32)] * 2,
        compiler_params=_params(1),
        name="peer_route",
    )(q, keys)


def _peer_body(ft_ref, u_ref, vt_ref, e1_ref, cnt_ref, rank2_ref, e2_ref, o_ref, g_s, act_s, *, te, nj):
    j = pl.program_id(1)

    @pl.when(j == 0)
    def _():
        o_ref[...] = jnp.zeros_like(o_ref)
        act_s[...] = jnp.zeros_like(act_s)

    tile = jnp.maximum(j - 1, 0)
    tm = act_s.shape[1]
    zero = jnp.zeros((), BF16)
    for r in range(te // N_KEYS):
        i1 = tile * (te // N_KEYS) + r
        rows = slice(r * N_KEYS, (r + 1) * N_KEYS)
        e1_rows = [e1_ref[h, pl.ds(i1, 1), :].astype(BF16) for h in range(PEER_HEADS)]
        cnt_rows = [cnt_ref[h, pl.ds(i1, 1), :].astype(BF16) for h in range(PEER_HEADS)]
        for cb in range(tm // LANES):
            cs = slice(cb * LANES, (cb + 1) * LANES)
            w = None
            for h in range(PEER_HEADS):
                term = jnp.where(rank2_ref[h, :, cs] < cnt_rows[h][:, cs], e2_ref[h, :, cs] * e1_rows[h][:, cs], zero)
                w = term if w is None else w + term
            g_s[rows, cs] = w * act_s[rows, cs].astype(BF16)

    o_ref[...] += jnp.dot(vt_ref[...], g_s[...], preferred_element_type=F32)
    act_s[...] = jax.nn.gelu(jnp.dot(u_ref[...], ft_ref[...], preferred_element_type=F32))


def _peer(ft, u, vt, l, e1, cnt, rank2, e2, tm, te):
    d, t = ft.shape
    nj = u.shape[1] // te
    once = pl.Buffered(1)
    big_spec = pl.BlockSpec((PEER_HEADS, N_KEYS, tm), lambda i, j: (0, 0, i), pipeline_mode=once)
    return pl.pallas_call(
        functools.partial(_peer_body, te=te, nj=nj),
        grid=(t // tm, nj + 1),
        in_specs=[pl.BlockSpec((d, tm), lambda i, j: (0, i), pipeline_mode=once),
                  pl.BlockSpec((None, te, d), lambda i, j: (l, jnp.minimum(j, nj - 1), 0)),
                  pl.BlockSpec((None, d, te), lambda i, j: (l, 0, jnp.maximum(j - 1, 0))),
                  big_spec, big_spec, big_spec, big_spec],
        out_specs=pl.BlockSpec((d, tm), lambda i, j: (0, i)),
        out_shape=jax.ShapeDtypeStruct((d, t), F32),
        scratch_shapes=[pltpu.VMEM((te, tm), BF16), pltpu.VMEM((te, tm), F32)],
        compiler_params=_params(2),
        name="peer_experts",
    )(ft, u, vt, e1, cnt, rank2, e2)


def _block_diag(w):
    n, r, c = w.shape
    eye = jnp.eye(n, dtype=w.dtype)
    return (w[:, :, None, :] * eye[:, None, :, None]).reshape(n * r, n * c)


def _lru_params(w_a, b_a, w_i, b_i, lam):
    wd = jnp.stack([jnp.concatenate([_block_diag(w_a[d]), _block_diag(w_i[d])], axis=1) for d in range(2)])
    bd = jnp.stack([jnp.concatenate([b_a[d], b_i[d]])[None, :] for d in range(2)])
    sp = jax.nn.softplus(-lam)[:, None, :]
    return wd.astype(BF16), bd, sp


def _s5_params(a_re, a_im, log_dt, b_re, b_im, c_re, c_im):
    g, p = a_re.shape[1:]
    gc = g // S5_CHUNKS
    outs = [[] for _ in range(6)]
    for d in range(2):
        big_a = lax.complex(a_re[d], a_im[d])
        dt = jnp.exp(log_dt[d])[:, None]
        a_bar = jnp.exp(big_a * dt)
        b_bar = ((a_bar - 1.0) / big_a)[..., None] * lax.complex(b_re[d], b_im[d])
        c_mat = lax.complex(c_re[d], c_im[d])
        bm = jnp.swapaxes(b_bar, 1, 2).reshape(S5_CHUNKS, gc, S5_GROUP, p)
        cm = jnp.swapaxes(c_mat, 1, 2).reshape(S5_CHUNKS, gc, p, S5_GROUP)
        bm = jnp.stack([_block_diag(bm[k]) for k in range(S5_CHUNKS)])
        cm = jnp.stack([_block_diag(cm[k]) for k in range(S5_CHUNKS)])
        pows = jnp.stack([jnp.exp(big_a * dt * float(n)) for n in range(1, SUBLANES + 1)]).reshape(SUBLANES, g * p)
        row = jnp.arange(SUBLANES)[:, None]
        dbl = jnp.stack([jnp.where((row < SUBLANES - s) if d == 1 else (row >= s), pows[s - 1][None, :], 0.0)
                         for s in (1, 2, 4)])
        pw = pows[::-1] if d == 1 else pows
        for lst, val in zip(outs, (jnp.real(bm), jnp.imag(bm), jnp.real(cm), jnp.imag(cm),
                                   jnp.stack([jnp.real(dbl), jnp.imag(dbl)]), jnp.stack([jnp.real(pw), jnp.imag(pw)]))):
            lst.append(val)
    bre, bim, cre, cim, dbl, pw = (jnp.stack(o) for o in outs)
    return bre.astype(BF16), bim.astype(BF16), cre.astype(BF16), cim.astype(BF16), dbl, pw


def _to_row_major(t, nb, seq):
    rows = seq // GRID_W
    lat = t[:nb * seq].reshape(nb, GRID_W, rows, -1).transpose(0, 2, 1, 3).reshape(nb * seq, -1)
    return jnp.concatenate([lat, t[nb * seq:]], axis=0)


def kernel(x, c, ctx, c_ctx, w_ada, b_ada, norm_gain, w_in, lru_conv_w, lru_conv_b, lru_w_a, lru_b_a, lru_w_i,
           lru_b_i, lru_lambda, s5_a_re, s5_a_im, s5_log_dt, s5_b_re, s5_b_im, s5_c_re, s5_c_im, s5_d, s5_w_glu,
           gla_w_alpha, gla_b_alpha, gla_norm, hgrn_lb_logits, hgrn_norm, w_branch, w_out, peer_w_q, peer_keys,
           peer_u, peer_v):
    nb, seq, d = x.shape
    nctx = ctx.shape[1]
    depth = w_ada.shape[0]
    mix = d // 4
    n_lat = nb * seq
    n_tok = n_lat + nb * nctx
    assert nb + 1 <= SUBLANES and seq % GRID_W == 0
    gla_dk, gla_dv = mix // 2 // GLA_HEADS, mix // GLA_HEADS
    hg_dk = hg_dv = mix // HGRN_HEADS
    gla_rank = gla_w_alpha.shape[2]
    tm = _tile(math.gcd(seq, nb * nctx), (512, 256, 128))
    tr = min(tm, 256)
    tw = min(tm, 256)
    tt = _tile(math.gcd(seq, nctx), (256, 128, 64, 32))
    ch = 64

    w_in_t = jnp.swapaxes(w_in, 1, 2)
    o_c = 3 * mix
    o_ca = o_c + 2 * GLA_HEADS * gla_dk + GLA_HEADS * gla_dv + mix
    o_dg = 2 * HGRN_HEADS * hg_dk + HGRN_HEADS * hg_dk + HGRN_HEADS * hg_dv
    o_gt = o_dg + mix
    za_w = 128

    p_lb = jax.nn.softmax(hgrn_lb_logits.astype(F32), axis=0)
    lower = jnp.cumsum(p_lb, axis=0) - p_lb[0]
    cvec = jnp.zeros((SUBLANES, d), F32).at[:nb].set(c).at[nb].set(c_ctx)
    b_ada3 = b_ada[:, None, :]
    u_tab = peer_u.astype(BF16)
    vt_tab = jnp.swapaxes(peer_v, 1, 2).astype(BF16)

    xs = jnp.concatenate([x.reshape(n_lat, d), ctx.reshape(nb * nctx, d)], axis=0)
    for l in range(depth):
        last = l == depth - 1
        m_out = n_lat if last else n_tok
        mod = _adaln(cvec, w_ada, b_ada3, l).reshape(SUBLANES, N_MOD, d)
        h = _normmod(xs, norm_gain, mod, l, seq, nb, tr)

        mm_in = lambda off, n, shift, tn, name: _mmt(h, w_in_t, l, off, n, shift, tn, n_tok, tw, name)
        pa = mm_in(0, 2 * mix, 0, mix, "proj_lru")
        pb = mm_in(2 * mix, mix, 0, mix, "proj_s5")
        pc = mm_in(o_c, o_ca - o_c - mix, 0, mix, "proj_gla")
        pcg = mm_in(o_ca - mix, mix, 0, mix, "proj_gla_gate")
        za = mm_in(o_ca, za_w, 0, za_w, "proj_gla_rank")
        pd = mm_in(o_ca, o_dg, gla_rank, mix, "proj_hgrn")
        pdg = mm_in(o_ca + o_dg, mix, gla_rank, mix, "proj_hgrn_gate")
        pg = mm_in(o_ca + o_gt, N_BRANCH * d, gla_rank, mix, "proj_gate")

        wd, bd, sp = _lru_params(lru_w_a[l], lru_b_a[l], lru_w_i[l], lru_b_i[l], lru_lambda[l])
        hf, hb = _lru(pa, lru_conv_w[l], lru_conv_b[l][None, :], wd, bd, sp, nb, seq, nctx, tt)
        ya = _lru_out(pa, hf, hb, tm)

        s5p = _s5_params(s5_a_re[l], s5_a_im[l], s5_log_dt[l], s5_b_re[l], s5_b_im[l], s5_c_re[l], s5_c_im[l])
        yf, ybk = _s5(pb, *s5p, nb, seq, nctx, tt)
        yb = _to_row_major(_s5_glu(pb, yf, ybk, s5_d[:, None, :], s5_w_glu, l, n_lat, seq, tm), nb, seq)

        wa = jnp.zeros((2, za_w, GLA_HEADS * gla_dk), F32).at[:, :gla_rank].set(gla_w_alpha[l]).astype(BF16)
        of, ob = _gla(pc, za, wa, gla_b_alpha[l][:, None, :], nb, seq, nctx, tt, ch, GLA_HEADS, gla_dk, gla_dv)
        yc = _headnorm(of, ob, pcg, 0, gla_norm[:, None, :], l, GLA_HEADS, tm)

        of, ob = _hgrn(pd, lower[l][None, :], nb, seq, nctx, tt, ch, HGRN_HEADS, hg_dk, hg_dv)
        yd = _to_row_major(_headnorm(of, ob, pdg, 0, hgrn_norm[:, None, :], l, HGRN_HEADS, tm, (n_lat, seq)), nb, seq)

        zm = _merge((ya, yb, yc, yd), pg, w_branch, l, m_out, tm)
        mo = _mm(zm, w_out, l, d, F32, m_out, tw, mix, "proj_out")
        xs, f, ft = _resid(xs, mo, norm_gain, mod, l, m_out, seq, nb, tr, 2, 1, True, False)

        q = _mm(f, peer_w_q, l, peer_w_q.shape[2], F32, m_out, tw, mix, "peer_query")
        e1, cnt, rank2, e2 = _route(q, peer_keys, l, _tile(m_out, (256, 128)))
        yt = _peer(ft, u_tab, vt_tab, l, e1, cnt, rank2, e2, _tile(m_out, (512, 256, 128)), 512)
        xs = _resid(xs, yt, norm_gain, mod, l, m_out, seq, nb, tr, 5, 3, False, True)[0]
    return xs[:n_lat].reshape(nb, seq, d)
```

```python
import functools
import math

import jax
import jax.numpy as jnp
from jax import lax
from jax.experimental import pallas as pl
from jax.experimental.pallas import tpu as pltpu

F32 = jnp.float32
BF16 = jnp.bfloat16
EPS = 1e-6
GRID_W = 64
N_MOD = 6
N_BRANCH = 4
LRU_BLOCKS = 16
LRU_C = 8.0
S5_GROUP = 16
S5_STATE = 64
S5_CHUNKS = 4
GLA_HEADS = 4
GLA_TAU = 16.0
HGRN_HEADS = 8
PEER_HEADS = 8
N_KEYS = 128
PEER_TOPK = 16
NO_RANK = 255.0
SUBLANES = 8
LANES = 128
VMEM_LIMIT = 56 * 1024 * 1024

NT_DIMS = (((1,), (1,)), ((), ()))
TN_DIMS = (((0,), (0,)), ((), ()))


def _params(n_axes, vmem=VMEM_LIMIT):
    return pltpu.CompilerParams(dimension_semantics=("arbitrary",) * n_axes, vmem_limit_bytes=vmem)


def _tile(n, prefs):
    for p in prefs:
        if n % p == 0:
            return p
    raise ValueError(f"no tile for {n} in {prefs}")


def _rms(x):
    return x * lax.rsqrt(jnp.mean(x * x, axis=-1, keepdims=True) + EPS)


def _ada_body(c_ref, w_ref, b_ref, o_ref):
    c = c_ref[...]
    s = (c * jax.nn.sigmoid(c)).astype(BF16)
    o_ref[...] = jnp.dot(s, w_ref[...].astype(BF16), preferred_element_type=F32) + b_ref[...]


def _adaln(cvec, w_ada, b_ada3, l):
    rows, d = cvec.shape
    n = w_ada.shape[2]
    tn = 512
    return pl.pallas_call(
        _ada_body,
        grid=(n // tn,),
        in_specs=[
            pl.BlockSpec((rows, d), lambda j: (0, 0)),
            pl.BlockSpec((None, d, tn), lambda j: (l, 0, j)),
            pl.BlockSpec((None, 1, tn), lambda j: (l, 0, j)),
        ],
        out_specs=pl.BlockSpec((rows, tn), lambda j: (0, j)),
        out_shape=jax.ShapeDtypeStruct((rows, n), F32),
        compiler_params=_params(1),
        name="adaln",
    )(cvec, w_ada, b_ada3)


def _normmod_body(x_ref, g_ref, mod_ref, o_ref, *, gi, shift_i, scale_i):
    y = _rms(x_ref[...]) * g_ref[gi:gi + 1, :]
    o_ref[...] = (y * (1.0 + mod_ref[scale_i:scale_i + 1, :]) + mod_ref[shift_i:shift_i + 1, :]).astype(o_ref.dtype)


def _mod_row(i, tm, seq, nb):
    return jnp.minimum((i * tm) // seq, nb)


def _normmod(x, gain, mod, l, seq, nb, tm):
    t, d = x.shape
    return pl.pallas_call(
        functools.partial(_normmod_body, gi=0, shift_i=0, scale_i=1),
        grid=(t // tm,),
        in_specs=[
            pl.BlockSpec((tm, d), lambda i: (i, 0)),
            pl.BlockSpec((None, 4, d), lambda i: (l, 0, 0)),
            pl.BlockSpec((None, N_MOD, d), lambda i: (_mod_row(i, tm, seq, nb), 0, 0)),
        ],
        out_specs=pl.BlockSpec((tm, d), lambda i: (i, 0)),
        out_shape=jax.ShapeDtypeStruct((t, d), BF16),
        compiler_params=_params(1),
        name="normmod",
    )(x, gain, mod)


def _mm_body(a_ref, w_ref, o_ref, wb_ref):
    @pl.when(pl.program_id(1) == 0)
    def _():
        wb_ref[...] = w_ref[...].astype(BF16)

    o_ref[...] = jnp.dot(a_ref[...], wb_ref[...], preferred_element_type=F32).astype(o_ref.dtype)


def _mm(a, w, l, ncols, out_dtype, m, tm, tn, name):
    k = a.shape[1]
    w_spec = pl.BlockSpec((None, k, tn), lambda j, i: (l, 0, j))
    return pl.pallas_call(
        _mm_body,
        grid=(ncols // tn, m // tm),
        in_specs=[pl.BlockSpec((tm, k), lambda j, i: (i, 0)), w_spec],
        out_specs=pl.BlockSpec((tm, tn), lambda j, i: (i, j)),
        out_shape=jax.ShapeDtypeStruct((m, ncols), out_dtype),
        scratch_shapes=[pltpu.VMEM((k, tn), BF16)],
        compiler_params=_params(2),
        name=name,
    )(a, w)


def _mmt_body(a_ref, w_ref, *rest, shift):
    if shift:
        tail_ref, o_ref, wb_ref = rest
    else:
        o_ref, wb_ref = rest
    tn = wb_ref.shape[0]

    @pl.when(pl.program_id(1) == 0)
    def _():
        if shift:
            wb_ref[0:tn - shift, :] = w_ref[shift:tn, :].astype(BF16)
            wb_ref[tn - shift:tn, :] = tail_ref[...].astype(BF16)
        else:
            wb_ref[...] = w_ref[...].astype(BF16)

    o_ref[...] = lax.dot_general(a_ref[...], wb_ref[...], NT_DIMS, preferred_element_type=F32).astype(o_ref.dtype)


def _mmt(a, wt, l, row_off, ncols, shift, tn, m, tm, name):
    k = a.shape[1]
    base = row_off // tn
    in_specs = [pl.BlockSpec((tm, k), lambda j, i: (i, 0)), pl.BlockSpec((None, tn, k), lambda j, i: (l, base + j, 0))]
    args = [a, wt]
    if shift:
        per = tn // shift
        in_specs.append(pl.BlockSpec((None, shift, k), lambda j, i: (l, (base + j + 1) * per, 0)))
        args.append(wt)
    return pl.pallas_call(
        functools.partial(_mmt_body, shift=shift),
        grid=(ncols // tn, m // tm),
        in_specs=in_specs,
        out_specs=pl.BlockSpec((tm, tn), lambda j, i: (i, j)),
        out_shape=jax.ShapeDtypeStruct((m, ncols), F32),
        scratch_shapes=[pltpu.VMEM((tn, k), BF16)],
        compiler_params=_params(2),
        name=name,
    )(*args)


def _seq_block(b, i, rev, nb, seq, ctx, tt):
    nctx, nlat = ctx // tt, seq // tt
    ic = (nctx - 1 - i) if rev else i
    il = (nlat - 1 - (i - nctx)) if rev else (i - nctx)
    return jnp.where(i < nctx, (nb * seq) // tt + b * nctx + ic, b * nlat + il)


def _seq_spec(cols, col_blk, rev, nb, seq, ctx, tt):
    return pl.BlockSpec((tt, cols), lambda b, i: (_seq_block(b, i, rev, nb, seq, ctx, tt), col_blk))


def _grid_view(arr):
    t, c = arr.shape
    return arr.reshape(t // GRID_W, GRID_W, c)


def _col_tokens(lat_ref, first, k, a, b):
    return jnp.concatenate([lat_ref[:, first + j, a:b] for j in range(k)], axis=0)


def _lat_step(i, rev, seq, ctx, tt):
    nctx, nlat = ctx // tt, seq // tt
    il = (nlat - 1 - (i - nctx)) if rev else (i - nctx)
    return jnp.clip(il, 0, nlat - 1)


def _col_lat_spec(width, col_blk, rev, seq, ctx, tt):
    rows = seq // GRID_W
    per = SUBLANES // (tt // rows)
    return pl.BlockSpec((rows, SUBLANES, width), lambda b, i: (b, _lat_step(i, rev, seq, ctx, tt) // per, col_blk))


def _first_col(i, rev, seq, ctx, tt):
    k = tt // (seq // GRID_W)
    return (_lat_step(i, rev, seq, ctx, tt) % (SUBLANES // k)) * k


def _ctx_spec(cols, col_blk, rev, nb, seq, ctx, tt):
    nctx = ctx // tt

    def index(b, i):
        ic = (nctx - 1 - i) if rev else i
        return (nb * seq) // tt + b * nctx + jnp.clip(ic, 0, nctx - 1), col_blk

    return pl.BlockSpec((tt, cols), index)


def _halo_spec(cols, col_blk, rev, side, nb, seq, ctx, tt, total):
    per = tt // SUBLANES
    last = total // SUBLANES - 1

    def index(b, i):
        blk = _seq_block(b, i, rev, nb, seq, ctx, tt) * per
        blk = blk - 1 if side < 0 else blk + per
        return jnp.clip(blk, 0, last), col_blk

    return pl.BlockSpec((SUBLANES, cols), index)


def _stream_pos(i, rev, seq, ctx, tt):
    nctx, nlat = ctx // tt, seq // tt
    is_ctx = i < nctx
    ii = jnp.where(is_ctx, (nctx - 1 - i) if rev else i, (nlat - 1 - (i - nctx)) if rev else (i - nctx))
    n = jnp.where(is_ctx, nctx, nlat)
    return ii == 0, ii == n - 1


def _scan_tile_real(a, u, carry, rev):
    row = lax.broadcasted_iota(jnp.int32, a.shape, 0)
    for s in (1, 2, 4):
        if rev:
            a_sh, u_sh = pltpu.roll(a, SUBLANES - s, 0), pltpu.roll(u, SUBLANES - s, 0)
            ok = row < SUBLANES - s
        else:
            a_sh, u_sh = pltpu.roll(a, s, 0), pltpu.roll(u, s, 0)
            ok = row >= s
        u = jnp.where(ok, a * u_sh + u, u)
        a = jnp.where(ok, a * a_sh, a)
    h = u + a * carry
    return h, (h[0:1] if rev else h[SUBLANES - 1:SUBLANES])


def _lru_body(xf_ref, pf_ref, nf_ref, xb_ref, pb_ref, nb_ref, cw_ref, cb_ref, wd_ref, bd_ref, sp_ref,
              hf_ref, hb_ref, a_s, u_s, carry_s, *, seq, ctx, tt, c):
    i = pl.program_id(1)

    @pl.when(i == 0)
    def _():
        carry_s[...] = jnp.zeros_like(carry_s)

    row = lax.broadcasted_iota(jnp.int32, (tt, c), 0)
    for d, (x_ref, p_ref, n_ref) in enumerate(((xf_ref, pf_ref, nf_ref), (xb_ref, pb_ref, nb_ref))):
        first, last = _stream_pos(i, d == 1, seq, ctx, tt)
        x = x_ref[...]
        prev = jnp.where(first, 0.0, p_ref[SUBLANES - 1:SUBLANES, :])
        nx1 = jnp.where(last, 0.0, n_ref[0:1, :])
        nx2 = jnp.where(last, 0.0, n_ref[1:2, :])
        xm1 = jnp.where(row == 0, prev, pltpu.roll(x, 1, 0))
        xp1 = jnp.where(row == tt - 1, nx1, pltpu.roll(x, tt - 1, 0))
        xp2 = jnp.where(row == tt - 1, nx2, jnp.where(row == tt - 2, nx1, pltpu.roll(x, tt - 2, 0)))
        xl = xm1 * cw_ref[0:1, :] + x * cw_ref[1:2, :] + xp1 * cw_ref[2:3, :] + xp2 * cw_ref[3:4, :] + cb_ref[...]
        z = jnp.dot(xl.astype(BF16), wd_ref[d], preferred_element_type=F32) + bd_ref[d]
        r = jax.nn.sigmoid(z[:, :c])
        gi = jax.nn.sigmoid(z[:, c:])
        log_a = -LRU_C * r * sp_ref[d]
        a = jnp.exp(log_a)
        a_s[d] = a
        u_s[d] = jnp.sqrt(-jnp.tanh(log_a) * (a * a + 1.0)) * gi * xl

    ntile = tt // SUBLANES

    def step(k, carry):
        cf, cb = carry
        rows_f = pl.ds(pl.multiple_of(k * SUBLANES, SUBLANES), SUBLANES)
        rows_b = pl.ds(pl.multiple_of((ntile - 1 - k) * SUBLANES, SUBLANES), SUBLANES)
        h, cf = _scan_tile_real(a_s[0, rows_f, :], u_s[0, rows_f, :], cf, False)
        hf_ref[rows_f, :] = h
        h, cb = _scan_tile_real(a_s[1, rows_b, :], u_s[1, rows_b, :], cb, True)
        hb_ref[rows_b, :] = h
        return cf, cb

    cf, cb = lax.fori_loop(0, ntile, step, (carry_s[0, 0:1, :], carry_s[1, 0:1, :]))
    carry_s[0, 0:1, :] = cf
    carry_s[1, 0:1, :] = cb


def _lru(pa, cw, cb, wd, bd, sp, nb, seq, ctx, tt):
    t = pa.shape[0]
    c = cw.shape[1]
    geo = (nb, seq, ctx, tt)
    full = lambda shape: pl.BlockSpec(shape, lambda b, i: (0,) * len(shape))
    return pl.pallas_call(
        functools.partial(_lru_body, seq=seq, ctx=ctx, tt=tt, c=c),
        grid=(nb, (seq + ctx) // tt),
        in_specs=[
            _seq_spec(c, 0, False, *geo), _halo_spec(c, 0, False, -1, *geo, t), _halo_spec(c, 0, False, 1, *geo, t),
            _seq_spec(c, 0, True, *geo), _halo_spec(c, 0, True, -1, *geo, t), _halo_spec(c, 0, True, 1, *geo, t),
            full(cw.shape), full(cb.shape), full(wd.shape), full(bd.shape), full(sp.shape),
        ],
        out_specs=[_seq_spec(c, 0, False, *geo), _seq_spec(c, 0, True, *geo)],
        out_shape=[jax.ShapeDtypeStruct((t, c), F32)] * 2,
        scratch_shapes=[pltpu.VMEM((2, tt, c), F32), pltpu.VMEM((2, tt, c), F32), pltpu.VMEM((2, SUBLANES, c), F32)],
        compiler_params=_params(2),
        name="rglru_scan",
    )(pa, pa, pa, pa, pa, pa, cw, cb, wd, bd, sp)


def _lru_out_body(ay_ref, hf_ref, hb_ref, o_ref):
    o_ref[...] = (jax.nn.gelu(ay_ref[...]) * (hf_ref[...] + hb_ref[...])).astype(o_ref.dtype)


def _lru_out(pa, hf, hb, tm):
    t, c = hf.shape
    return pl.pallas_call(
        _lru_out_body,
        grid=(t // tm,),
        in_specs=[pl.BlockSpec((tm, c), lambda i: (i, 1)), pl.BlockSpec((tm, c), lambda i: (i, 0)),
                  pl.BlockSpec((tm, c), lambda i: (i, 0))],
        out_specs=pl.BlockSpec((tm, c), lambda i: (i, 0)),
        out_shape=jax.ShapeDtypeStruct((t, c), BF16),
        compiler_params=_params(1),
        name="rglru_out",
    )(pa, hf, hb)


def _scan_tile_cplx(xr, xi, dbl_ref, pw_ref, d, lanes, cr, ci, rev):
    for n, s in enumerate((1, 2, 4)):
        ar = dbl_ref[d, 0, n, :, lanes]
        ai = dbl_ref[d, 1, n, :, lanes]
        shift = SUBLANES - s if rev else s
        sr, si = pltpu.roll(xr, shift, 0), pltpu.roll(xi, shift, 0)
        xr, xi = xr + ar * sr - ai * si, xi + ar * si + ai * sr
    pr = pw_ref[d, 0, :, lanes]
    pi = pw_ref[d, 1, :, lanes]
    hr = xr + pr * cr - pi * ci
    hi = xi + pr * ci + pi * cr
    sel = slice(0, 1) if rev else slice(SUBLANES - 1, SUBLANES)
    return hr, hi, hr[sel], hi[sel]


def _load_seq_block(dst_ref, d, lat_ref, ctx_ref, is_ctx, first, k):
    @pl.when(is_ctx)
    def _():
        dst_ref[d] = ctx_ref[...]

    @pl.when(jnp.logical_not(is_ctx))
    def _():
        dst_ref[d] = _col_tokens(lat_ref, first, k, 0, lat_ref.shape[2])


def _s5_body(ulf_ref, ucf_ref, ulb_ref, ucb_ref, bre_ref, bim_ref, cre_ref, cim_ref, dbl_ref, pw_ref, yf_ref, yb_ref,
             u_s, hr_s, hi_s, carry_s, *, tt, gw, sw, seq, ctx):
    i = pl.program_id(1)

    @pl.when(i == 0)
    def _():
        carry_s[...] = jnp.zeros_like(carry_s)

    for d, (lat_ref, ctx_ref) in enumerate(((ulf_ref, ucf_ref), (ulb_ref, ucb_ref))):
        _load_seq_block(u_s, d, lat_ref, ctx_ref, i < ctx // tt, _first_col(i, d == 1, seq, ctx, tt),
                        tt // (seq // GRID_W))

    ntile = tt // SUBLANES
    for ch in range(S5_CHUNKS):
        cols = slice(ch * gw, (ch + 1) * gw)
        lanes = slice(ch * sw, (ch + 1) * sw)
        for d in range(2):
            u = u_s[d, :, cols].astype(BF16)
            hr_s[d] = jnp.dot(u, bre_ref[d, ch], preferred_element_type=F32)
            hi_s[d] = jnp.dot(u, bim_ref[d, ch], preferred_element_type=F32)

        def step(k, carry):
            crf, cif, crb, cib = carry
            rows_f = pl.ds(pl.multiple_of(k * SUBLANES, SUBLANES), SUBLANES)
            rows_b = pl.ds(pl.multiple_of((ntile - 1 - k) * SUBLANES, SUBLANES), SUBLANES)
            hr, hi, crf, cif = _scan_tile_cplx(hr_s[0, rows_f, :], hi_s[0, rows_f, :], dbl_ref, pw_ref, 0, lanes,
                                               crf, cif, False)
            hr_s[0, rows_f, :] = hr
            hi_s[0, rows_f, :] = hi
            hr, hi, crb, cib = _scan_tile_cplx(hr_s[1, rows_b, :], hi_s[1, rows_b, :], dbl_ref, pw_ref, 1, lanes,
                                               crb, cib, True)
            hr_s[1, rows_b, :] = hr
            hi_s[1, rows_b, :] = hi
            return crf, cif, crb, cib

        init = tuple(carry_s[n, 0:1, lanes] for n in range(4))
        fin = lax.fori_loop(0, ntile, step, init)
        for n in range(4):
            carry_s[n, 0:1, lanes] = fin[n]
        for d, y_ref in enumerate((yf_ref, yb_ref)):
            y_ref[:, cols] = (jnp.dot(hr_s[d].astype(BF16), cre_ref[d, ch], preferred_element_type=F32)
                              - jnp.dot(hi_s[d].astype(BF16), cim_ref[d, ch], preferred_element_type=F32))


def _s5(u, bre, bim, cre, cim, dbl, pw, nb, seq, ctx, tt):
    t, c = u.shape
    gw = c // S5_CHUNKS
    sw = bre.shape[-1]
    geo = (nb, seq, ctx, tt)
    full = lambda shape: pl.BlockSpec(shape, lambda b, i: (0,) * len(shape))
    lat = lambda rev: _col_lat_spec(c, 0, rev, seq, ctx, tt)
    cx = lambda rev: _ctx_spec(c, 0, rev, *geo)
    uv = _grid_view(u)
    return pl.pallas_call(
        functools.partial(_s5_body, tt=tt, gw=gw, sw=sw, seq=seq, ctx=ctx),
        grid=(nb, (seq + ctx) // tt),
        in_specs=[lat(False), cx(False), lat(True), cx(True),
                  full(bre.shape), full(bim.shape), full(cre.shape), full(cim.shape), full(dbl.shape), full(pw.shape)],
        out_specs=[_seq_spec(c, 0, False, *geo), _seq_spec(c, 0, True, *geo)],
        out_shape=[jax.ShapeDtypeStruct((t, c), F32)] * 2,
        scratch_shapes=[pltpu.VMEM((2, tt, c), F32), pltpu.VMEM((2, tt, sw), F32), pltpu.VMEM((2, tt, sw), F32),
                        pltpu.VMEM((4, SUBLANES, sw * S5_CHUNKS), F32)],
        compiler_params=_params(2),
        name="s5_scan",
    )(uv, u, uv, u, bre, bim, cre, cim, dbl, pw)


def _tile_specs(cols, col_blk, n_lat, seq, tm):
    rows = seq // GRID_W
    nlat, per = n_lat // tm, seq // tm

    def lat_index(i):
        il = jnp.minimum(i, nlat - 1)
        return il // per, il % per

    def lat_index3(i):
        b, blk = lat_index(i)
        return b, blk, col_blk

    return (pl.BlockSpec((rows, tm // rows, cols), lat_index3),
            pl.BlockSpec((tm, cols), lambda i: (jnp.maximum(i, nlat), col_blk)))


def _load_tile(dst_ref, lat_ref, ctx_ref, is_ctx):
    @pl.when(is_ctx)
    def _():
        dst_ref[...] = ctx_ref[...]

    @pl.when(jnp.logical_not(is_ctx))
    def _():
        dst_ref[...] = _col_tokens(lat_ref, 0, lat_ref.shape[1], 0, lat_ref.shape[2])


def _s5_glu_body(ul_ref, uc_ref, yf_ref, yb_ref, d_ref, w_ref, o_ref, wb_ref, u_s, *, c, nlat):
    @pl.when(pl.program_id(0) == 0)
    def _():
        wb_ref[...] = w_ref[...].astype(BF16)

    _load_tile(u_s, ul_ref, uc_ref, pl.program_id(0) >= nlat)
    y = jax.nn.gelu(u_s[...] * d_ref[...] + yf_ref[...] + yb_ref[...])
    z = jnp.dot(y.astype(BF16), wb_ref[...], preferred_element_type=F32)
    o_ref[...] = (z[:, :c] * jax.nn.sigmoid(z[:, c:])).astype(o_ref.dtype)


def _s5_glu(u, yf, yb, dskip, w_glu, l, n_lat, seq, tm):
    t, c = u.shape
    lat_spec, ctx_spec = _tile_specs(c, 0, n_lat, seq, tm)
    return pl.pallas_call(
        functools.partial(_s5_glu_body, c=c, nlat=n_lat // tm),
        grid=(t // tm,),
        in_specs=[lat_spec, ctx_spec] + [pl.BlockSpec((tm, c), lambda i: (i, 0))] * 2
        + [pl.BlockSpec((None, 1, c), lambda i: (l, 0, 0)), pl.BlockSpec((None, c, 2 * c), lambda i: (l, 0, 0))],
        out_specs=pl.BlockSpec((tm, c), lambda i: (i, 0)),
        out_shape=jax.ShapeDtypeStruct((t, c), BF16),
        scratch_shapes=[pltpu.VMEM((c, 2 * c), BF16), pltpu.VMEM((tm, c), F32)],
        compiler_params=_params(1),
        name="s5_glu",
    )(_grid_view(u), u, yf, yb, dskip, w_glu)


def _chunk_cumsum(x, rev, ch):
    row = lax.broadcasted_iota(jnp.int32, x.shape, 0)
    s = 1
    while s < ch:
        if rev:
            x = x + jnp.where(row < ch - s, pltpu.roll(x, ch - s, 0), 0.0)
        else:
            x = x + jnp.where(row >= s, pltpu.roll(x, s, 0), 0.0)
        s *= 2
    return x


def _chunk_scan(q_s, k_s, v_s, lf_s, st_s, o_refs, *, tt, ch, heads, dk, dv):
    nchunk = tt // ch
    tri_r = lax.broadcasted_iota(jnp.int32, (ch, ch), 0)
    tri_c = lax.broadcasted_iota(jnp.int32, (ch, ch), 1)

    def step(n, _):
        for d in range(2):
            rev = d == 1
            cc = (nchunk - 1 - n) if rev else n
            rows = pl.ds(pl.multiple_of(cc * ch, ch), ch)
            b = _chunk_cumsum(lf_s[d, rows, :], rev, ch)
            piv = ch // 2 if rev else ch // 2 - 1
            end = 0 if rev else ch - 1
            m = b[piv:piv + 1, :]
            bl = b[end:end + 1, :]
            qm = q_s[d, rows, :] * jnp.exp(b - m)
            km = k_s[d, rows, :] * jnp.exp(m - b)
            qg = (qm * jnp.exp(m)).astype(BF16)
            kg = (km * jnp.exp(bl - m)).astype(BF16)
            dec = jnp.exp(bl)
            qm = qm.astype(BF16)
            km = km.astype(BF16)
            v = v_s[d, rows, :].astype(BF16)
            keep = (tri_r <= tri_c) if rev else (tri_r >= tri_c)
            for h in range(heads):
                ks = slice(h * dk, (h + 1) * dk)
                vs = slice(h * dv, (h + 1) * dv)
                sc = lax.dot_general(qm[:, ks], km[:, ks], NT_DIMS, preferred_element_type=F32)
                sc = jnp.where(keep, sc, 0.0).astype(BF16)
                st = st_s[d, h]
                o = jnp.dot(sc, v[:, vs], preferred_element_type=F32)
                o = o + lax.dot_general(qg[:, ks], st.astype(BF16), NT_DIMS, preferred_element_type=F32)
                st_s[d, h] = st * dec[:, ks] + lax.dot_general(v[:, vs], kg[:, ks], TN_DIMS,
                                                               preferred_element_type=F32)
                o_refs[d][rows, vs] = o
        return 0

    lax.fori_loop(0, nchunk, step, 0)


def _gla_body(qf_ref, zf_ref, qb_ref, zb_ref, wa_ref, ba_ref, of_ref, ob_ref, q_s, k_s, v_s, lf_s, st_s,
              *, tt, ch, heads, dk, dv):
    @pl.when(pl.program_id(1) == 0)
    def _():
        st_s[...] = jnp.zeros_like(st_s)

    hk = heads * dk
    for d, (x_ref, z_ref) in enumerate(((qf_ref, zf_ref), (qb_ref, zb_ref))):
        q_s[d] = x_ref[:, :hk] * dk ** -0.5
        k_s[d] = x_ref[:, hk:2 * hk]
        v_s[d] = x_ref[:, 2 * hk:]
        la = jnp.dot(z_ref[...].astype(BF16), wa_ref[d], preferred_element_type=F32) + ba_ref[d]
        lf_s[d] = jax.nn.log_sigmoid(la) / GLA_TAU
    _chunk_scan(q_s, k_s, v_s, lf_s, st_s, (of_ref, ob_ref), tt=tt, ch=ch, heads=heads, dk=dk, dv=dv)


def _hgrn_body(*refs, tt, ch, heads, dk, dv, seq, ctx):
    lat_refs, ctx_refs = (refs[0:3], refs[6:9]), (refs[3:6], refs[9:12])
    lb_ref, of_ref, ob_ref, q_s, k_s, v_s, lf_s, st_s = refs[12:]
    i = pl.program_id(1)

    @pl.when(i == 0)
    def _():
        st_s[...] = jnp.zeros_like(st_s)

    lb = lb_ref[...]

    def prep(d, z, q, v):
        q_s[d] = jax.nn.silu(q) * dk ** -0.5
        v_s[d] = v
        k_s[d] = (1.0 - lb) * jax.nn.sigmoid(-z)
        lf_s[d] = jnp.log(lb + (1.0 - lb) * jax.nn.sigmoid(z))

    k_cols = tt // (seq // GRID_W)
    for d in range(2):
        @pl.when(i < ctx // tt)
        def _():
            prep(d, *(r[...] for r in ctx_refs[d]))

        @pl.when(i >= ctx // tt)
        def _():
            first = _first_col(i, d == 1, seq, ctx, tt)
            prep(d, *(_col_tokens(r, first, k_cols, 0, r.shape[2]) for r in lat_refs[d]))

    _chunk_scan(q_s, k_s, v_s, lf_s, st_s, (of_ref, ob_ref), tt=tt, ch=ch, heads=heads, dk=dk, dv=dv)


def _chunk_scratch(tt, heads, dk, dv):
    return [pltpu.VMEM((2, tt, heads * dk), F32), pltpu.VMEM((2, tt, heads * dk), F32),
            pltpu.VMEM((2, tt, heads * dv), F32), pltpu.VMEM((2, tt, heads * dk), F32),
            pltpu.VMEM((2, heads, dv, dk), F32)]


def _gla(pc, za, wa, ba, nb, seq, ctx, tt, ch, heads, dk, dv):
    t = pc.shape[0]
    geo = (nb, seq, ctx, tt)
    wide = 2 * heads * dk + heads * dv
    full = lambda shape: pl.BlockSpec(shape, lambda b, i: (0,) * len(shape))
    return pl.pallas_call(
        functools.partial(_gla_body, tt=tt, ch=ch, heads=heads, dk=dk, dv=dv),
        grid=(nb, (seq + ctx) // tt),
        in_specs=[_seq_spec(wide, 0, False, *geo), _seq_spec(za.shape[1], 0, False, *geo),
                  _seq_spec(wide, 0, True, *geo), _seq_spec(za.shape[1], 0, True, *geo),
                  full(wa.shape), full(ba.shape)],
        out_specs=[_seq_spec(heads * dv, 0, False, *geo), _seq_spec(heads * dv, 0, True, *geo)],
        out_shape=[jax.ShapeDtypeStruct((t, heads * dv), F32)] * 2,
        scratch_shapes=_chunk_scratch(tt, heads, dk, dv),
        compiler_params=_params(2),
        name="gla_scan",
    )(pc, za, pc, za, wa, ba)


def _hgrn(pd, lb, nb, seq, ctx, tt, ch, heads, dk, dv):
    t = pd.shape[0]
    geo = (nb, seq, ctx, tt)
    hk = heads * dk
    lat = lambda d: [_col_lat_spec(hk, blk, d == 1, seq, ctx, tt) for blk in (d, 2, 3)]
    cx = lambda d: [_ctx_spec(hk, blk, d == 1, *geo) for blk in (d, 2, 3)]
    pv = _grid_view(pd)
    return pl.pallas_call(
        functools.partial(_hgrn_body, tt=tt, ch=ch, heads=heads, dk=dk, dv=dv, seq=seq, ctx=ctx),
        grid=(nb, (seq + ctx) // tt),
        in_specs=lat(0) + cx(0) + lat(1) + cx(1) + [pl.BlockSpec(lb.shape, lambda b, i: (0, 0))],
        out_specs=[_seq_spec(heads * dv, 0, False, *geo), _seq_spec(heads * dv, 0, True, *geo)],
        out_shape=[jax.ShapeDtypeStruct((t, heads * dv), F32)] * 2,
        scratch_shapes=_chunk_scratch(tt, heads, dk, dv),
        compiler_params=_params(2),
        name="hgrn_scan",
    )(*([pv] * 3 + [pd] * 3) * 2, lb)


def _headnorm_body(of_ref, ob_ref, *rest, heads, dv, nlat):
    if nlat is None:
        g_ref, n_ref, o_ref = rest
        gate = g_ref[...]
    else:
        gl_ref, gc_ref, n_ref, o_ref, g_s = rest
        _load_tile(g_s, gl_ref, gc_ref, pl.program_id(0) >= nlat)
        gate = g_s[...]
    o = of_ref[...] + ob_ref[...]
    parts = [_rms(o[:, h * dv:(h + 1) * dv]) for h in range(heads)]
    y = jnp.concatenate(parts, axis=-1) * n_ref[...]
    o_ref[...] = (y * jax.nn.silu(gate)).astype(o_ref.dtype)


def _headnorm(of, ob, gate_arr, gate_blk, gain, l, heads, tm, col_major=None):
    t, c = of.shape
    row = pl.BlockSpec((tm, c), lambda i: (i, 0))
    if col_major is None:
        gate_specs, gate_args, scratch, nlat = [pl.BlockSpec((tm, c), lambda i: (i, gate_blk))], [gate_arr], [], None
    else:
        n_lat, seq = col_major
        gate_specs = list(_tile_specs(c, gate_blk, n_lat, seq, tm))
        gate_args, scratch, nlat = [_grid_view(gate_arr), gate_arr], [pltpu.VMEM((tm, c), F32)], n_lat // tm
    return pl.pallas_call(
        functools.partial(_headnorm_body, heads=heads, dv=c // heads, nlat=nlat),
        grid=(t // tm,),
        in_specs=[row, row] + gate_specs + [pl.BlockSpec((None, 1, c), lambda i: (l, 0, 0))],
        out_specs=row,
        out_shape=jax.ShapeDtypeStruct((t, c), BF16),
        scratch_shapes=scratch,
        compiler_params=_params(1),
        name="headnorm_gate",
    )(of, ob, *gate_args, gain)


def _merge_body(ya_ref, yb_ref, yc_ref, yd_ref, g0_ref, g1_ref, g2_ref, g3_ref, w_ref, o_ref, wb_ref):
    @pl.when(pl.program_id(1) == 0)
    def _():
        wb_ref[...] = w_ref[...].astype(BF16)

    acc = None
    for k, (y_ref, g_ref) in enumerate(((ya_ref, g0_ref), (yb_ref, g1_ref), (yc_ref, g2_ref), (yd_ref, g3_ref))):
        term = jax.nn.sigmoid(g_ref[...]) * jnp.dot(y_ref[...], wb_ref[k], preferred_element_type=F32)
        acc = term if acc is None else acc + term
    o_ref[...] = acc.astype(o_ref.dtype)


def _merge(ys, pg, w_branch, l, m, tm):
    c = ys[0].shape[1]
    d = w_branch.shape[3]
    tn = 512
    nj = d // tn
    y_spec = pl.BlockSpec((tm, c), lambda j, i: (i, 0))
    g_specs = [pl.BlockSpec((tm, tn), functools.partial(lambda j, i, k: (i, k * nj + j), k=k)) for k in range(N_BRANCH)]
    return pl.pallas_call(
        _merge_body,
        grid=(nj, m // tm),
        in_specs=[y_spec] * N_BRANCH + g_specs + [pl.BlockSpec((None, N_BRANCH, c, tn), lambda j, i: (l, 0, 0, j))],
        out_specs=pl.BlockSpec((tm, tn), lambda j, i: (i, j)),
        out_shape=jax.ShapeDtypeStruct((m, d), BF16),
        scratch_shapes=[pltpu.VMEM((N_BRANCH, c, tn), BF16)],
        compiler_params=_params(2),
        name="branch_merge",
    )(*ys, pg, pg, pg, pg, w_branch)


def _resid_body(x_ref, y_ref, g_ref, mod_ref, xo_ref, *f_refs, gate_i, gy, gf, shift_i, scale_i, y_transposed):
    y = y_ref[...].T if y_transposed else y_ref[...]
    xn = x_ref[...] + mod_ref[gate_i:gate_i + 1, :] * (_rms(y) * g_ref[gy:gy + 1, :])
    xo_ref[...] = xn
    if f_refs:
        f = _rms(xn) * g_ref[gf:gf + 1, :]
        f = f * (1.0 + mod_ref[scale_i:scale_i + 1, :]) + mod_ref[shift_i:shift_i + 1, :]
        f_refs[0][...] = f.astype(BF16)
        f_refs[1][...] = f.T.astype(BF16)


def _resid(x, y, gain, mod, l, m, seq, nb, tm, gate_i, gy, with_f, y_transposed):
    d = x.shape[1]
    row = pl.BlockSpec((tm, d), lambda i: (i, 0))
    col = pl.BlockSpec((d, tm), lambda i: (0, i))
    out_shape = [jax.ShapeDtypeStruct((m, d), F32)]
    out_specs = [row]
    if with_f:
        out_shape += [jax.ShapeDtypeStruct((m, d), BF16), jax.ShapeDtypeStruct((d, m), BF16)]
        out_specs += [row, col]
    return pl.pallas_call(
        functools.partial(_resid_body, gate_i=gate_i, gy=gy, gf=2, shift_i=3, scale_i=4, y_transposed=y_transposed),
        grid=(m // tm,),
        in_specs=[row, col if y_transposed else row, pl.BlockSpec((None, 4, d), lambda i: (l, 0, 0)),
                  pl.BlockSpec((None, N_MOD, d), lambda i: (_mod_row(i, tm, seq, nb), 0, 0))],
        out_specs=out_specs,
        out_shape=out_shape,
        compiler_params=_params(1),
        name="residual",
    )(x, y, gain, mod)


def _top_rows(x, vals_ref, idx_ref, want_rank):
    n_rows = x.shape[0]
    iota = lax.broadcasted_iota(jnp.int32, x.shape, 0)

    def step(r, carry):
        x = carry[0]
        mx = jnp.max(x, axis=0, keepdims=True)
        vals_ref[pl.ds(r, 1), :] = mx
        first = jnp.min(jnp.where(x == mx, iota, n_rows), axis=0, keepdims=True)
        idx_ref[pl.ds(r, 1), :] = first
        hit = iota == first
        x = jnp.where(hit, -jnp.inf, x)
        return (x, jnp.where(hit, jnp.asarray(r, F32), carry[1])) if want_rank else (x,)

    init = (x, jnp.full(x.shape, NO_RANK, F32)) if want_rank else (x,)
    return lax.fori_loop(0, PEER_TOPK, step, init)[-1]


def _top_rows_distinct(x, vals_ref, want_rank):
    def step(r, carry):
        x = carry[0]
        mx = jnp.max(x, axis=0, keepdims=True)
        vals_ref[pl.ds(r, 1), :] = mx
        hit = x == mx
        x = jnp.where(hit, -jnp.inf, x)
        return (x, jnp.where(hit, jnp.asarray(r, F32), carry[1])) if want_rank else (x,)

    init = (x, jnp.full(x.shape, NO_RANK, F32)) if want_rank else (x,)
    out = lax.fori_loop(0, PEER_TOPK, step, init)
    removed = jnp.sum((out[0] == -jnp.inf).astype(F32), axis=0, keepdims=True)
    return removed != float(PEER_TOPK), out[-1]


def _pair_rows(a, b):
    half = PEER_TOPK // 2
    return jnp.concatenate([a[0:1, :] + b] + [a[r:r + 1, :] + b[0:half, :] for r in range(1, half)]
                           + [a[half:, :] + b[0:1, :]], axis=0)


def _route_body(q_ref, keys_ref, e1_ref, cnt_ref, rank2_ref, e2_ref, va_s, vb_s, vc_s, ia_s, ib_s, *, dq):
    half = PEER_TOPK // 2
    iota = lax.broadcasted_iota(jnp.int32, (N_KEYS, q_ref.shape[0]), 0)
    for h in range(PEER_HEADS):
        scores = []
        for p in range(2):
            qh = q_ref[:, h * 2 * dq + p * dq:h * 2 * dq + (p + 1) * dq].astype(BF16)
            scores.append(lax.dot_general(keys_ref[h, p].astype(BF16), qh, NT_DIMS, preferred_element_type=F32))

        def emit(a, b, best, cnt, rank2):
            z = jnp.sum(jnp.exp(best - best[0:1, :]), axis=0, keepdims=True)
            e1_ref[h] = jnp.exp(scores[0] - a[0:1, :]) / z
            cnt_ref[h] = cnt
            rank2_ref[h] = rank2.astype(BF16)
            e2_ref[h] = jnp.exp(scores[1] - b[0:1, :]).astype(BF16)

        tied_a, _ = _top_rows_distinct(scores[0], va_s, False)
        tied_b, rank2 = _top_rows_distinct(scores[1], vb_s, True)
        a, b = va_s[...], vb_s[...]
        cand = _pair_rows(a, b)
        tied_c, _ = _top_rows_distinct(cand, vc_s, False)
        best = vc_s[...]
        keep = (cand >= best[PEER_TOPK - 1:PEER_TOPK, :]).astype(F32)
        groups = [(0, PEER_TOPK)] + [(PEER_TOPK + half * (r - 1), PEER_TOPK + half * r) for r in range(1, half)] \
            + [(PEER_TOPK + half * (half - 1) + r, PEER_TOPK + half * (half - 1) + r + 1) for r in range(half)]
        cnt = jnp.zeros(scores[0].shape, F32)
        for i, (lo, hi) in enumerate(groups):
            n_i = jnp.sum(keep[lo:hi, :], axis=0, keepdims=True)
            cnt = jnp.where(scores[0] == a[i:i + 1, :], n_i, cnt)
        emit(a, b, best, cnt, rank2)
        tied = jnp.max((tied_a | tied_b | tied_c).astype(F32)) > 0.0

        @pl.when(tied)
        def _():
            _top_rows(scores[0], va_s, ia_s, False)
            rank2 = _top_rows(scores[1], vb_s, ib_s, True)
            a, b = va_s[...], vb_s[...]
            _top_rows(_pair_rows(a, b), vc_s, ib_s, False)
            row = ib_s[...]
            sel_i = jnp.where(row < PEER_TOPK, 0,
                              jnp.where(row < PEER_TOPK + half * (half - 1), (row - half) // half, row - half * half))
            key_a = ia_s[...]
            cnt = jnp.zeros(scores[0].shape, F32)
            for i in range(PEER_TOPK):
                n_i = jnp.sum((sel_i == i).astype(F32), axis=0, keepdims=True)
                cnt = jnp.where(iota == key_a[i:i + 1, :], n_i, cnt)
            emit(a, b, vc_s[...], cnt, rank2)


def _route(q, keys, l, tm):
    t = q.shape[0]
    dq = keys.shape[-1]
    spec = pl.BlockSpec((PEER_HEADS, N_KEYS, tm), lambda i: (0, 0, i))
    return pl.pallas_call(
        functools.partial(_route_body, dq=dq),
        grid=(t // tm,),
        in_specs=[pl.BlockSpec((tm, q.shape[1]), lambda i: (i, 0)),
                  pl.BlockSpec((None,) + keys.shape[1:], lambda i: (l, 0, 0, 0, 0))],
        out_specs=[spec] * 4,
        out_shape=[jax.ShapeDtypeStruct((PEER_HEADS, N_KEYS, t), dt) for dt in (F32, F32, BF16, BF16)],
        scratch_shapes=[pltpu.VMEM((PEER_TOPK, tm), F32)] * 3 + [pltpu.VMEM((PEER_TOPK, tm), jnp.int32)] * 2,
        compiler_params=_params(1),
        name="peer_route",
    )(q, keys)


def _peer_body(ft_ref, u_ref, vt_ref, e1_ref, cnt_ref, rank2_ref, e2_ref, o_ref, g_s, act_s, *, te, nj):
    j = pl.program_id(1)

    @pl.when(j == 0)
    def _():
        o_ref[...] = jnp.zeros_like(o_ref)
        act_s[...] = jnp.zeros_like(act_s)

    tile = jnp.maximum(j - 1, 0)
    tm = act_s.shape[1]
    zero = jnp.zeros((), BF16)
    for r in range(te // N_KEYS):
        i1 = tile * (te // N_KEYS) + r
        rows = slice(r * N_KEYS, (r + 1) * N_KEYS)
        e1_rows = [e1_ref[h, pl.ds(i1, 1), :].astype(BF16) for h in range(PEER_HEADS)]
        cnt_rows = [cnt_ref[h, pl.ds(i1, 1), :].astype(BF16) for h in range(PEER_HEADS)]
        for cb in range(tm // LANES):
            cs = slice(cb * LANES, (cb + 1) * LANES)
            w = None
            for h in range(PEER_HEADS):
                term = jnp.where(rank2_ref[h, :, cs] < cnt_rows[h][:, cs], e2_ref[h, :, cs] * e1_rows[h][:, cs], zero)
                w = term if w is None else w + term
            g_s[rows, cs] = w * act_s[rows, cs].astype(BF16)

    o_ref[...] += jnp.dot(vt_ref[...], g_s[...], preferred_element_type=F32)
    act_s[...] = jax.nn.gelu(jnp.dot(u_ref[...], ft_ref[...], preferred_element_type=F32))


def _peer(ft, u, vt, l, e1, cnt, rank2, e2, tm, te):
    d, t = ft.shape
    nj = u.shape[1] // te
    once = pl.Buffered(1)
    big_spec = pl.BlockSpec((PEER_HEADS, N_KEYS, tm), lambda i, j: (0, 0, i), pipeline_mode=once)
    return pl.pallas_call(
        functools.partial(_peer_body, te=te, nj=nj),
        grid=(t // tm, nj + 1),
        in_specs=[pl.BlockSpec((d, tm), lambda i, j: (0, i), pipeline_mode=once),
                  pl.BlockSpec((None, te, d), lambda i, j: (l, jnp.minimum(j, nj - 1), 0)),
                  pl.BlockSpec((None, d, te), lambda i, j: (l, 0, jnp.maximum(j - 1, 0))),
                  big_spec, big_spec, big_spec, big_spec],
        out_specs=pl.BlockSpec((d, tm), lambda i, j: (0, i)),
        out_shape=jax.ShapeDtypeStruct((d, t), F32),
        scratch_shapes=[pltpu.VMEM((te, tm), BF16), pltpu.VMEM((te, tm), F32)],
        compiler_params=_params(2),
        name="peer_experts",
    )(ft, u, vt, e1, cnt, rank2, e2)


def _block_diag(w):
    n, r, c = w.shape
    eye = jnp.eye(n, dtype=w.dtype)
    return (w[:, :, None, :] * eye[:, None, :, None]).reshape(n * r, n * c)


def _lru_params(w_a, b_a, w_i, b_i, lam):
    wd = jnp.stack([jnp.concatenate([_block_diag(w_a[d]), _block_diag(w_i[d])], axis=1) for d in range(2)])
    bd = jnp.stack([jnp.concatenate([b_a[d], b_i[d]])[None, :] for d in range(2)])
    sp = jax.nn.softplus(-lam)[:, None, :]
    return wd.astype(BF16), bd, sp


def _s5_params(a_re, a_im, log_dt, b_re, b_im, c_re, c_im):
    g, p = a_re.shape[1:]
    gc = g // S5_CHUNKS
    outs = [[] for _ in range(6)]
    for d in range(2):
        big_a = lax.complex(a_re[d], a_im[d])
        dt = jnp.exp(log_dt[d])[:, None]
        a_bar = jnp.exp(big_a * dt)
        b_bar = ((a_bar - 1.0) / big_a)[..., None] * lax.complex(b_re[d], b_im[d])
        c_mat = lax.complex(c_re[d], c_im[d])
        bm = jnp.swapaxes(b_bar, 1, 2).reshape(S5_CHUNKS, gc, S5_GROUP, p)
        cm = jnp.swapaxes(c_mat, 1, 2).reshape(S5_CHUNKS, gc, p, S5_GROUP)
        bm = jnp.stack([_block_diag(bm[k]) for k in range(S5_CHUNKS)])
        cm = jnp.stack([_block_diag(cm[k]) for k in range(S5_CHUNKS)])
        pows = jnp.stack([jnp.exp(big_a * dt * float(n)) for n in range(1, SUBLANES + 1)]).reshape(SUBLANES, g * p)
        row = jnp.arange(SUBLANES)[:, None]
        dbl = jnp.stack([jnp.where((row < SUBLANES - s) if d == 1 else (row >= s), pows[s - 1][None, :], 0.0)
                         for s in (1, 2, 4)])
        pw = pows[::-1] if d == 1 else pows
        for lst, val in zip(outs, (jnp.real(bm), jnp.imag(bm), jnp.real(cm), jnp.imag(cm),
                                   jnp.stack([jnp.real(dbl), jnp.imag(dbl)]), jnp.stack([jnp.real(pw), jnp.imag(pw)]))):
            lst.append(val)
    bre, bim, cre, cim, dbl, pw = (jnp.stack(o) for o in outs)
    return bre.astype(BF16), bim.astype(BF16), cre.astype(BF16), cim.astype(BF16), dbl, pw


def _to_row_major(t, nb, seq):
    rows = seq // GRID_W
    lat = t[:nb * seq].reshape(nb, GRID_W, rows, -1).transpose(0, 2, 1, 3).reshape(nb * seq, -1)
    return jnp.concatenate([lat, t[nb * seq:]], axis=0)


def kernel(x, c, ctx, c_ctx, w_ada, b_ada, norm_gain, w_in, lru_conv_w, lru_conv_b, lru_w_a, lru_b_a, lru_w_i,
           lru_b_i, lru_lambda, s5_a_re, s5_a_im, s5_log_dt, s5_b_re, s5_b_im, s5_c_re, s5_c_im, s5_d, s5_w_glu,
           gla_w_alpha, gla_b_alpha, gla_norm, hgrn_lb_logits, hgrn_norm, w_branch, w_out, peer_w_q, peer_keys,
           peer_u, peer_v):
    nb, seq, d = x.shape
    nctx = ctx.shape[1]
    depth = w_ada.shape[0]
    mix = d // 4
    n_lat = nb * seq
    n_tok = n_lat + nb * nctx
    assert nb + 1 <= SUBLANES and seq % GRID_W == 0
    gla_dk, gla_dv = mix // 2 // GLA_HEADS, mix // GLA_HEADS
    hg_dk = hg_dv = mix // HGRN_HEADS
    gla_rank = gla_w_alpha.shape[2]
    tm = _tile(math.gcd(seq, nb * nctx), (512, 256, 128))
    tr = min(tm, 256)
    tw = min(tm, 256)
    tt = _tile(math.gcd(seq, nctx), (256, 128, 64, 32))
    ch = 64

    w_in_t = jnp.swapaxes(w_in, 1, 2)
    o_c = 3 * mix
    o_ca = o_c + 2 * GLA_HEADS * gla_dk + GLA_HEADS * gla_dv + mix
    o_dg = 2 * HGRN_HEADS * hg_dk + HGRN_HEADS * hg_dk + HGRN_HEADS * hg_dv
    o_gt = o_dg + mix
    za_w = 128

    p_lb = jax.nn.softmax(hgrn_lb_logits.astype(F32), axis=0)
    lower = jnp.cumsum(p_lb, axis=0) - p_lb[0]
    cvec = jnp.zeros((SUBLANES, d), F32).at[:nb].set(c).at[nb].set(c_ctx)
    b_ada3 = b_ada[:, None, :]
    u_tab = peer_u.astype(BF16)
    vt_tab = jnp.swapaxes(peer_v, 1, 2).astype(BF16)

    xs = jnp.concatenate([x.reshape(n_lat, d), ctx.reshape(nb * nctx, d)], axis=0)
    for l in range(depth):
        last = l == depth - 1
        m_out = n_lat if last else n_tok
        mod = _adaln(cvec, w_ada, b_ada3, l).reshape(SUBLANES, N_MOD, d)
        h = _normmod(xs, norm_gain, mod, l, seq, nb, tr)

        mm_in = lambda off, n, shift, tn, name: _mmt(h, w_in_t, l, off, n, shift, tn, n_tok, tw, name)
        pa = mm_in(0, 2 * mix, 0, mix, "proj_lru")
        pb = mm_in(2 * mix, mix, 0, mix, "proj_s5")
        pc = mm_in(o_c, o_ca - o_c - mix, 0, mix, "proj_gla")
        pcg = mm_in(o_ca - mix, mix, 0, mix, "proj_gla_gate")
        za = mm_in(o_ca, za_w, 0, za_w, "proj_gla_rank")
        pd = mm_in(o_ca, o_dg, gla_rank, mix, "proj_hgrn")
        pdg = mm_in(o_ca + o_dg, mix, gla_rank, mix, "proj_hgrn_gate")
        pg = mm_in(o_ca + o_gt, N_BRANCH * d, gla_rank, mix, "proj_gate")

        wd, bd, sp = _lru_params(lru_w_a[l], lru_b_a[l], lru_w_i[l], lru_b_i[l], lru_lambda[l])
        hf, hb = _lru(pa, lru_conv_w[l], lru_conv_b[l][None, :], wd, bd, sp, nb, seq, nctx, tt)
        ya = _lru_out(pa, hf, hb, tm)

        s5p = _s5_params(s5_a_re[l], s5_a_im[l], s5_log_dt[l], s5_b_re[l], s5_b_im[l], s5_c_re[l], s5_c_im[l])
        yf, ybk = _s5(pb, *s5p, nb, seq, nctx, tt)
        yb = _to_row_major(_s5_glu(pb, yf, ybk, s5_d[:, None, :], s5_w_glu, l, n_lat, seq, tm), nb, seq)

        wa = jnp.zeros((2, za_w, GLA_HEADS * gla_dk), F32).at[:, :gla_rank].set(gla_w_alpha[l]).astype(BF16)
        of, ob = _gla(pc, za, wa, gla_b_alpha[l][:, None, :], nb, seq, nctx, tt, ch, GLA_HEADS, gla_dk, gla_dv)
        yc = _headnorm(of, ob, pcg, 0, gla_norm[:, None, :], l, GLA_HEADS, tm)

        of, ob = _hgrn(pd, lower[l][None, :], nb, seq, nctx, tt, ch, HGRN_HEADS, hg_dk, hg_dv)
        yd = _to_row_major(_headnorm(of, ob, pdg, 0, hgrn_norm[:, None, :], l, HGRN_HEADS, tm, (n_lat, seq)), nb, seq)

        zm = _merge((ya, yb, yc, yd), pg, w_branch, l, m_out, tm)
        mo = _mm(zm, w_out, l, d, F32, m_out, tw, mix, "proj_out")
        xs, f, ft = _resid(xs, mo, norm_gain, mod, l, m_out, seq, nb, tr, 2, 1, True, False)

        q = _mm(f, peer_w_q, l, peer_w_q.shape[2], F32, m_out, tw, mix, "peer_query")
        e1, cnt, rank2, e2 = _route(q, peer_keys, l, _tile(m_out, (256, 128)))
        yt = _peer(ft, u_tab, vt_tab, l, e1, cnt, rank2, e2, _tile(m_out, (512, 256, 128)), 512)
        xs = _resid(xs, yt, norm_gain, mod, l, m_out, seq, nb, tr, 5, 3, False, True)[0]
    return xs[:n_lat].reshape(nb, seq, d)
```

```python
import functools
import math

import jax
import jax.numpy as jnp
from jax import lax
from jax.experimental import pallas as pl
from jax.experimental.pallas import tpu as pltpu

F32 = jnp.float32
BF16 = jnp.bfloat16
EPS = 1e-6
GRID_W = 64
N_MOD = 6
N_BRANCH = 4
LRU_BLOCKS = 16
LRU_C = 8.0
S5_GROUP = 16
S5_STATE = 64
S5_CHUNKS = 4
GLA_HEADS = 4
GLA_TAU = 16.0
HGRN_HEADS = 8
PEER_HEADS = 8
N_KEYS = 128
PEER_TOPK = 16
NO_RANK = 255.0
SUBLANES = 8
LANES = 128
VMEM_LIMIT = 56 * 1024 * 1024

NT_DIMS = (((1,), (1,)), ((), ()))
TN_DIMS = (((0,), (0,)), ((), ()))


def _params(n_axes, vmem=VMEM_LIMIT):
    return pltpu.CompilerParams(dimension_semantics=("arbitrary",) * n_axes, vmem_limit_bytes=vmem)


def _tile(n, prefs):
    for p in prefs:
        if n % p == 0:
            return p
    raise ValueError(f"no tile for {n} in {prefs}")


def _rms(x):
    return x * lax.rsqrt(jnp.mean(x * x, axis=-1, keepdims=True) + EPS)


def _ada_body(c_ref, w_ref, b_ref, o_ref):
    c = c_ref[...]
    s = (c * jax.nn.sigmoid(c)).astype(BF16)
    o_ref[...] = jnp.dot(s, w_ref[...].astype(BF16), preferred_element_type=F32) + b_ref[...]


def _adaln(cvec, w_ada, b_ada3, l):
    rows, d = cvec.shape
    n = w_ada.shape[2]
    tn = 512
    return pl.pallas_call(
        _ada_body,
        grid=(n // tn,),
        in_specs=[
            pl.BlockSpec((rows, d), lambda j: (0, 0)),
            pl.BlockSpec((None, d, tn), lambda j: (l, 0, j)),
            pl.BlockSpec((None, 1, tn), lambda j: (l, 0, j)),
        ],
        out_specs=pl.BlockSpec((rows, tn), lambda j: (0, j)),
        out_shape=jax.ShapeDtypeStruct((rows, n), F32),
        compiler_params=_params(1),
        name="adaln",
    )(cvec, w_ada, b_ada3)


def _normmod_body(x_ref, g_ref, mod_ref, o_ref, *, gi, shift_i, scale_i):
    y = _rms(x_ref[...]) * g_ref[gi:gi + 1, :]
    o_ref[...] = (y * (1.0 + mod_ref[scale_i:scale_i + 1, :]) + mod_ref[shift_i:shift_i + 1, :]).astype(o_ref.dtype)


def _mod_row(i, tm, seq, nb):
    return jnp.minimum((i * tm) // seq, nb)


def _normmod(x, gain, mod, l, seq, nb, tm):
    t, d = x.shape
    return pl.pallas_call(
        functools.partial(_normmod_body, gi=0, shift_i=0, scale_i=1),
        grid=(t // tm,),
        in_specs=[
            pl.BlockSpec((tm, d), lambda i: (i, 0)),
            pl.BlockSpec((None, 4, d), lambda i: (l, 0, 0)),
            pl.BlockSpec((None, N_MOD, d), lambda i: (_mod_row(i, tm, seq, nb), 0, 0)),
        ],
        out_specs=pl.BlockSpec((tm, d), lambda i: (i, 0)),
        out_shape=jax.ShapeDtypeStruct((t, d), BF16),
        compiler_params=_params(1),
        name="normmod",
    )(x, gain, mod)


def _mm_body(a_ref, w_ref, o_ref, wb_ref):
    @pl.when(pl.program_id(1) == 0)
    def _():
        wb_ref[...] = w_ref[...].astype(BF16)

    o_ref[...] = jnp.dot(a_ref[...], wb_ref[...], preferred_element_type=F32).astype(o_ref.dtype)


def _mm(a, w, l, ncols, out_dtype, m, tm, tn, name):
    k = a.shape[1]
    w_spec = pl.BlockSpec((None, k, tn), lambda j, i: (l, 0, j))
    return pl.pallas_call(
        _mm_body,
        grid=(ncols // tn, m // tm),
        in_specs=[pl.BlockSpec((tm, k), lambda j, i: (i, 0)), w_spec],
        out_specs=pl.BlockSpec((tm, tn), lambda j, i: (i, j)),
        out_shape=jax.ShapeDtypeStruct((m, ncols), out_dtype),
        scratch_shapes=[pltpu.VMEM((k, tn), BF16)],
        compiler_params=_params(2),
        name=name,
    )(a, w)


def _mmt_body(a_ref, w_ref, *rest, shift):
    if shift:
        tail_ref, o_ref, wb_ref = rest
    else:
        o_ref, wb_ref = rest
    tn = wb_ref.shape[0]

    @pl.when(pl.program_id(1) == 0)
    def _():
        if shift:
            wb_ref[0:tn - shift, :] = w_ref[shift:tn, :].astype(BF16)
            wb_ref[tn - shift:tn, :] = tail_ref[...].astype(BF16)
        else:
            wb_ref[...] = w_ref[...].astype(BF16)

    o_ref[...] = lax.dot_general(a_ref[...], wb_ref[...], NT_DIMS, preferred_element_type=F32).astype(o_ref.dtype)


def _mmt(a, wt, l, row_off, ncols, shift, tn, m, tm, name):
    k = a.shape[1]
    base = row_off // tn
    in_specs = [pl.BlockSpec((tm, k), lambda j, i: (i, 0)), pl.BlockSpec((None, tn, k), lambda j, i: (l, base + j, 0))]
    args = [a, wt]
    if shift:
        per = tn // shift
        in_specs.append(pl.BlockSpec((None, shift, k), lambda j, i: (l, (base + j + 1) * per, 0)))
        args.append(wt)
    return pl.pallas_call(
        functools.partial(_mmt_body, shift=shift),
        grid=(ncols // tn, m // tm),
        in_specs=in_specs,
        out_specs=pl.BlockSpec((tm, tn), lambda j, i: (i, j)),
        out_shape=jax.ShapeDtypeStruct((m, ncols), F32),
        scratch_shapes=[pltpu.VMEM((tn, k), BF16)],
        compiler_params=_params(2),
        name=name,
    )(*args)


def _seq_block(b, i, rev, nb, seq, ctx, tt):
    nctx, nlat = ctx // tt, seq // tt
    ic = (nctx - 1 - i) if rev else i
    il = (nlat - 1 - (i - nctx)) if rev else (i - nctx)
    return jnp.where(i < nctx, (nb * seq) // tt + b * nctx + ic, b * nlat + il)


def _seq_spec(cols, col_blk, rev, nb, seq, ctx, tt):
    return pl.BlockSpec((tt, cols), lambda b, i: (_seq_block(b, i, rev, nb, seq, ctx, tt), col_blk))


def _grid_view(arr):
    t, c = arr.shape
    return arr.reshape(t // GRID_W, GRID_W, c)


def _col_tokens(lat_ref, first, k, a, b):
    return jnp.concatenate([lat_ref[:, first + j, a:b] for j in range(k)], axis=0)


def _lat_step(i, rev, seq, ctx, tt):
    nctx, nlat = ctx // tt, seq // tt
    il = (nlat - 1 - (i - nctx)) if rev else (i - nctx)
    return jnp.clip(il, 0, nlat - 1)


def _col_lat_spec(width, col_blk, rev, seq, ctx, tt):
    rows = seq // GRID_W
    per = SUBLANES // (tt // rows)
    return pl.BlockSpec((rows, SUBLANES, width), lambda b, i: (b, _lat_step(i, rev, seq, ctx, tt) // per, col_blk))


def _first_col(i, rev, seq, ctx, tt):
    k = tt // (seq // GRID_W)
    return (_lat_step(i, rev, seq, ctx, tt) % (SUBLANES // k)) * k


def _ctx_spec(cols, col_blk, rev, nb, seq, ctx, tt):
    nctx = ctx // tt

    def index(b, i):
        ic = (nctx - 1 - i) if rev else i
        return (nb * seq) // tt + b * nctx + jnp.clip(ic, 0, nctx - 1), col_blk

    return pl.BlockSpec((tt, cols), index)


def _halo_spec(cols, col_blk, rev, side, nb, seq, ctx, tt, total):
    per = tt // SUBLANES
    last = total // SUBLANES - 1

    def index(b, i):
        blk = _seq_block(b, i, rev, nb, seq, ctx, tt) * per
        blk = blk - 1 if side < 0 else blk + per
        return jnp.clip(blk, 0, last), col_blk

    return pl.BlockSpec((SUBLANES, cols), index)


def _stream_pos(i, rev, seq, ctx, tt):
    nctx, nlat = ctx // tt, seq // tt
    is_ctx = i < nctx
    ii = jnp.where(is_ctx, (nctx - 1 - i) if rev else i, (nlat - 1 - (i - nctx)) if rev else (i - nctx))
    n = jnp.where(is_ctx, nctx, nlat)
    return ii == 0, ii == n - 1


def _scan_tile_real(a, u, carry, rev):
    row = lax.broadcasted_iota(jnp.int32, a.shape, 0)
    for s in (1, 2, 4):
        if rev:
            a_sh, u_sh = pltpu.roll(a, SUBLANES - s, 0), pltpu.roll(u, SUBLANES - s, 0)
            ok = row < SUBLANES - s
        else:
            a_sh, u_sh = pltpu.roll(a, s, 0), pltpu.roll(u, s, 0)
            ok = row >= s
        u = jnp.where(ok, a * u_sh + u, u)
        a = jnp.where(ok, a * a_sh, a)
    h = u + a * carry
    return h, (h[0:1] if rev else h[SUBLANES - 1:SUBLANES])


def _lru_body(xf_ref, pf_ref, nf_ref, xb_ref, pb_ref, nb_ref, cw_ref, cb_ref, wd_ref, bd_ref, sp_ref,
              hf_ref, hb_ref, a_s, u_s, carry_s, *, seq, ctx, tt, c):
    i = pl.program_id(1)

    @pl.when(i == 0)
    def _():
        carry_s[...] = jnp.zeros_like(carry_s)

    row = lax.broadcasted_iota(jnp.int32, (tt, c), 0)
    for d, (x_ref, p_ref, n_ref) in enumerate(((xf_ref, pf_ref, nf_ref), (xb_ref, pb_ref, nb_ref))):
        first, last = _stream_pos(i, d == 1, seq, ctx, tt)
        x = x_ref[...]
        prev = jnp.where(first, 0.0, p_ref[SUBLANES - 1:SUBLANES, :])
        nx1 = jnp.where(last, 0.0, n_ref[0:1, :])
        nx2 = jnp.where(last, 0.0, n_ref[1:2, :])
        xm1 = jnp.where(row == 0, prev, pltpu.roll(x, 1, 0))
        xp1 = jnp.where(row == tt - 1, nx1, pltpu.roll(x, tt - 1, 0))
        xp2 = jnp.where(row == tt - 1, nx2, jnp.where(row == tt - 2, nx1, pltpu.roll(x, tt - 2, 0)))
        xl = xm1 * cw_ref[0:1, :] + x * cw_ref[1:2, :] + xp1 * cw_ref[2:3, :] + xp2 * cw_ref[3:4, :] + cb_ref[...]
        z = jnp.dot(xl.astype(BF16), wd_ref[d], preferred_element_type=F32) + bd_ref[d]
        r = jax.nn.sigmoid(z[:, :c])
        gi = jax.nn.sigmoid(z[:, c:])
        log_a = -LRU_C * r * sp_ref[d]
        a = jnp.exp(log_a)
        a_s[d] = a
        u_s[d] = jnp.sqrt(-jnp.tanh(log_a) * (a * a + 1.0)) * gi * xl

    ntile = tt // SUBLANES

    def step(k, carry):
        cf, cb = carry
        rows_f = pl.ds(pl.multiple_of(k * SUBLANES, SUBLANES), SUBLANES)
        rows_b = pl.ds(pl.multiple_of((ntile - 1 - k) * SUBLANES, SUBLANES), SUBLANES)
        h, cf = _scan_tile_real(a_s[0, rows_f, :], u_s[0, rows_f, :], cf, False)
        hf_ref[rows_f, :] = h
        h, cb = _scan_tile_real(a_s[1, rows_b, :], u_s[1, rows_b, :], cb, True)
        hb_ref[rows_b, :] = h
        return cf, cb

    cf, cb = lax.fori_loop(0, ntile, step, (carry_s[0, 0:1, :], carry_s[1, 0:1, :]))
    carry_s[0, 0:1, :] = cf
    carry_s[1, 0:1, :] = cb


def _layer_spec(arr, l):
    return pl.BlockSpec((None,) + arr.shape[1:], lambda b, i: (l,) + (0,) * (arr.ndim - 1))


def _lru(pa, cw, cb, wd, bd, sp, l, nb, seq, ctx, tt):
    t = pa.shape[0]
    c = cw.shape[-1]
    geo = (nb, seq, ctx, tt)
    return pl.pallas_call(
        functools.partial(_lru_body, seq=seq, ctx=ctx, tt=tt, c=c),
        grid=(nb, (seq + ctx) // tt),
        in_specs=[
            _seq_spec(c, 0, False, *geo), _halo_spec(c, 0, False, -1, *geo, t), _halo_spec(c, 0, False, 1, *geo, t),
            _seq_spec(c, 0, True, *geo), _halo_spec(c, 0, True, -1, *geo, t), _halo_spec(c, 0, True, 1, *geo, t),
            *(_layer_spec(arr, l) for arr in (cw, cb, wd, bd, sp)),
        ],
        out_specs=[_seq_spec(c, 0, False, *geo), _seq_spec(c, 0, True, *geo)],
        out_shape=[jax.ShapeDtypeStruct((t, c), F32)] * 2,
        scratch_shapes=[pltpu.VMEM((2, tt, c), F32), pltpu.VMEM((2, tt, c), F32), pltpu.VMEM((2, SUBLANES, c), F32)],
        compiler_params=_params(2),
        name="rglru_scan",
    )(pa, pa, pa, pa, pa, pa, cw, cb, wd, bd, sp)


def _lru_out_body(ay_ref, hf_ref, hb_ref, o_ref):
    o_ref[...] = (jax.nn.gelu(ay_ref[...]) * (hf_ref[...] + hb_ref[...])).astype(o_ref.dtype)


def _lru_out(pa, hf, hb, tm):
    t, c = hf.shape
    return pl.pallas_call(
        _lru_out_body,
        grid=(t // tm,),
        in_specs=[pl.BlockSpec((tm, c), lambda i: (i, 1)), pl.BlockSpec((tm, c), lambda i: (i, 0)),
                  pl.BlockSpec((tm, c), lambda i: (i, 0))],
        out_specs=pl.BlockSpec((tm, c), lambda i: (i, 0)),
        out_shape=jax.ShapeDtypeStruct((t, c), BF16),
        compiler_params=_params(1),
        name="rglru_out",
    )(pa, hf, hb)


def _scan_tile_cplx(xr, xi, dbl_ref, pw_ref, d, lanes, cr, ci, rev):
    for n, s in enumerate((1, 2, 4)):
        ar = dbl_ref[d, 0, n, :, lanes]
        ai = dbl_ref[d, 1, n, :, lanes]
        shift = SUBLANES - s if rev else s
        sr, si = pltpu.roll(xr, shift, 0), pltpu.roll(xi, shift, 0)
        xr, xi = xr + ar * sr - ai * si, xi + ar * si + ai * sr
    pr = pw_ref[d, 0, :, lanes]
    pi = pw_ref[d, 1, :, lanes]
    hr = xr + pr * cr - pi * ci
    hi = xi + pr * ci + pi * cr
    sel = slice(0, 1) if rev else slice(SUBLANES - 1, SUBLANES)
    return hr, hi, hr[sel], hi[sel]


def _load_seq_block(dst_ref, d, lat_ref, ctx_ref, is_ctx, first, k):
    @pl.when(is_ctx)
    def _():
        dst_ref[d] = ctx_ref[...]

    @pl.when(jnp.logical_not(is_ctx))
    def _():
        dst_ref[d] = _col_tokens(lat_ref, first, k, 0, lat_ref.shape[2])


def _s5_body(ulf_ref, ucf_ref, ulb_ref, ucb_ref, bre_ref, bim_ref, cre_ref, cim_ref, dbl_ref, pw_ref, yf_ref, yb_ref,
             u_s, hr_s, hi_s, carry_s, *, tt, gw, sw, seq, ctx):
    i = pl.program_id(1)

    @pl.when(i == 0)
    def _():
        carry_s[...] = jnp.zeros_like(carry_s)

    for d, (lat_ref, ctx_ref) in enumerate(((ulf_ref, ucf_ref), (ulb_ref, ucb_ref))):
        _load_seq_block(u_s, d, lat_ref, ctx_ref, i < ctx // tt, _first_col(i, d == 1, seq, ctx, tt),
                        tt // (seq // GRID_W))

    ntile = tt // SUBLANES
    for ch in range(S5_CHUNKS):
        cols = slice(ch * gw, (ch + 1) * gw)
        lanes = slice(ch * sw, (ch + 1) * sw)
        for d in range(2):
            u = u_s[d, :, cols].astype(BF16)
            hr_s[d] = jnp.dot(u, bre_ref[d, ch], preferred_element_type=F32)
            hi_s[d] = jnp.dot(u, bim_ref[d, ch], preferred_element_type=F32)

        def step(k, carry):
            crf, cif, crb, cib = carry
            rows_f = pl.ds(pl.multiple_of(k * SUBLANES, SUBLANES), SUBLANES)
            rows_b = pl.ds(pl.multiple_of((ntile - 1 - k) * SUBLANES, SUBLANES), SUBLANES)
            hr, hi, crf, cif = _scan_tile_cplx(hr_s[0, rows_f, :], hi_s[0, rows_f, :], dbl_ref, pw_ref, 0, lanes,
                                               crf, cif, False)
            hr_s[0, rows_f, :] = hr
            hi_s[0, rows_f, :] = hi
            hr, hi, crb, cib = _scan_tile_cplx(hr_s[1, rows_b, :], hi_s[1, rows_b, :], dbl_ref, pw_ref, 1, lanes,
                                               crb, cib, True)
            hr_s[1, rows_b, :] = hr
            hi_s[1, rows_b, :] = hi
            return crf, cif, crb, cib

        init = tuple(carry_s[n, 0:1, lanes] for n in range(4))
        fin = lax.fori_loop(0, ntile, step, init)
        for n in range(4):
            carry_s[n, 0:1, lanes] = fin[n]
        for d, y_ref in enumerate((yf_ref, yb_ref)):
            y_ref[:, cols] = (jnp.dot(hr_s[d].astype(BF16), cre_ref[d, ch], preferred_element_type=F32)
                              - jnp.dot(hi_s[d].astype(BF16), cim_ref[d, ch], preferred_element_type=F32))


def _s5(u, bre, bim, cre, cim, dbl, pw, l, nb, seq, ctx, tt):
    t, c = u.shape
    gw = c // S5_CHUNKS
    sw = bre.shape[-1]
    geo = (nb, seq, ctx, tt)
    lat = lambda rev: _col_lat_spec(c, 0, rev, seq, ctx, tt)
    cx = lambda rev: _ctx_spec(c, 0, rev, *geo)
    uv = _grid_view(u)
    return pl.pallas_call(
        functools.partial(_s5_body, tt=tt, gw=gw, sw=sw, seq=seq, ctx=ctx),
        grid=(nb, (seq + ctx) // tt),
        in_specs=[lat(False), cx(False), lat(True), cx(True),
                  *(_layer_spec(arr, l) for arr in (bre, bim, cre, cim, dbl, pw))],
        out_specs=[_seq_spec(c, 0, False, *geo), _seq_spec(c, 0, True, *geo)],
        out_shape=[jax.ShapeDtypeStruct((t, c), F32)] * 2,
        scratch_shapes=[pltpu.VMEM((2, tt, c), F32), pltpu.VMEM((2, tt, sw), F32), pltpu.VMEM((2, tt, sw), F32),
                        pltpu.VMEM((4, SUBLANES, sw * S5_CHUNKS), F32)],
        compiler_params=_params(2),
        name="s5_scan",
    )(uv, u, uv, u, bre, bim, cre, cim, dbl, pw)


def _tile_specs(cols, col_blk, n_lat, seq, tm):
    rows = seq // GRID_W
    nlat, per = n_lat // tm, seq // tm

    def lat_index(i):
        il = jnp.minimum(i, nlat - 1)
        return il // per, il % per

    def lat_index3(i):
        b, blk = lat_index(i)
        return b, blk, col_blk

    return (pl.BlockSpec((rows, tm // rows, cols), lat_index3),
            pl.BlockSpec((tm, cols), lambda i: (jnp.maximum(i, nlat), col_blk)))


def _load_tile(dst_ref, lat_ref, ctx_ref, is_ctx):
    @pl.when(is_ctx)
    def _():
        dst_ref[...] = ctx_ref[...]

    @pl.when(jnp.logical_not(is_ctx))
    def _():
        dst_ref[...] = _col_tokens(lat_ref, 0, lat_ref.shape[1], 0, lat_ref.shape[2])


def _s5_glu_body(ul_ref, uc_ref, yf_ref, yb_ref, d_ref, w_ref, o_ref, wb_ref, u_s, *, c, nlat):
    @pl.when(pl.program_id(0) == 0)
    def _():
        wb_ref[...] = w_ref[...].astype(BF16)

    _load_tile(u_s, ul_ref, uc_ref, pl.program_id(0) >= nlat)
    y = jax.nn.gelu(u_s[...] * d_ref[...] + yf_ref[...] + yb_ref[...])
    z = jnp.dot(y.astype(BF16), wb_ref[...], preferred_element_type=F32)
    o_ref[...] = (z[:, :c] * jax.nn.sigmoid(z[:, c:])).astype(o_ref.dtype)


def _s5_glu(u, yf, yb, dskip, w_glu, l, n_lat, seq, tm):
    t, c = u.shape
    lat_spec, ctx_spec = _tile_specs(c, 0, n_lat, seq, tm)
    return pl.pallas_call(
        functools.partial(_s5_glu_body, c=c, nlat=n_lat // tm),
        grid=(t // tm,),
        in_specs=[lat_spec, ctx_spec] + [pl.BlockSpec((tm, c), lambda i: (i, 0))] * 2
        + [pl.BlockSpec((None, 1, c), lambda i: (l, 0, 0)), pl.BlockSpec((None, c, 2 * c), lambda i: (l, 0, 0))],
        out_specs=pl.BlockSpec((tm, c), lambda i: (i, 0)),
        out_shape=jax.ShapeDtypeStruct((t, c), BF16),
        scratch_shapes=[pltpu.VMEM((c, 2 * c), BF16), pltpu.VMEM((tm, c), F32)],
        compiler_params=_params(1),
        name="s5_glu",
    )(_grid_view(u), u, yf, yb, dskip, w_glu)


def _chunk_cumsum(x, rev, ch):
    row = lax.broadcasted_iota(jnp.int32, x.shape, 0)
    s = 1
    while s < ch:
        if rev:
            x = x + jnp.where(row < ch - s, pltpu.roll(x, ch - s, 0), 0.0)
        else:
            x = x + jnp.where(row >= s, pltpu.roll(x, s, 0), 0.0)
        s *= 2
    return x


def _chunk_scan(q_s, k_s, v_s, lf_s, st_s, o_refs, *, tt, ch, heads, dk, dv):
    nchunk = tt // ch
    tri_r = lax.broadcasted_iota(jnp.int32, (ch, ch), 0)
    tri_c = lax.broadcasted_iota(jnp.int32, (ch, ch), 1)

    def step(n, _):
        for d in range(2):
            rev = d == 1
            cc = (nchunk - 1 - n) if rev else n
            rows = pl.ds(pl.multiple_of(cc * ch, ch), ch)
            b = _chunk_cumsum(lf_s[d, rows, :], rev, ch)
            piv = ch // 2 if rev else ch // 2 - 1
            end = 0 if rev else ch - 1
            m = b[piv:piv + 1, :]
            bl = b[end:end + 1, :]
            qm = q_s[d, rows, :] * jnp.exp(b - m)
            km = k_s[d, rows, :] * jnp.exp(m - b)
            qg = (qm * jnp.exp(m)).astype(BF16)
            kg = (km * jnp.exp(bl - m)).astype(BF16)
            dec = jnp.exp(bl)
            qm = qm.astype(BF16)
            km = km.astype(BF16)
            v = v_s[d, rows, :].astype(BF16)
            keep = (tri_r <= tri_c) if rev else (tri_r >= tri_c)
            for h in range(heads):
                ks = slice(h * dk, (h + 1) * dk)
                vs = slice(h * dv, (h + 1) * dv)
                sc = lax.dot_general(qm[:, ks], km[:, ks], NT_DIMS, preferred_element_type=F32)
                sc = jnp.where(keep, sc, 0.0).astype(BF16)
                st = st_s[d, h]
                o = jnp.dot(sc, v[:, vs], preferred_element_type=F32)
                o = o + lax.dot_general(qg[:, ks], st.astype(BF16), NT_DIMS, preferred_element_type=F32)
                st_s[d, h] = st * dec[:, ks] + lax.dot_general(v[:, vs], kg[:, ks], TN_DIMS,
                                                               preferred_element_type=F32)
                o_refs[d][rows, vs] = o
        return 0

    lax.fori_loop(0, nchunk, step, 0)


def _gla_body(qf_ref, zf_ref, qb_ref, zb_ref, wa_ref, ba_ref, of_ref, ob_ref, q_s, k_s, v_s, lf_s, st_s,
              *, tt, ch, heads, dk, dv):
    @pl.when(pl.program_id(1) == 0)
    def _():
        st_s[...] = jnp.zeros_like(st_s)

    hk = heads * dk
    for d, (x_ref, z_ref) in enumerate(((qf_ref, zf_ref), (qb_ref, zb_ref))):
        q_s[d] = x_ref[:, :hk] * dk ** -0.5
        k_s[d] = x_ref[:, hk:2 * hk]
        v_s[d] = x_ref[:, 2 * hk:]
        la = jnp.dot(z_ref[...].astype(BF16), wa_ref[d], preferred_element_type=F32) + ba_ref[d]
        lf_s[d] = jax.nn.log_sigmoid(la) / GLA_TAU
    _chunk_scan(q_s, k_s, v_s, lf_s, st_s, (of_ref, ob_ref), tt=tt, ch=ch, heads=heads, dk=dk, dv=dv)


def _hgrn_body(*refs, tt, ch, heads, dk, dv, seq, ctx):
    lat_refs, ctx_refs = (refs[0:3], refs[6:9]), (refs[3:6], refs[9:12])
    lb_ref, of_ref, ob_ref, q_s, k_s, v_s, lf_s, st_s = refs[12:]
    i = pl.program_id(1)

    @pl.when(i == 0)
    def _():
        st_s[...] = jnp.zeros_like(st_s)

    lb = lb_ref[...]

    def prep(d, z, q, v):
        q_s[d] = jax.nn.silu(q) * dk ** -0.5
        v_s[d] = v
        k_s[d] = (1.0 - lb) * jax.nn.sigmoid(-z)
        lf_s[d] = jnp.log(lb + (1.0 - lb) * jax.nn.sigmoid(z))

    k_cols = tt // (seq // GRID_W)
    for d in range(2):
        @pl.when(i < ctx // tt)
        def _():
            prep(d, *(r[...] for r in ctx_refs[d]))

        @pl.when(i >= ctx // tt)
        def _():
            first = _first_col(i, d == 1, seq, ctx, tt)
            prep(d, *(_col_tokens(r, first, k_cols, 0, r.shape[2]) for r in lat_refs[d]))

    _chunk_scan(q_s, k_s, v_s, lf_s, st_s, (of_ref, ob_ref), tt=tt, ch=ch, heads=heads, dk=dk, dv=dv)


def _chunk_scratch(tt, heads, dk, dv):
    return [pltpu.VMEM((2, tt, heads * dk), F32), pltpu.VMEM((2, tt, heads * dk), F32),
            pltpu.VMEM((2, tt, heads * dv), F32), pltpu.VMEM((2, tt, heads * dk), F32),
            pltpu.VMEM((2, heads, dv, dk), F32)]


def _gla(pc, za, wa, ba, nb, seq, ctx, tt, ch, heads, dk, dv):
    t = pc.shape[0]
    geo = (nb, seq, ctx, tt)
    wide = 2 * heads * dk + heads * dv
    full = lambda shape: pl.BlockSpec(shape, lambda b, i: (0,) * len(shape))
    return pl.pallas_call(
        functools.partial(_gla_body, tt=tt, ch=ch, heads=heads, dk=dk, dv=dv),
        grid=(nb, (seq + ctx) // tt),
        in_specs=[_seq_spec(wide, 0, False, *geo), _seq_spec(za.shape[1], 0, False, *geo),
                  _seq_spec(wide, 0, True, *geo), _seq_spec(za.shape[1], 0, True, *geo),
                  full(wa.shape), full(ba.shape)],
        out_specs=[_seq_spec(heads * dv, 0, False, *geo), _seq_spec(heads * dv, 0, True, *geo)],
        out_shape=[jax.ShapeDtypeStruct((t, heads * dv), F32)] * 2,
        scratch_shapes=_chunk_scratch(tt, heads, dk, dv),
        compiler_params=_params(2),
        name="gla_scan",
    )(pc, za, pc, za, wa, ba)


def _hgrn(pd, lb, nb, seq, ctx, tt, ch, heads, dk, dv):
    t = pd.shape[0]
    geo = (nb, seq, ctx, tt)
    hk = heads * dk
    lat = lambda d: [_col_lat_spec(hk, blk, d == 1, seq, ctx, tt) for blk in (d, 2, 3)]
    cx = lambda d: [_ctx_spec(hk, blk, d == 1, *geo) for blk in (d, 2, 3)]
    pv = _grid_view(pd)
    return pl.pallas_call(
        functools.partial(_hgrn_body, tt=tt, ch=ch, heads=heads, dk=dk, dv=dv, seq=seq, ctx=ctx),
        grid=(nb, (seq + ctx) // tt),
        in_specs=lat(0) + cx(0) + lat(1) + cx(1) + [pl.BlockSpec(lb.shape, lambda b, i: (0, 0))],
        out_specs=[_seq_spec(heads * dv, 0, False, *geo), _seq_spec(heads * dv, 0, True, *geo)],
        out_shape=[jax.ShapeDtypeStruct((t, heads * dv), F32)] * 2,
        scratch_shapes=_chunk_scratch(tt, heads, dk, dv),
        compiler_params=_params(2),
        name="hgrn_scan",
    )(*([pv] * 3 + [pd] * 3) * 2, lb)


def _headnorm_body(of_ref, ob_ref, *rest, heads, dv, nlat):
    if nlat is None:
        g_ref, n_ref, o_ref = rest
        gate = g_ref[...]
    else:
        gl_ref, gc_ref, n_ref, o_ref, g_s = rest
        _load_tile(g_s, gl_ref, gc_ref, pl.program_id(0) >= nlat)
        gate = g_s[...]
    o = of_ref[...] + ob_ref[...]
    parts = [_rms(o[:, h * dv:(h + 1) * dv]) for h in range(heads)]
    y = jnp.concatenate(parts, axis=-1) * n_ref[...]
    o_ref[...] = (y * jax.nn.silu(gate)).astype(o_ref.dtype)


def _headnorm(of, ob, gate_arr, gate_blk, gain, l, heads, tm, col_major=None):
    t, c = of.shape
    row = pl.BlockSpec((tm, c), lambda i: (i, 0))
    if col_major is None:
        gate_specs, gate_args, scratch, nlat = [pl.BlockSpec((tm, c), lambda i: (i, gate_blk))], [gate_arr], [], None
    else:
        n_lat, seq = col_major
        gate_specs = list(_tile_specs(c, gate_blk, n_lat, seq, tm))
        gate_args, scratch, nlat = [_grid_view(gate_arr), gate_arr], [pltpu.VMEM((tm, c), F32)], n_lat // tm
    return pl.pallas_call(
        functools.partial(_headnorm_body, heads=heads, dv=c // heads, nlat=nlat),
        grid=(t // tm,),
        in_specs=[row, row] + gate_specs + [pl.BlockSpec((None, 1, c), lambda i: (l, 0, 0))],
        out_specs=row,
        out_shape=jax.ShapeDtypeStruct((t, c), BF16),
        scratch_shapes=scratch,
        compiler_params=_params(1),
        name="headnorm_gate",
    )(of, ob, *gate_args, gain)


def _merge_body(ya_ref, yb_ref, yc_ref, yd_ref, g0_ref, g1_ref, g2_ref, g3_ref, w_ref, o_ref, wb_ref):
    @pl.when(pl.program_id(1) == 0)
    def _():
        wb_ref[...] = w_ref[...].astype(BF16)

    acc = None
    for k, (y_ref, g_ref) in enumerate(((ya_ref, g0_ref), (yb_ref, g1_ref), (yc_ref, g2_ref), (yd_ref, g3_ref))):
        term = jax.nn.sigmoid(g_ref[...]) * jnp.dot(y_ref[...], wb_ref[k], preferred_element_type=F32)
        acc = term if acc is None else acc + term
    o_ref[...] = acc.astype(o_ref.dtype)


def _merge(ys, pg, w_branch, l, m, tm):
    c = ys[0].shape[1]
    d = w_branch.shape[3]
    tn = 512
    nj = d // tn
    y_spec = pl.BlockSpec((tm, c), lambda j, i: (i, 0))
    g_specs = [pl.BlockSpec((tm, tn), functools.partial(lambda j, i, k: (i, k * nj + j), k=k)) for k in range(N_BRANCH)]
    return pl.pallas_call(
        _merge_body,
        grid=(nj, m // tm),
        in_specs=[y_spec] * N_BRANCH + g_specs + [pl.BlockSpec((None, N_BRANCH, c, tn), lambda j, i: (l, 0, 0, j))],
        out_specs=pl.BlockSpec((tm, tn), lambda j, i: (i, j)),
        out_shape=jax.ShapeDtypeStruct((m, d), BF16),
        scratch_shapes=[pltpu.VMEM((N_BRANCH, c, tn), BF16)],
        compiler_params=_params(2),
        name="branch_merge",
    )(*ys, pg, pg, pg, pg, w_branch)


def _resid_body(x_ref, y_ref, g_ref, mod_ref, xo_ref, *f_refs, gate_i, gy, gf, shift_i, scale_i, y_transposed):
    y = y_ref[...].T if y_transposed else y_ref[...]
    xn = x_ref[...] + mod_ref[gate_i:gate_i + 1, :] * (_rms(y) * g_ref[gy:gy + 1, :])
    xo_ref[...] = xn
    if f_refs:
        f = _rms(xn) * g_ref[gf:gf + 1, :]
        f = f * (1.0 + mod_ref[scale_i:scale_i + 1, :]) + mod_ref[shift_i:shift_i + 1, :]
        f_refs[0][...] = f.astype(BF16)
        f_refs[1][...] = f.T.astype(BF16)


def _resid(x, y, gain, mod, l, m, seq, nb, tm, gate_i, gy, with_f, y_transposed):
    d = x.shape[1]
    row = pl.BlockSpec((tm, d), lambda i: (i, 0))
    col = pl.BlockSpec((d, tm), lambda i: (0, i))
    out_shape = [jax.ShapeDtypeStruct((m, d), F32)]
    out_specs = [row]
    if with_f:
        out_shape += [jax.ShapeDtypeStruct((m, d), BF16), jax.ShapeDtypeStruct((d, m), BF16)]
        out_specs += [row, col]
    return pl.pallas_call(
        functools.partial(_resid_body, gate_i=gate_i, gy=gy, gf=2, shift_i=3, scale_i=4, y_transposed=y_transposed),
        grid=(m // tm,),
        in_specs=[row, col if y_transposed else row, pl.BlockSpec((None, 4, d), lambda i: (l, 0, 0)),
                  pl.BlockSpec((None, N_MOD, d), lambda i: (_mod_row(i, tm, seq, nb), 0, 0))],
        out_specs=out_specs,
        out_shape=out_shape,
        compiler_params=_params(1),
        name="residual",
    )(x, y, gain, mod)


def _top_rows(x, vals_ref, idx_ref, want_rank):
    n_rows = x.shape[0]
    iota = lax.broadcasted_iota(jnp.int32, x.shape, 0)

    def step(r, carry):
        x = carry[0]
        mx = jnp.max(x, axis=0, keepdims=True)
        vals_ref[pl.ds(r, 1), :] = mx
        first = jnp.min(jnp.where(x == mx, iota, n_rows), axis=0, keepdims=True)
        idx_ref[pl.ds(r, 1), :] = first
        hit = iota == first
        x = jnp.where(hit, -jnp.inf, x)
        return (x, jnp.where(hit, jnp.asarray(r, F32), carry[1])) if want_rank else (x,)

    init = (x, jnp.full(x.shape, NO_RANK, F32)) if want_rank else (x,)
    return lax.fori_loop(0, PEER_TOPK, step, init)[-1]


def _top_rows_distinct(x, vals_ref, want_rank):
    def step(r, carry):
        x = carry[0]
        mx = jnp.max(x, axis=0, keepdims=True)
        vals_ref[pl.ds(r, 1), :] = mx
        hit = x == mx
        x = jnp.where(hit, -jnp.inf, x)
        return (x, jnp.where(hit, jnp.asarray(r, F32), carry[1])) if want_rank else (x,)

    init = (x, jnp.full(x.shape, NO_RANK, F32)) if want_rank else (x,)
    out = lax.fori_loop(0, PEER_TOPK, step, init)
    removed = jnp.sum((out[0] == -jnp.inf).astype(F32), axis=0, keepdims=True)
    return removed != float(PEER_TOPK), out[-1]


def _pair_rows(a, b):
    half = PEER_TOPK // 2
    return jnp.concatenate([a[0:1, :] + b] + [a[r:r + 1, :] + b[0:half, :] for r in range(1, half)]
                           + [a[half:, :] + b[0:1, :]], axis=0)


def _route_body(q_ref, keys_ref, e1_ref, cnt_ref, rank2_ref, e2_ref, va_s, vb_s, vc_s, ia_s, ib_s, *, dq):
    half = PEER_TOPK // 2
    iota = lax.broadcasted_iota(jnp.int32, (N_KEYS, q_ref.shape[0]), 0)
    for h in range(PEER_HEADS):
        scores = []
        for p in range(2):
            qh = q_ref[:, h * 2 * dq + p * dq:h * 2 * dq + (p + 1) * dq].astype(BF16)
            scores.append(lax.dot_general(keys_ref[h, p].astype(BF16), qh, NT_DIMS, preferred_element_type=F32))

        def emit(a, b, best, cnt, rank2):
            z = jnp.sum(jnp.exp(best - best[0:1, :]), axis=0, keepdims=True)
            e1_ref[h] = jnp.exp(scores[0] - a[0:1, :]) / z
            cnt_ref[h] = cnt
            rank2_ref[h] = rank2.astype(BF16)
            e2_ref[h] = jnp.exp(scores[1] - b[0:1, :]).astype(BF16)

        tied_a, _ = _top_rows_distinct(scores[0], va_s, False)
        tied_b, rank2 = _top_rows_distinct(scores[1], vb_s, True)
        a, b = va_s[...], vb_s[...]
        cand = _pair_rows(a, b)
        tied_c, _ = _top_rows_distinct(cand, vc_s, False)
        best = vc_s[...]
        keep = (cand >= best[PEER_TOPK - 1:PEER_TOPK, :]).astype(F32)
        groups = [(0, PEER_TOPK)] + [(PEER_TOPK + half * (r - 1), PEER_TOPK + half * r) for r in range(1, half)] \
            + [(PEER_TOPK + half * (half - 1) + r, PEER_TOPK + half * (half - 1) + r + 1) for r in range(half)]
        cnt = jnp.zeros(scores[0].shape, F32)
        for i, (lo, hi) in enumerate(groups):
            n_i = jnp.sum(keep[lo:hi, :], axis=0, keepdims=True)
            cnt = jnp.where(scores[0] == a[i:i + 1, :], n_i, cnt)
        emit(a, b, best, cnt, rank2)
        tied = jnp.max((tied_a | tied_b | tied_c).astype(F32)) > 0.0

        @pl.when(tied)
        def _():
            _top_rows(scores[0], va_s, ia_s, False)
            rank2 = _top_rows(scores[1], vb_s, ib_s, True)
            a, b = va_s[...], vb_s[...]
            _top_rows(_pair_rows(a, b), vc_s, ib_s, False)
            row = ib_s[...]
            sel_i = jnp.where(row < PEER_TOPK, 0,
                              jnp.where(row < PEER_TOPK + half * (half - 1), (row - half) // half, row - half * half))
            key_a = ia_s[...]
            cnt = jnp.zeros(scores[0].shape, F32)
            for i in range(PEER_TOPK):
                n_i = jnp.sum((sel_i == i).astype(F32), axis=0, keepdims=True)
                cnt = jnp.where(iota == key_a[i:i + 1, :], n_i, cnt)
            emit(a, b, vc_s[...], cnt, rank2)


def _route(q, keys, l, tm):
    t = q.shape[0]
    dq = keys.shape[-1]
    spec = pl.BlockSpec((PEER_HEADS, N_KEYS, tm), lambda i: (0, 0, i))
    return pl.pallas_call(
        functools.partial(_route_body, dq=dq),
        grid=(t // tm,),
        in_specs=[pl.BlockSpec((tm, q.shape[1]), lambda i: (i, 0)),
                  pl.BlockSpec((None,) + keys.shape[1:], lambda i: (l, 0, 0, 0, 0))],
        out_specs=[spec] * 4,
        out_shape=[jax.ShapeDtypeStruct((PEER_HEADS, N_KEYS, t), dt) for dt in (F32, F32, BF16, BF16)],
        scratch_shapes=[pltpu.VMEM((PEER_TOPK, tm), F32)] * 3 + [pltpu.VMEM((PEER_TOPK, tm), jnp.int32)] * 2,
        compiler_params=_params(1),
        name="peer_route",
    )(q, keys)


def _peer_body(ft_ref, u_ref, vt_ref, e1_ref, cnt_ref, rank2_ref, e2_ref, o_ref, g_s, act_s, *, te, nj):
    j = pl.program_id(1)

    @pl.when(j == 0)
    def _():
        o_ref[...] = jnp.zeros_like(o_ref)
        act_s[...] = jnp.zeros_like(act_s)

    tile = jnp.maximum(j - 1, 0)
    tm = act_s.shape[1]
    zero = jnp.zeros((), BF16)
    for r in range(te // N_KEYS):
        i1 = tile * (te // N_KEYS) + r
        rows = slice(r * N_KEYS, (r + 1) * N_KEYS)
        e1_rows = [e1_ref[h, pl.ds(i1, 1), :].astype(BF16) for h in range(PEER_HEADS)]
        cnt_rows = [cnt_ref[h, pl.ds(i1, 1), :].astype(BF16) for h in range(PEER_HEADS)]
        for cb in range(tm // LANES):
            cs = slice(cb * LANES, (cb + 1) * LANES)
            w = None
            for h in range(PEER_HEADS):
                term = jnp.where(rank2_ref[h, :, cs] < cnt_rows[h][:, cs], e2_ref[h, :, cs] * e1_rows[h][:, cs], zero)
                w = term if w is None else w + term
            g_s[rows, cs] = w * act_s[rows, cs].astype(BF16)

    o_ref[...] += jnp.dot(vt_ref[...], g_s[...], preferred_element_type=F32)
    act_s[...] = jax.nn.gelu(jnp.dot(u_ref[...], ft_ref[...], preferred_element_type=F32))


def _peer(ft, u, vt, l, e1, cnt, rank2, e2, tm, te):
    d, t = ft.shape
    nj = u.shape[1] // te
    once = pl.Buffered(1)
    big_spec = pl.BlockSpec((PEER_HEADS, N_KEYS, tm), lambda i, j: (0, 0, i), pipeline_mode=once)
    return pl.pallas_call(
        functools.partial(_peer_body, te=te, nj=nj),
        grid=(t // tm, nj + 1),
        in_specs=[pl.BlockSpec((d, tm), lambda i, j: (0, i), pipeline_mode=once),
                  pl.BlockSpec((None, te, d), lambda i, j: (l, jnp.minimum(j, nj - 1), 0)),
                  pl.BlockSpec((None, d, te), lambda i, j: (l, 0, jnp.maximum(j - 1, 0))),
                  big_spec, big_spec, big_spec, big_spec],
        out_specs=pl.BlockSpec((d, tm), lambda i, j: (0, i)),
        out_shape=jax.ShapeDtypeStruct((d, t), F32),
        scratch_shapes=[pltpu.VMEM((te, tm), BF16), pltpu.VMEM((te, tm), F32)],
        compiler_params=_params(2),
        name="peer_experts",
    )(ft, u, vt, e1, cnt, rank2, e2)


def _block_diag(w):
    *lead, n, r, c = w.shape
    eye = jnp.eye(n, dtype=w.dtype)
    return (w[..., :, :, None, :] * eye[:, None, :, None]).reshape(*lead, n * r, n * c)


def _lru_params(w_a, b_a, w_i, b_i, lam):
    wd = jnp.concatenate([_block_diag(w_a), _block_diag(w_i)], axis=-1)
    bd = jnp.concatenate([b_a, b_i], axis=-1)[..., None, :]
    sp = jax.nn.softplus(-lam)[..., None, :]
    return wd.astype(BF16), bd, sp


def _s5_params(a_re, a_im, log_dt, b_re, b_im, c_re, c_im):
    lead = a_re.shape[:2]
    g, p = a_re.shape[2:]
    gc = g // S5_CHUNKS
    big_a = lax.complex(a_re, a_im)
    adt = big_a * jnp.exp(log_dt)[..., None]
    a_bar = jnp.exp(adt)
    b_bar = ((a_bar - 1.0) / big_a)[..., None] * lax.complex(b_re, b_im)
    c_mat = lax.complex(c_re, c_im)
    bm = _block_diag(jnp.swapaxes(b_bar, -1, -2).reshape(*lead, S5_CHUNKS, gc, S5_GROUP, p))
    cm = _block_diag(jnp.swapaxes(c_mat, -1, -2).reshape(*lead, S5_CHUNKS, gc, p, S5_GROUP))
    steps = jnp.arange(1, SUBLANES + 1, dtype=F32)[:, None, None]
    pows = jnp.exp(adt[:, :, None] * steps).reshape(*lead, SUBLANES, g * p)
    row = jnp.arange(SUBLANES)
    shifts = jnp.array([1, 2, 4])
    ok = jnp.stack([row[None, :] >= shifts[:, None], row[None, :] < SUBLANES - shifts[:, None]])
    dbl = jnp.where(ok[None, :, :, :, None], pows[:, :, shifts - 1][:, :, :, None, :], 0.0)
    pw = jnp.stack([pows[:, 0], pows[:, 1, ::-1]], axis=1)
    split = lambda z: jnp.stack([jnp.real(z), jnp.imag(z)], axis=2)
    return (jnp.real(bm).astype(BF16), jnp.imag(bm).astype(BF16), jnp.real(cm).astype(BF16),
            jnp.imag(cm).astype(BF16), split(dbl), split(pw))


def _to_row_major(t, nb, seq):
    rows = seq // GRID_W
    lat = t[:nb * seq].reshape(nb, GRID_W, rows, -1).transpose(0, 2, 1, 3).reshape(nb * seq, -1)
    return jnp.concatenate([lat, t[nb * seq:]], axis=0)


def kernel(x, c, ctx, c_ctx, w_ada, b_ada, norm_gain, w_in, lru_conv_w, lru_conv_b, lru_w_a, lru_b_a, lru_w_i,
           lru_b_i, lru_lambda, s5_a_re, s5_a_im, s5_log_dt, s5_b_re, s5_b_im, s5_c_re, s5_c_im, s5_d, s5_w_glu,
           gla_w_alpha, gla_b_alpha, gla_norm, hgrn_lb_logits, hgrn_norm, w_branch, w_out, peer_w_q, peer_keys,
           peer_u, peer_v):
    nb, seq, d = x.shape
    nctx = ctx.shape[1]
    depth = w_ada.shape[0]
    mix = d // 4
    n_lat = nb * seq
    n_tok = n_lat + nb * nctx
    assert nb + 1 <= SUBLANES and seq % GRID_W == 0
    gla_dk, gla_dv = mix // 2 // GLA_HEADS, mix // GLA_HEADS
    hg_dk = hg_dv = mix // HGRN_HEADS
    gla_rank = gla_w_alpha.shape[2]
    tm = _tile(math.gcd(seq, nb * nctx), (512, 256, 128))
    tr = min(tm, 256)
    tw = min(tm, 256)
    tt = _tile(math.gcd(seq, nctx), (256, 128, 64, 32))
    ch = 64

    w_in_t = jnp.swapaxes(w_in, 1, 2)
    o_c = 3 * mix
    o_ca = o_c + 2 * GLA_HEADS * gla_dk + GLA_HEADS * gla_dv + mix
    o_dg = 2 * HGRN_HEADS * hg_dk + HGRN_HEADS * hg_dk + HGRN_HEADS * hg_dv
    o_gt = o_dg + mix
    za_w = 128

    p_lb = jax.nn.softmax(hgrn_lb_logits.astype(F32), axis=0)
    lower = jnp.cumsum(p_lb, axis=0) - p_lb[0]
    cvec = jnp.zeros((SUBLANES, d), F32).at[:nb].set(c).at[nb].set(c_ctx)
    b_ada3 = b_ada[:, None, :]
    u_tab = peer_u.astype(BF16)
    vt_tab = jnp.swapaxes(peer_v, 1, 2).astype(BF16)

    wd, bd, sp = _lru_params(lru_w_a, lru_b_a, lru_w_i, lru_b_i, lru_lambda)
    s5p = _s5_params(s5_a_re, s5_a_im, s5_log_dt, s5_b_re, s5_b_im, s5_c_re, s5_c_im)

    xs = jnp.concatenate([x.reshape(n_lat, d), ctx.reshape(nb * nctx, d)], axis=0)
    for l in range(depth):
        last = l == depth - 1
        m_out = n_lat if last else n_tok
        mod = _adaln(cvec, w_ada, b_ada3, l).reshape(SUBLANES, N_MOD, d)
        h = _normmod(xs, norm_gain, mod, l, seq, nb, tr)

        mm_in = lambda off, n, shift, tn, name: _mmt(h, w_in_t, l, off, n, shift, tn, n_tok, tw, name)
        pa = mm_in(0, 2 * mix, 0, mix, "proj_lru")
        pb = mm_in(2 * mix, mix, 0, mix, "proj_s5")
        pc = mm_in(o_c, o_ca - o_c - mix, 0, mix, "proj_gla")
        pcg = mm_in(o_ca - mix, mix, 0, mix, "proj_gla_gate")
        za = mm_in(o_ca, za_w, 0, za_w, "proj_gla_rank")
        pd = mm_in(o_ca, o_dg, gla_rank, mix, "proj_hgrn")
        pdg = mm_in(o_ca + o_dg, mix, gla_rank, mix, "proj_hgrn_gate")
        pg = mm_in(o_ca + o_gt, N_BRANCH * d, gla_rank, mix, "proj_gate")

        hf, hb = _lru(pa, lru_conv_w, lru_conv_b[:, None, :], wd, bd, sp, l, nb, seq, nctx, tt)
        ya = _lru_out(pa, hf, hb, tm)

        yf, ybk = _s5(pb, *s5p, l, nb, seq, nctx, tt)
        yb = _to_row_major(_s5_glu(pb, yf, ybk, s5_d[:, None, :], s5_w_glu, l, n_lat, seq, tm), nb, seq)

        wa = jnp.zeros((2, za_w, GLA_HEADS * gla_dk), F32).at[:, :gla_rank].set(gla_w_alpha[l]).astype(BF16)
        of, ob = _gla(pc, za, wa, gla_b_alpha[l][:, None, :], nb, seq, nctx, tt, ch, GLA_HEADS, gla_dk, gla_dv)
        yc = _headnorm(of, ob, pcg, 0, gla_norm[:, None, :], l, GLA_HEADS, tm)

        of, ob = _hgrn(pd, lower[l][None, :], nb, seq, nctx, tt, ch, HGRN_HEADS, hg_dk, hg_dv)
        yd = _to_row_major(_headnorm(of, ob, pdg, 0, hgrn_norm[:, None, :], l, HGRN_HEADS, tm, (n_lat, seq)), nb, seq)

        zm = _merge((ya, yb, yc, yd), pg, w_branch, l, m_out, tm)
        mo = _mm(zm, w_out, l, d, F32, m_out, tw, mix, "proj_out")
        xs, f, ft = _resid(xs, mo, norm_gain, mod, l, m_out, seq, nb, tr, 2, 1, True, False)

        q = _mm(f, peer_w_q, l, peer_w_q.shape[2], F32, m_out, tw, mix, "peer_query")
        e1, cnt, rank2, e2 = _route(q, peer_keys, l, _tile(m_out, (256, 128)))
        yt = _peer(ft, u_tab, vt_tab, l, e1, cnt, rank2, e2, _tile(m_out, (512, 256, 128)), 512)
        xs = _resid(xs, yt, norm_gain, mod, l, m_out, seq, nb, tr, 5, 3, False, True)[0]
    return xs[:n_lat].reshape(nb, seq, d)
```

```python
import functools
import math

import jax
import jax.numpy as jnp
from jax import lax
from jax.experimental import pallas as pl
from jax.experimental.pallas import tpu as pltpu

F32 = jnp.float32
BF16 = jnp.bfloat16
EPS = 1e-6
GRID_W = 64
N_MOD = 6
N_BRANCH = 4
LRU_BLOCKS = 16
LRU_C = 8.0
S5_GROUP = 16
S5_STATE = 64
S5_CHUNKS = 4
GLA_HEADS = 4
GLA_TAU = 16.0
HGRN_HEADS = 8
PEER_HEADS = 8
N_KEYS = 128
PEER_TOPK = 16
NO_RANK = 255.0
SUBLANES = 8
LANES = 128
VMEM_LIMIT = 56 * 1024 * 1024

NT_DIMS = (((1,), (1,)), ((), ()))
TN_DIMS = (((0,), (0,)), ((), ()))


def _params(n_axes, vmem=VMEM_LIMIT):
    return pltpu.CompilerParams(dimension_semantics=("arbitrary",) * n_axes, vmem_limit_bytes=vmem)


def _tile(n, prefs):
    for p in prefs:
        if n % p == 0:
            return p
    raise ValueError(f"no tile for {n} in {prefs}")


def _rms(x):
    return x * lax.rsqrt(jnp.mean(x * x, axis=-1, keepdims=True) + EPS)


def _ada_body(c_ref, w_ref, b_ref, o_ref):
    c = c_ref[...]
    s = (c * jax.nn.sigmoid(c)).astype(BF16)
    o_ref[...] = jnp.dot(s, w_ref[...].astype(BF16), preferred_element_type=F32) + b_ref[...]


def _adaln(cvec, w_ada, b_ada3, l):
    rows, d = cvec.shape
    n = w_ada.shape[2]
    tn = 512
    return pl.pallas_call(
        _ada_body,
        grid=(n // tn,),
        in_specs=[
            pl.BlockSpec((rows, d), lambda j: (0, 0)),
            pl.BlockSpec((None, d, tn), lambda j: (l, 0, j)),
            pl.BlockSpec((None, 1, tn), lambda j: (l, 0, j)),
        ],
        out_specs=pl.BlockSpec((rows, tn), lambda j: (0, j)),
        out_shape=jax.ShapeDtypeStruct((rows, n), F32),
        compiler_params=_params(1),
        name="adaln",
    )(cvec, w_ada, b_ada3)


def _normmod_body(x_ref, g_ref, mod_ref, o_ref, *, gi, shift_i, scale_i):
    y = _rms(x_ref[...]) * g_ref[gi:gi + 1, :]
    o_ref[...] = (y * (1.0 + mod_ref[scale_i:scale_i + 1, :]) + mod_ref[shift_i:shift_i + 1, :]).astype(o_ref.dtype)


def _mod_row(i, tm, seq, nb):
    return jnp.minimum((i * tm) // seq, nb)


def _normmod(x, gain, mod, l, seq, nb, tm):
    t, d = x.shape
    return pl.pallas_call(
        functools.partial(_normmod_body, gi=0, shift_i=0, scale_i=1),
        grid=(t // tm,),
        in_specs=[
            pl.BlockSpec((tm, d), lambda i: (i, 0)),
            pl.BlockSpec((None, 4, d), lambda i: (l, 0, 0)),
            pl.BlockSpec((None, N_MOD, d), lambda i: (_mod_row(i, tm, seq, nb), 0, 0)),
        ],
        out_specs=pl.BlockSpec((tm, d), lambda i: (i, 0)),
        out_shape=jax.ShapeDtypeStruct((t, d), BF16),
        compiler_params=_params(1),
        name="normmod",
    )(x, gain, mod)


def _mm_body(a_ref, w_ref, o_ref, wb_ref):
    @pl.when(pl.program_id(1) == 0)
    def _():
        wb_ref[...] = w_ref[...].astype(BF16)

    o_ref[...] = jnp.dot(a_ref[...], wb_ref[...], preferred_element_type=F32).astype(o_ref.dtype)


def _mm(a, w, l, ncols, out_dtype, m, tm, tn, name):
    k = a.shape[1]
    w_spec = pl.BlockSpec((None, k, tn), lambda j, i: (l, 0, j))
    return pl.pallas_call(
        _mm_body,
        grid=(ncols // tn, m // tm),
        in_specs=[pl.BlockSpec((tm, k), lambda j, i: (i, 0)), w_spec],
        out_specs=pl.BlockSpec((tm, tn), lambda j, i: (i, j)),
        out_shape=jax.ShapeDtypeStruct((m, ncols), out_dtype),
        scratch_shapes=[pltpu.VMEM((k, tn), BF16)],
        compiler_params=_params(2),
        name=name,
    )(a, w)


def _mmt_body(a_ref, w_ref, *rest, shift):
    if shift:
        tail_ref, o_ref, wb_ref = rest
    else:
        o_ref, wb_ref = rest
    tn = wb_ref.shape[0]

    @pl.when(pl.program_id(1) == 0)
    def _():
        if shift:
            wb_ref[0:tn - shift, :] = w_ref[shift:tn, :].astype(BF16)
            wb_ref[tn - shift:tn, :] = tail_ref[...].astype(BF16)
        else:
            wb_ref[...] = w_ref[...].astype(BF16)

    o_ref[...] = lax.dot_general(a_ref[...], wb_ref[...], NT_DIMS, preferred_element_type=F32).astype(o_ref.dtype)


def _mmt(a, wt, l, row_off, ncols, shift, tn, m, tm, name):
    k = a.shape[1]
    base = row_off // tn
    in_specs = [pl.BlockSpec((tm, k), lambda j, i: (i, 0)), pl.BlockSpec((None, tn, k), lambda j, i: (l, base + j, 0))]
    args = [a, wt]
    if shift:
        per = tn // shift
        in_specs.append(pl.BlockSpec((None, shift, k), lambda j, i: (l, (base + j + 1) * per, 0)))
        args.append(wt)
    return pl.pallas_call(
        functools.partial(_mmt_body, shift=shift),
        grid=(ncols // tn, m // tm),
        in_specs=in_specs,
        out_specs=pl.BlockSpec((tm, tn), lambda j, i: (i, j)),
        out_shape=jax.ShapeDtypeStruct((m, ncols), F32),
        scratch_shapes=[pltpu.VMEM((tn, k), BF16)],
        compiler_params=_params(2),
        name=name,
    )(*args)


def _seq_block(b, i, rev, nb, seq, ctx, tt):
    nctx, nlat = ctx // tt, seq // tt
    ic = (nctx - 1 - i) if rev else i
    il = (nlat - 1 - (i - nctx)) if rev else (i - nctx)
    return jnp.where(i < nctx, (nb * seq) // tt + b * nctx + ic, b * nlat + il)


def _seq_spec(cols, col_blk, rev, nb, seq, ctx, tt):
    return pl.BlockSpec((tt, cols), lambda b, i: (_seq_block(b, i, rev, nb, seq, ctx, tt), col_blk))


def _grid_view(arr):
    t, c = arr.shape
    return arr.reshape(t // GRID_W, GRID_W, c)


def _col_tokens(lat_ref, first, k, a, b):
    return jnp.concatenate([lat_ref[:, first + j, a:b] for j in range(k)], axis=0)


def _lat_step(i, rev, seq, ctx, tt):
    nctx, nlat = ctx // tt, seq // tt
    il = (nlat - 1 - (i - nctx)) if rev else (i - nctx)
    return jnp.clip(il, 0, nlat - 1)


def _col_lat_spec(width, col_blk, rev, seq, ctx, tt):
    rows = seq // GRID_W
    per = SUBLANES // (tt // rows)
    return pl.BlockSpec((rows, SUBLANES, width), lambda b, i: (b, _lat_step(i, rev, seq, ctx, tt) // per, col_blk))


def _first_col(i, rev, seq, ctx, tt):
    k = tt // (seq // GRID_W)
    return (_lat_step(i, rev, seq, ctx, tt) % (SUBLANES // k)) * k


def _ctx_spec(cols, col_blk, rev, nb, seq, ctx, tt):
    nctx = ctx // tt

    def index(b, i):
        ic = (nctx - 1 - i) if rev else i
        return (nb * seq) // tt + b * nctx + jnp.clip(ic, 0, nctx - 1), col_blk

    return pl.BlockSpec((tt, cols), index)


def _halo_spec(cols, col_blk, rev, side, nb, seq, ctx, tt, total):
    per = tt // SUBLANES
    last = total // SUBLANES - 1

    def index(b, i):
        blk = _seq_block(b, i, rev, nb, seq, ctx, tt) * per
        blk = blk - 1 if side < 0 else blk + per
        return jnp.clip(blk, 0, last), col_blk

    return pl.BlockSpec((SUBLANES, cols), index)


def _stream_pos(i, rev, seq, ctx, tt):
    nctx, nlat = ctx // tt, seq // tt
    is_ctx = i < nctx
    ii = jnp.where(is_ctx, (nctx - 1 - i) if rev else i, (nlat - 1 - (i - nctx)) if rev else (i - nctx))
    n = jnp.where(is_ctx, nctx, nlat)
    return ii == 0, ii == n - 1


def _scan_tile_real(a, u, carry, rev):
    row = lax.broadcasted_iota(jnp.int32, a.shape, 0)
    for s in (1, 2, 4):
        if rev:
            a_sh, u_sh = pltpu.roll(a, SUBLANES - s, 0), pltpu.roll(u, SUBLANES - s, 0)
            ok = row < SUBLANES - s
        else:
            a_sh, u_sh = pltpu.roll(a, s, 0), pltpu.roll(u, s, 0)
            ok = row >= s
        u = jnp.where(ok, a * u_sh + u, u)
        a = jnp.where(ok, a * a_sh, a)
    h = u + a * carry
    return h, (h[0:1] if rev else h[SUBLANES - 1:SUBLANES])


def _lru_body(xf_ref, pf_ref, nf_ref, xb_ref, pb_ref, nb_ref, cw_ref, cb_ref, wd_ref, bd_ref, sp_ref,
              hf_ref, hb_ref, a_s, u_s, carry_s, *, seq, ctx, tt, c):
    i = pl.program_id(1)

    @pl.when(i == 0)
    def _():
        carry_s[...] = jnp.zeros_like(carry_s)

    row = lax.broadcasted_iota(jnp.int32, (tt, c), 0)
    for d, (x_ref, p_ref, n_ref) in enumerate(((xf_ref, pf_ref, nf_ref), (xb_ref, pb_ref, nb_ref))):
        first, last = _stream_pos(i, d == 1, seq, ctx, tt)
        x = x_ref[...]
        prev = jnp.where(first, 0.0, p_ref[SUBLANES - 1:SUBLANES, :])
        nx1 = jnp.where(last, 0.0, n_ref[0:1, :])
        nx2 = jnp.where(last, 0.0, n_ref[1:2, :])
        xm1 = jnp.where(row == 0, prev, pltpu.roll(x, 1, 0))
        xp1 = jnp.where(row == tt - 1, nx1, pltpu.roll(x, tt - 1, 0))
        xp2 = jnp.where(row == tt - 1, nx2, jnp.where(row == tt - 2, nx1, pltpu.roll(x, tt - 2, 0)))
        xl = xm1 * cw_ref[0:1, :] + x * cw_ref[1:2, :] + xp1 * cw_ref[2:3, :] + xp2 * cw_ref[3:4, :] + cb_ref[...]
        z = jnp.dot(xl.astype(BF16), wd_ref[d], preferred_element_type=F32) + bd_ref[d]
        r = jax.nn.sigmoid(z[:, :c])
        gi = jax.nn.sigmoid(z[:, c:])
        log_a = -LRU_C * r * sp_ref[d]
        a = jnp.exp(log_a)
        a_s[d] = a
        u_s[d] = jnp.sqrt(-jnp.tanh(log_a) * (a * a + 1.0)) * gi * xl

    ntile = tt // SUBLANES

    def step(k, carry):
        cf, cb = carry
        rows_f = pl.ds(pl.multiple_of(k * SUBLANES, SUBLANES), SUBLANES)
        rows_b = pl.ds(pl.multiple_of((ntile - 1 - k) * SUBLANES, SUBLANES), SUBLANES)
        h, cf = _scan_tile_real(a_s[0, rows_f, :], u_s[0, rows_f, :], cf, False)
        hf_ref[rows_f, :] = h
        h, cb = _scan_tile_real(a_s[1, rows_b, :], u_s[1, rows_b, :], cb, True)
        hb_ref[rows_b, :] = h
        return cf, cb

    cf, cb = lax.fori_loop(0, ntile, step, (carry_s[0, 0:1, :], carry_s[1, 0:1, :]))
    carry_s[0, 0:1, :] = cf
    carry_s[1, 0:1, :] = cb


def _layer_spec(arr, l):
    return pl.BlockSpec((None,) + arr.shape[1:], lambda b, i: (l,) + (0,) * (arr.ndim - 1))


def _lru(pa, cw, cb, wd, bd, sp, l, nb, seq, ctx, tt):
    t = pa.shape[0]
    c = cw.shape[-1]
    geo = (nb, seq, ctx, tt)
    return pl.pallas_call(
        functools.partial(_lru_body, seq=seq, ctx=ctx, tt=tt, c=c),
        grid=(nb, (seq + ctx) // tt),
        in_specs=[
            _seq_spec(c, 0, False, *geo), _halo_spec(c, 0, False, -1, *geo, t), _halo_spec(c, 0, False, 1, *geo, t),
            _seq_spec(c, 0, True, *geo), _halo_spec(c, 0, True, -1, *geo, t), _halo_spec(c, 0, True, 1, *geo, t),
            *(_layer_spec(arr, l) for arr in (cw, cb, wd, bd, sp)),
        ],
        out_specs=[_seq_spec(c, 0, False, *geo), _seq_spec(c, 0, True, *geo)],
        out_shape=[jax.ShapeDtypeStruct((t, c), F32)] * 2,
        scratch_shapes=[pltpu.VMEM((2, tt, c), F32), pltpu.VMEM((2, tt, c), F32), pltpu.VMEM((2, SUBLANES, c), F32)],
        compiler_params=_params(2),
        name="rglru_scan",
    )(pa, pa, pa, pa, pa, pa, cw, cb, wd, bd, sp)


def _lru_out_body(ay_ref, hf_ref, hb_ref, o_ref):
    o_ref[...] = (jax.nn.gelu(ay_ref[...]) * (hf_ref[...] + hb_ref[...])).astype(o_ref.dtype)


def _lru_out(pa, hf, hb, tm):
    t, c = hf.shape
    return pl.pallas_call(
        _lru_out_body,
        grid=(t // tm,),
        in_specs=[pl.BlockSpec((tm, c), lambda i: (i, 1)), pl.BlockSpec((tm, c), lambda i: (i, 0)),
                  pl.BlockSpec((tm, c), lambda i: (i, 0))],
        out_specs=pl.BlockSpec((tm, c), lambda i: (i, 0)),
        out_shape=jax.ShapeDtypeStruct((t, c), BF16),
        compiler_params=_params(1),
        name="rglru_out",
    )(pa, hf, hb)


def _scan_tile_cplx(xr, xi, dbl_ref, pw_ref, d, lanes, cr, ci, rev):
    for n, s in enumerate((1, 2, 4)):
        ar = dbl_ref[d, 0, n, :, lanes]
        ai = dbl_ref[d, 1, n, :, lanes]
        shift = SUBLANES - s if rev else s
        sr, si = pltpu.roll(xr, shift, 0), pltpu.roll(xi, shift, 0)
        xr, xi = xr + ar * sr - ai * si, xi + ar * si + ai * sr
    pr = pw_ref[d, 0, :, lanes]
    pi = pw_ref[d, 1, :, lanes]
    hr = xr + pr * cr - pi * ci
    hi = xi + pr * ci + pi * cr
    sel = slice(0, 1) if rev else slice(SUBLANES - 1, SUBLANES)
    return hr, hi, hr[sel], hi[sel]


def _load_seq_block(dst_ref, d, lat_ref, ctx_ref, is_ctx, first, k):
    @pl.when(is_ctx)
    def _():
        dst_ref[d] = ctx_ref[...]

    @pl.when(jnp.logical_not(is_ctx))
    def _():
        dst_ref[d] = _col_tokens(lat_ref, first, k, 0, lat_ref.shape[2])


def _s5_body(ulf_ref, ucf_ref, ulb_ref, ucb_ref, bre_ref, bim_ref, cre_ref, cim_ref, dbl_ref, pw_ref, yf_ref, yb_ref,
             u_s, hr_s, hi_s, carry_s, *, tt, gw, sw, seq, ctx):
    i = pl.program_id(1)

    @pl.when(i == 0)
    def _():
        carry_s[...] = jnp.zeros_like(carry_s)

    for d, (lat_ref, ctx_ref) in enumerate(((ulf_ref, ucf_ref), (ulb_ref, ucb_ref))):
        _load_seq_block(u_s, d, lat_ref, ctx_ref, i < ctx // tt, _first_col(i, d == 1, seq, ctx, tt),
                        tt // (seq // GRID_W))

    ntile = tt // SUBLANES
    for ch in range(S5_CHUNKS):
        cols = slice(ch * gw, (ch + 1) * gw)
        lanes = slice(ch * sw, (ch + 1) * sw)
        for d in range(2):
            u = u_s[d, :, cols].astype(BF16)
            hr_s[d] = jnp.dot(u, bre_ref[d, ch], preferred_element_type=F32)
            hi_s[d] = jnp.dot(u, bim_ref[d, ch], preferred_element_type=F32)

        def step(k, carry):
            crf, cif, crb, cib = carry
            rows_f = pl.ds(pl.multiple_of(k * SUBLANES, SUBLANES), SUBLANES)
            rows_b = pl.ds(pl.multiple_of((ntile - 1 - k) * SUBLANES, SUBLANES), SUBLANES)
            hr, hi, crf, cif = _scan_tile_cplx(hr_s[0, rows_f, :], hi_s[0, rows_f, :], dbl_ref, pw_ref, 0, lanes,
                                               crf, cif, False)
            hr_s[0, rows_f, :] = hr
            hi_s[0, rows_f, :] = hi
            hr, hi, crb, cib = _scan_tile_cplx(hr_s[1, rows_b, :], hi_s[1, rows_b, :], dbl_ref, pw_ref, 1, lanes,
                                               crb, cib, True)
            hr_s[1, rows_b, :] = hr
            hi_s[1, rows_b, :] = hi
            return crf, cif, crb, cib

        init = tuple(carry_s[n, 0:1, lanes] for n in range(4))
        fin = lax.fori_loop(0, ntile, step, init)
        for n in range(4):
            carry_s[n, 0:1, lanes] = fin[n]
        for d, y_ref in enumerate((yf_ref, yb_ref)):
            y_ref[:, cols] = (jnp.dot(hr_s[d].astype(BF16), cre_ref[d, ch], preferred_element_type=F32)
                              - jnp.dot(hi_s[d].astype(BF16), cim_ref[d, ch], preferred_element_type=F32))


def _s5(u, bre, bim, cre, cim, dbl, pw, l, nb, seq, ctx, tt):
    t, c = u.shape
    gw = c // S5_CHUNKS
    sw = bre.shape[-1]
    geo = (nb, seq, ctx, tt)
    lat = lambda rev: _col_lat_spec(c, 0, rev, seq, ctx, tt)
    cx = lambda rev: _ctx_spec(c, 0, rev, *geo)
    uv = _grid_view(u)
    return pl.pallas_call(
        functools.partial(_s5_body, tt=tt, gw=gw, sw=sw, seq=seq, ctx=ctx),
        grid=(nb, (seq + ctx) // tt),
        in_specs=[lat(False), cx(False), lat(True), cx(True),
                  *(_layer_spec(arr, l) for arr in (bre, bim, cre, cim, dbl, pw))],
        out_specs=[_seq_spec(c, 0, False, *geo), _seq_spec(c, 0, True, *geo)],
        out_shape=[jax.ShapeDtypeStruct((t, c), F32)] * 2,
        scratch_shapes=[pltpu.VMEM((2, tt, c), F32), pltpu.VMEM((2, tt, sw), F32), pltpu.VMEM((2, tt, sw), F32),
                        pltpu.VMEM((4, SUBLANES, sw * S5_CHUNKS), F32)],
        compiler_params=_params(2),
        name="s5_scan",
    )(uv, u, uv, u, bre, bim, cre, cim, dbl, pw)


def _tile_specs(cols, col_blk, n_lat, seq, tm):
    rows = seq // GRID_W
    nlat, per = n_lat // tm, seq // tm

    def lat_index(i):
        il = jnp.minimum(i, nlat - 1)
        return il // per, il % per

    def lat_index3(i):
        b, blk = lat_index(i)
        return b, blk, col_blk

    return (pl.BlockSpec((rows, tm // rows, cols), lat_index3),
            pl.BlockSpec((tm, cols), lambda i: (jnp.maximum(i, nlat), col_blk)))


def _load_tile(dst_ref, lat_ref, ctx_ref, is_ctx):
    @pl.when(is_ctx)
    def _():
        dst_ref[...] = ctx_ref[...]

    @pl.when(jnp.logical_not(is_ctx))
    def _():
        dst_ref[...] = _col_tokens(lat_ref, 0, lat_ref.shape[1], 0, lat_ref.shape[2])


def _s5_glu_body(ul_ref, uc_ref, yf_ref, yb_ref, d_ref, w_ref, o_ref, wb_ref, u_s, *, c, nlat):
    @pl.when(pl.program_id(0) == 0)
    def _():
        wb_ref[...] = w_ref[...].astype(BF16)

    _load_tile(u_s, ul_ref, uc_ref, pl.program_id(0) >= nlat)
    y = jax.nn.gelu(u_s[...] * d_ref[...] + yf_ref[...] + yb_ref[...])
    z = jnp.dot(y.astype(BF16), wb_ref[...], preferred_element_type=F32)
    o_ref[...] = (z[:, :c] * jax.nn.sigmoid(z[:, c:])).astype(o_ref.dtype)


def _s5_glu(u, yf, yb, dskip, w_glu, l, n_lat, seq, tm):
    t, c = u.shape
    lat_spec, ctx_spec = _tile_specs(c, 0, n_lat, seq, tm)
    return pl.pallas_call(
        functools.partial(_s5_glu_body, c=c, nlat=n_lat // tm),
        grid=(t // tm,),
        in_specs=[lat_spec, ctx_spec] + [pl.BlockSpec((tm, c), lambda i: (i, 0))] * 2
        + [pl.BlockSpec((None, 1, c), lambda i: (l, 0, 0)), pl.BlockSpec((None, c, 2 * c), lambda i: (l, 0, 0))],
        out_specs=pl.BlockSpec((tm, c), lambda i: (i, 0)),
        out_shape=jax.ShapeDtypeStruct((t, c), BF16),
        scratch_shapes=[pltpu.VMEM((c, 2 * c), BF16), pltpu.VMEM((tm, c), F32)],
        compiler_params=_params(1),
        name="s5_glu",
    )(_grid_view(u), u, yf, yb, dskip, w_glu)


def _chunk_cumsum(x, rev, ch):
    row = lax.broadcasted_iota(jnp.int32, x.shape, 0)
    s = 1
    while s < ch:
        if rev:
            x = x + jnp.where(row < ch - s, pltpu.roll(x, ch - s, 0), 0.0)
        else:
            x = x + jnp.where(row >= s, pltpu.roll(x, s, 0), 0.0)
        s *= 2
    return x


def _chunk_scan(q_s, k_s, v_s, lf_s, st_s, o_refs, *, tt, ch, heads, dk, dv):
    nchunk = tt // ch
    tri_r = lax.broadcasted_iota(jnp.int32, (ch, ch), 0)
    tri_c = lax.broadcasted_iota(jnp.int32, (ch, ch), 1)

    def step(n, _):
        for d in range(2):
            rev = d == 1
            cc = (nchunk - 1 - n) if rev else n
            rows = pl.ds(pl.multiple_of(cc * ch, ch), ch)
            b = _chunk_cumsum(lf_s[d, rows, :], rev, ch)
            piv = ch // 2 if rev else ch // 2 - 1
            end = 0 if rev else ch - 1
            m = b[piv:piv + 1, :]
            bl = b[end:end + 1, :]
            qm = q_s[d, rows, :] * jnp.exp(b - m)
            km = k_s[d, rows, :] * jnp.exp(m - b)
            qg = (qm * jnp.exp(m)).astype(BF16)
            kg = (km * jnp.exp(bl - m)).astype(BF16)
            dec = jnp.exp(bl)
            qm = qm.astype(BF16)
            km = km.astype(BF16)
            v = v_s[d, rows, :].astype(BF16)
            keep = (tri_r <= tri_c) if rev else (tri_r >= tri_c)
            for h in range(heads):
                ks = slice(h * dk, (h + 1) * dk)
                vs = slice(h * dv, (h + 1) * dv)
                sc = lax.dot_general(qm[:, ks], km[:, ks], NT_DIMS, preferred_element_type=F32)
                sc = jnp.where(keep, sc, 0.0).astype(BF16)
                st = st_s[d, h]
                o = jnp.dot(sc, v[:, vs], preferred_element_type=F32)
                o = o + lax.dot_general(qg[:, ks], st.astype(BF16), NT_DIMS, preferred_element_type=F32)
                st_s[d, h] = st * dec[:, ks] + lax.dot_general(v[:, vs], kg[:, ks], TN_DIMS,
                                                               preferred_element_type=F32)
                o_refs[d][rows, vs] = o
        return 0

    lax.fori_loop(0, nchunk, step, 0)


def _gla_body(qf_ref, zf_ref, qb_ref, zb_ref, wa_ref, ba_ref, of_ref, ob_ref, q_s, k_s, v_s, lf_s, st_s,
              *, tt, ch, heads, dk, dv):
    @pl.when(pl.program_id(1) == 0)
    def _():
        st_s[...] = jnp.zeros_like(st_s)

    hk = heads * dk
    for d, (x_ref, z_ref) in enumerate(((qf_ref, zf_ref), (qb_ref, zb_ref))):
        q_s[d] = x_ref[:, :hk] * dk ** -0.5
        k_s[d] = x_ref[:, hk:2 * hk]
        v_s[d] = x_ref[:, 2 * hk:]
        la = jnp.dot(z_ref[...].astype(BF16), wa_ref[d], preferred_element_type=F32) + ba_ref[d]
        lf_s[d] = jax.nn.log_sigmoid(la) / GLA_TAU
    _chunk_scan(q_s, k_s, v_s, lf_s, st_s, (of_ref, ob_ref), tt=tt, ch=ch, heads=heads, dk=dk, dv=dv)


def _hgrn_body(*refs, tt, ch, heads, dk, dv, seq, ctx):
    lat_refs, ctx_refs = (refs[0:3], refs[6:9]), (refs[3:6], refs[9:12])
    lb_ref, of_ref, ob_ref, q_s, k_s, v_s, lf_s, st_s = refs[12:]
    i = pl.program_id(1)

    @pl.when(i == 0)
    def _():
        st_s[...] = jnp.zeros_like(st_s)

    lb = lb_ref[...]

    def prep(d, z, q, v):
        q_s[d] = jax.nn.silu(q) * dk ** -0.5
        v_s[d] = v
        k_s[d] = (1.0 - lb) * jax.nn.sigmoid(-z)
        lf_s[d] = jnp.log(lb + (1.0 - lb) * jax.nn.sigmoid(z))

    k_cols = tt // (seq // GRID_W)
    for d in range(2):
        @pl.when(i < ctx // tt)
        def _():
            prep(d, *(r[...] for r in ctx_refs[d]))

        @pl.when(i >= ctx // tt)
        def _():
            first = _first_col(i, d == 1, seq, ctx, tt)
            prep(d, *(_col_tokens(r, first, k_cols, 0, r.shape[2]) for r in lat_refs[d]))

    _chunk_scan(q_s, k_s, v_s, lf_s, st_s, (of_ref, ob_ref), tt=tt, ch=ch, heads=heads, dk=dk, dv=dv)


def _chunk_scratch(tt, heads, dk, dv):
    return [pltpu.VMEM((2, tt, heads * dk), F32), pltpu.VMEM((2, tt, heads * dk), F32),
            pltpu.VMEM((2, tt, heads * dv), F32), pltpu.VMEM((2, tt, heads * dk), F32),
            pltpu.VMEM((2, heads, dv, dk), F32)]


def _gla(pc, za, wa, ba, nb, seq, ctx, tt, ch, heads, dk, dv):
    t = pc.shape[0]
    geo = (nb, seq, ctx, tt)
    wide = 2 * heads * dk + heads * dv
    full = lambda shape: pl.BlockSpec(shape, lambda b, i: (0,) * len(shape))
    return pl.pallas_call(
        functools.partial(_gla_body, tt=tt, ch=ch, heads=heads, dk=dk, dv=dv),
        grid=(nb, (seq + ctx) // tt),
        in_specs=[_seq_spec(wide, 0, False, *geo), _seq_spec(za.shape[1], 0, False, *geo),
                  _seq_spec(wide, 0, True, *geo), _seq_spec(za.shape[1], 0, True, *geo),
                  full(wa.shape), full(ba.shape)],
        out_specs=[_seq_spec(heads * dv, 0, False, *geo), _seq_spec(heads * dv, 0, True, *geo)],
        out_shape=[jax.ShapeDtypeStruct((t, heads * dv), F32)] * 2,
        scratch_shapes=_chunk_scratch(tt, heads, dk, dv),
        compiler_params=_params(2),
        name="gla_scan",
    )(pc, za, pc, za, wa, ba)


def _hgrn(pd, lb, nb, seq, ctx, tt, ch, heads, dk, dv):
    t = pd.shape[0]
    geo = (nb, seq, ctx, tt)
    hk = heads * dk
    lat = lambda d: [_col_lat_spec(hk, blk, d == 1, seq, ctx, tt) for blk in (d, 2, 3)]
    cx = lambda d: [_ctx_spec(hk, blk, d == 1, *geo) for blk in (d, 2, 3)]
    pv = _grid_view(pd)
    return pl.pallas_call(
        functools.partial(_hgrn_body, tt=tt, ch=ch, heads=heads, dk=dk, dv=dv, seq=seq, ctx=ctx),
        grid=(nb, (seq + ctx) // tt),
        in_specs=lat(0) + cx(0) + lat(1) + cx(1) + [pl.BlockSpec(lb.shape, lambda b, i: (0, 0))],
        out_specs=[_seq_spec(heads * dv, 0, False, *geo), _seq_spec(heads * dv, 0, True, *geo)],
        out_shape=[jax.ShapeDtypeStruct((t, heads * dv), F32)] * 2,
        scratch_shapes=_chunk_scratch(tt, heads, dk, dv),
        compiler_params=_params(2),
        name="hgrn_scan",
    )(*([pv] * 3 + [pd] * 3) * 2, lb)


def _headnorm_body(of_ref, ob_ref, *rest, heads, dv, nlat):
    if nlat is None:
        g_ref, n_ref, o_ref = rest
        gate = g_ref[...]
    else:
        gl_ref, gc_ref, n_ref, o_ref, g_s = rest
        _load_tile(g_s, gl_ref, gc_ref, pl.program_id(0) >= nlat)
        gate = g_s[...]
    o = of_ref[...] + ob_ref[...]
    parts = [_rms(o[:, h * dv:(h + 1) * dv]) for h in range(heads)]
    y = jnp.concatenate(parts, axis=-1) * n_ref[...]
    o_ref[...] = (y * jax.nn.silu(gate)).astype(o_ref.dtype)


def _headnorm(of, ob, gate_arr, gate_blk, gain, l, heads, tm, col_major=None):
    t, c = of.shape
    row = pl.BlockSpec((tm, c), lambda i: (i, 0))
    if col_major is None:
        gate_specs, gate_args, scratch, nlat = [pl.BlockSpec((tm, c), lambda i: (i, gate_blk))], [gate_arr], [], None
    else:
        n_lat, seq = col_major
        gate_specs = list(_tile_specs(c, gate_blk, n_lat, seq, tm))
        gate_args, scratch, nlat = [_grid_view(gate_arr), gate_arr], [pltpu.VMEM((tm, c), F32)], n_lat // tm
    return pl.pallas_call(
        functools.partial(_headnorm_body, heads=heads, dv=c // heads, nlat=nlat),
        grid=(t // tm,),
        in_specs=[row, row] + gate_specs + [pl.BlockSpec((None, 1, c), lambda i: (l, 0, 0))],
        out_specs=row,
        out_shape=jax.ShapeDtypeStruct((t, c), BF16),
        scratch_shapes=scratch,
        compiler_params=_params(1),
        name="headnorm_gate",
    )(of, ob, *gate_args, gain)


def _merge_body(ya_ref, yb_ref, yc_ref, yd_ref, g0_ref, g1_ref, g2_ref, g3_ref, w_ref, o_ref, wb_ref):
    @pl.when(pl.program_id(1) == 0)
    def _():
        wb_ref[...] = w_ref[...].astype(BF16)

    acc = None
    for k, (y_ref, g_ref) in enumerate(((ya_ref, g0_ref), (yb_ref, g1_ref), (yc_ref, g2_ref), (yd_ref, g3_ref))):
        term = jax.nn.sigmoid(g_ref[...]) * jnp.dot(y_ref[...], wb_ref[k], preferred_element_type=F32)
        acc = term if acc is None else acc + term
    o_ref[...] = acc.astype(o_ref.dtype)


def _merge(ys, pg, w_branch, l, m, tm):
    c = ys[0].shape[1]
    d = w_branch.shape[3]
    tn = 512
    nj = d // tn
    y_spec = pl.BlockSpec((tm, c), lambda j, i: (i, 0))
    g_specs = [pl.BlockSpec((tm, tn), functools.partial(lambda j, i, k: (i, k * nj + j), k=k)) for k in range(N_BRANCH)]
    return pl.pallas_call(
        _merge_body,
        grid=(nj, m // tm),
        in_specs=[y_spec] * N_BRANCH + g_specs + [pl.BlockSpec((None, N_BRANCH, c, tn), lambda j, i: (l, 0, 0, j))],
        out_specs=pl.BlockSpec((tm, tn), lambda j, i: (i, j)),
        out_shape=jax.ShapeDtypeStruct((m, d), BF16),
        scratch_shapes=[pltpu.VMEM((N_BRANCH, c, tn), BF16)],
        compiler_params=_params(2),
        name="branch_merge",
    )(*ys, pg, pg, pg, pg, w_branch)


def _resid_body(x_ref, y_ref, g_ref, mod_ref, xo_ref, *f_refs, gate_i, gy, gf, shift_i, scale_i, y_transposed):
    y = y_ref[...].T if y_transposed else y_ref[...]
    xn = x_ref[...] + mod_ref[gate_i:gate_i + 1, :] * (_rms(y) * g_ref[gy:gy + 1, :])
    xo_ref[...] = xn
    if f_refs:
        f = _rms(xn) * g_ref[gf:gf + 1, :]
        f = f * (1.0 + mod_ref[scale_i:scale_i + 1, :]) + mod_ref[shift_i:shift_i + 1, :]
        f_refs[0][...] = f.astype(BF16)
        f_refs[1][...] = f.T.astype(BF16)


def _resid(x, y, gain, mod, l, m, seq, nb, tm, gate_i, gy, with_f, y_transposed):
    d = x.shape[1]
    row = pl.BlockSpec((tm, d), lambda i: (i, 0))
    col = pl.BlockSpec((d, tm), lambda i: (0, i))
    out_shape = [jax.ShapeDtypeStruct((m, d), F32)]
    out_specs = [row]
    if with_f:
        out_shape += [jax.ShapeDtypeStruct((m, d), BF16), jax.ShapeDtypeStruct((d, m), BF16)]
        out_specs += [row, col]
    return pl.pallas_call(
        functools.partial(_resid_body, gate_i=gate_i, gy=gy, gf=2, shift_i=3, scale_i=4, y_transposed=y_transposed),
        grid=(m // tm,),
        in_specs=[row, col if y_transposed else row, pl.BlockSpec((None, 4, d), lambda i: (l, 0, 0)),
                  pl.BlockSpec((None, N_MOD, d), lambda i: (_mod_row(i, tm, seq, nb), 0, 0))],
        out_specs=out_specs,
        out_shape=out_shape,
        compiler_params=_params(1),
        name="residual",
    )(x, y, gain, mod)


def _top_rows(x, vals_ref, idx_ref, want_rank):
    n_rows = x.shape[0]
    iota = lax.broadcasted_iota(jnp.int32, x.shape, 0)

    def step(r, carry):
        x = carry[0]
        mx = jnp.max(x, axis=0, keepdims=True)
        vals_ref[pl.ds(r, 1), :] = mx
        first = jnp.min(jnp.where(x == mx, iota, n_rows), axis=0, keepdims=True)
        idx_ref[pl.ds(r, 1), :] = first
        hit = iota == first
        x = jnp.where(hit, -jnp.inf, x)
        return (x, jnp.where(hit, jnp.asarray(r, F32), carry[1])) if want_rank else (x,)

    init = (x, jnp.full(x.shape, NO_RANK, F32)) if want_rank else (x,)
    return lax.fori_loop(0, PEER_TOPK, step, init)[-1]


def _top_rows_distinct(x, vals_ref, want_rank):
    def step(r, carry):
        x = carry[0]
        mx = jnp.max(x, axis=0, keepdims=True)
        vals_ref[pl.ds(r, 1), :] = mx
        hit = x == mx
        x = jnp.where(hit, -jnp.inf, x)
        return (x, jnp.where(hit, jnp.asarray(r, F32), carry[1])) if want_rank else (x,)

    init = (x, jnp.full(x.shape, NO_RANK, F32)) if want_rank else (x,)
    out = lax.fori_loop(0, PEER_TOPK, step, init)
    removed = jnp.sum((out[0] == -jnp.inf).astype(F32), axis=0, keepdims=True)
    return removed != float(PEER_TOPK), out[-1]


def _pair_rows(a, b):
    half = PEER_TOPK // 2
    return jnp.concatenate([a[0:1, :] + b] + [a[r:r + 1, :] + b[0:half, :] for r in range(1, half)]
                           + [a[half:, :] + b[0:1, :]], axis=0)


def _route_body(q_ref, keys_ref, e1_ref, cnt_ref, rank2_ref, e2_ref, va_s, vb_s, vc_s, ia_s, ib_s, *, dq):
    half = PEER_TOPK // 2
    iota = lax.broadcasted_iota(jnp.int32, (N_KEYS, q_ref.shape[0]), 0)
    for h in range(PEER_HEADS):
        scores = []
        for p in range(2):
            qh = q_ref[:, h * 2 * dq + p * dq:h * 2 * dq + (p + 1) * dq].astype(BF16)
            scores.append(lax.dot_general(keys_ref[h, p].astype(BF16), qh, NT_DIMS, preferred_element_type=F32))

        def emit(a, b, best, cnt, rank2):
            z = jnp.sum(jnp.exp(best - best[0:1, :]), axis=0, keepdims=True)
            e1_ref[h] = jnp.exp(scores[0] - a[0:1, :]) / z
            cnt_ref[h] = cnt
            rank2_ref[h] = rank2.astype(BF16)
            e2_ref[h] = jnp.exp(scores[1] - b[0:1, :]).astype(BF16)

        tied_a, _ = _top_rows_distinct(scores[0], va_s, False)
        tied_b, _ = _top_rows_distinct(scores[1], vb_s, False)
        a, b = va_s[...], vb_s[...]
        rank2 = jnp.full(scores[1].shape, NO_RANK, F32)
        for r in range(PEER_TOPK):
            rank2 = jnp.where(scores[1] == b[r:r + 1, :], float(r), rank2)
        cand = _pair_rows(a, b)
        tied_c, _ = _top_rows_distinct(cand, vc_s, False)
        best = vc_s[...]
        keep = (cand >= best[PEER_TOPK - 1:PEER_TOPK, :]).astype(F32)
        groups = [(0, PEER_TOPK)] + [(PEER_TOPK + half * (r - 1), PEER_TOPK + half * r) for r in range(1, half)] \
            + [(PEER_TOPK + half * (half - 1) + r, PEER_TOPK + half * (half - 1) + r + 1) for r in range(half)]
        cnt = jnp.zeros(scores[0].shape, F32)
        for i, (lo, hi) in enumerate(groups):
            n_i = jnp.sum(keep[lo:hi, :], axis=0, keepdims=True)
            cnt = jnp.where(scores[0] == a[i:i + 1, :], n_i, cnt)
        emit(a, b, best, cnt, rank2)
        tied = jnp.max((tied_a | tied_b | tied_c).astype(F32)) > 0.0

        @pl.when(tied)
        def _():
            _top_rows(scores[0], va_s, ia_s, False)
            rank2 = _top_rows(scores[1], vb_s, ib_s, True)
            a, b = va_s[...], vb_s[...]
            _top_rows(_pair_rows(a, b), vc_s, ib_s, False)
            row = ib_s[...]
            sel_i = jnp.where(row < PEER_TOPK, 0,
                              jnp.where(row < PEER_TOPK + half * (half - 1), (row - half) // half, row - half * half))
            key_a = ia_s[...]
            cnt = jnp.zeros(scores[0].shape, F32)
            for i in range(PEER_TOPK):
                n_i = jnp.sum((sel_i == i).astype(F32), axis=0, keepdims=True)
                cnt = jnp.where(iota == key_a[i:i + 1, :], n_i, cnt)
            emit(a, b, vc_s[...], cnt, rank2)


def _route(q, keys, l, tm):
    t = q.shape[0]
    dq = keys.shape[-1]
    spec = pl.BlockSpec((PEER_HEADS, N_KEYS, tm), lambda i: (0, 0, i))
    return pl.pallas_call(
        functools.partial(_route_body, dq=dq),
        grid=(t // tm,),
        in_specs=[pl.BlockSpec((tm, q.shape[1]), lambda i: (i, 0)),
                  pl.BlockSpec((None,) + keys.shape[1:], lambda i: (l, 0, 0, 0, 0))],
        out_specs=[spec] * 4,
        out_shape=[jax.ShapeDtypeStruct((PEER_HEADS, N_KEYS, t), dt) for dt in (F32, F32, BF16, BF16)],
        scratch_shapes=[pltpu.VMEM((PEER_TOPK, tm), F32)] * 3 + [pltpu.VMEM((PEER_TOPK, tm), jnp.int32)] * 2,
        compiler_params=_params(1),
        name="peer_route",
    )(q, keys)


def _peer_body(ft_ref, u_ref, vt_ref, e1_ref, cnt_ref, rank2_ref, e2_ref, o_ref, g_s, act_s, *, te, nj):
    j = pl.program_id(1)

    @pl.when(j == 0)
    def _():
        o_ref[...] = jnp.zeros_like(o_ref)
        act_s[...] = jnp.zeros_like(act_s)

    tile = jnp.maximum(j - 1, 0)
    tm = act_s.shape[1]
    zero = jnp.zeros((), BF16)
    for r in range(te // N_KEYS):
        i1 = tile * (te // N_KEYS) + r
        rows = slice(r * N_KEYS, (r + 1) * N_KEYS)
        e1_rows = [e1_ref[h, pl.ds(i1, 1), :].astype(BF16) for h in range(PEER_HEADS)]
        cnt_rows = [cnt_ref[h, pl.ds(i1, 1), :].astype(BF16) for h in range(PEER_HEADS)]
        for cb in range(tm // LANES):
            cs = slice(cb * LANES, (cb + 1) * LANES)
            w = None
            for h in range(PEER_HEADS):
                term = jnp.where(rank2_ref[h, :, cs] < cnt_rows[h][:, cs], e2_ref[h, :, cs] * e1_rows[h][:, cs], zero)
                w = term if w is None else w + term
            g_s[rows, cs] = w * act_s[rows, cs]

    o_ref[...] += jnp.dot(vt_ref[...], g_s[...], preferred_element_type=F32)
    act_s[...] = jax.nn.gelu(jnp.dot(u_ref[...], ft_ref[...], preferred_element_type=F32)).astype(BF16)


def _peer(ft, u, vt, l, e1, cnt, rank2, e2, tm, te):
    d, t = ft.shape
    nj = u.shape[1] // te
    once = pl.Buffered(1)
    big_spec = pl.BlockSpec((PEER_HEADS, N_KEYS, tm), lambda i, j: (0, 0, i), pipeline_mode=once)
    return pl.pallas_call(
        functools.partial(_peer_body, te=te, nj=nj),
        grid=(t // tm, nj + 1),
        in_specs=[pl.BlockSpec((d, tm), lambda i, j: (0, i), pipeline_mode=once),
                  pl.BlockSpec((None, te, d), lambda i, j: (l, jnp.minimum(j, nj - 1), 0)),
                  pl.BlockSpec((None, d, te), lambda i, j: (l, 0, jnp.maximum(j - 1, 0))),
                  big_spec, big_spec, big_spec, big_spec],
        out_specs=pl.BlockSpec((d, tm), lambda i, j: (0, i)),
        out_shape=jax.ShapeDtypeStruct((d, t), F32),
        scratch_shapes=[pltpu.VMEM((te, tm), BF16), pltpu.VMEM((te, tm), BF16)],
        compiler_params=_params(2),
        name="peer_experts",
    )(ft, u, vt, e1, cnt, rank2, e2)


def _block_diag(w):
    *lead, n, r, c = w.shape
    eye = jnp.eye(n, dtype=w.dtype)
    return (w[..., :, :, None, :] * eye[:, None, :, None]).reshape(*lead, n * r, n * c)


def _lru_params(w_a, b_a, w_i, b_i, lam):
    wd = jnp.concatenate([_block_diag(w_a.astype(BF16)), _block_diag(w_i.astype(BF16))], axis=-1)
    bd = jnp.concatenate([b_a, b_i], axis=-1)[..., None, :]
    sp = jax.nn.softplus(-lam)[..., None, :]
    return wd, bd, sp


def _s5_params(a_re, a_im, log_dt, b_re, b_im, c_re, c_im):
    lead = a_re.shape[:2]
    g, p = a_re.shape[2:]
    gc = g // S5_CHUNKS
    big_a = lax.complex(a_re, a_im)
    adt = big_a * jnp.exp(log_dt)[..., None]
    a_bar = jnp.exp(adt)
    b_bar = ((a_bar - 1.0) / big_a)[..., None] * lax.complex(b_re, b_im)
    c_mat = lax.complex(c_re, c_im)
    bm = jnp.swapaxes(b_bar, -1, -2).reshape(*lead, S5_CHUNKS, gc, S5_GROUP, p)
    cm = jnp.swapaxes(c_mat, -1, -2).reshape(*lead, S5_CHUNKS, gc, p, S5_GROUP)
    bre, bim, cre, cim = (_block_diag(part(m).astype(BF16)) for m in (bm, cm) for part in (jnp.real, jnp.imag))
    steps = jnp.arange(1, SUBLANES + 1, dtype=F32)[:, None, None]
    pows = jnp.exp(adt[:, :, None] * steps).reshape(*lead, SUBLANES, g * p)
    row = jnp.arange(SUBLANES)
    shifts = jnp.array([1, 2, 4])
    ok = jnp.stack([row[None, :] >= shifts[:, None], row[None, :] < SUBLANES - shifts[:, None]])
    dbl = jnp.where(ok[None, :, :, :, None], pows[:, :, shifts - 1][:, :, :, None, :], 0.0)
    pw = jnp.stack([pows[:, 0], pows[:, 1, ::-1]], axis=1)
    split = lambda z: jnp.stack([jnp.real(z), jnp.imag(z)], axis=2)
    return bre, bim, cre, cim, split(dbl), split(pw)


def _to_row_major(t, nb, seq):
    rows = seq // GRID_W
    lat = t[:nb * seq].reshape(nb, GRID_W, rows, -1).transpose(0, 2, 1, 3).reshape(nb * seq, -1)
    return jnp.concatenate([lat, t[nb * seq:]], axis=0)


def kernel(x, c, ctx, c_ctx, w_ada, b_ada, norm_gain, w_in, lru_conv_w, lru_conv_b, lru_w_a, lru_b_a, lru_w_i,
           lru_b_i, lru_lambda, s5_a_re, s5_a_im, s5_log_dt, s5_b_re, s5_b_im, s5_c_re, s5_c_im, s5_d, s5_w_glu,
           gla_w_alpha, gla_b_alpha, gla_norm, hgrn_lb_logits, hgrn_norm, w_branch, w_out, peer_w_q, peer_keys,
           peer_u, peer_v):
    nb, seq, d = x.shape
    nctx = ctx.shape[1]
    depth = w_ada.shape[0]
    mix = d // 4
    n_lat = nb * seq
    n_tok = n_lat + nb * nctx
    assert nb + 1 <= SUBLANES and seq % GRID_W == 0
    gla_dk, gla_dv = mix // 2 // GLA_HEADS, mix // GLA_HEADS
    hg_dk = hg_dv = mix // HGRN_HEADS
    gla_rank = gla_w_alpha.shape[2]
    tm = _tile(math.gcd(seq, nb * nctx), (512, 256, 128))
    tr = min(tm, 256)
    tw = min(tm, 256)
    tt = _tile(math.gcd(seq, nctx), (256, 128, 64, 32))
    ch = 64

    w_in_t = jnp.swapaxes(w_in, 1, 2)
    o_c = 3 * mix
    o_ca = o_c + 2 * GLA_HEADS * gla_dk + GLA_HEADS * gla_dv + mix
    o_dg = 2 * HGRN_HEADS * hg_dk + HGRN_HEADS * hg_dk + HGRN_HEADS * hg_dv
    o_gt = o_dg + mix
    za_w = 128

    p_lb = jax.nn.softmax(hgrn_lb_logits.astype(F32), axis=0)
    lower = jnp.cumsum(p_lb, axis=0) - p_lb[0]
    cvec = jnp.zeros((SUBLANES, d), F32).at[:nb].set(c).at[nb].set(c_ctx)
    b_ada3 = b_ada[:, None, :]
    u_tab = peer_u.astype(BF16)
    vt_tab = jnp.swapaxes(peer_v, 1, 2).astype(BF16)

    wd, bd, sp = _lru_params(lru_w_a, lru_b_a, lru_w_i, lru_b_i, lru_lambda)
    s5p = _s5_params(s5_a_re, s5_a_im, s5_log_dt, s5_b_re, s5_b_im, s5_c_re, s5_c_im)

    xs = jnp.concatenate([x.reshape(n_lat, d), ctx.reshape(nb * nctx, d)], axis=0)
    for l in range(depth):
        last = l == depth - 1
        m_out = n_lat if last else n_tok
        mod = _adaln(cvec, w_ada, b_ada3, l).reshape(SUBLANES, N_MOD, d)
        h = _normmod(xs, norm_gain, mod, l, seq, nb, tr)

        mm_in = lambda off, n, shift, tn, name: _mmt(h, w_in_t, l, off, n, shift, tn, n_tok, tw, name)
        pa = mm_in(0, 2 * mix, 0, mix, "proj_lru")
        pb = mm_in(2 * mix, mix, 0, mix, "proj_s5")
        pc = mm_in(o_c, o_ca - o_c - mix, 0, mix, "proj_gla")
        pcg = mm_in(o_ca - mix, mix, 0, mix, "proj_gla_gate")
        za = mm_in(o_ca, za_w, 0, za_w, "proj_gla_rank")
        pd = mm_in(o_ca, o_dg, gla_rank, mix, "proj_hgrn")
        pdg = mm_in(o_ca + o_dg, mix, gla_rank, mix, "proj_hgrn_gate")
        pg = mm_in(o_ca + o_gt, N_BRANCH * d, gla_rank, mix, "proj_gate")

        hf, hb = _lru(pa, lru_conv_w, lru_conv_b[:, None, :], wd, bd, sp, l, nb, seq, nctx, tt)
        ya = _lru_out(pa, hf, hb, tm)

        yf, ybk = _s5(pb, *s5p, l, nb, seq, nctx, tt)
        yb = _to_row_major(_s5_glu(pb, yf, ybk, s5_d[:, None, :], s5_w_glu, l, n_lat, seq, tm), nb, seq)

        wa = jnp.zeros((2, za_w, GLA_HEADS * gla_dk), F32).at[:, :gla_rank].set(gla_w_alpha[l]).astype(BF16)
        of, ob = _gla(pc, za, wa, gla_b_alpha[l][:, None, :], nb, seq, nctx, tt, ch, GLA_HEADS, gla_dk, gla_dv)
        yc = _headnorm(of, ob, pcg, 0, gla_norm[:, None, :], l, GLA_HEADS, tm)

        of, ob = _hgrn(pd, lower[l][None, :], nb, seq, nctx, tt, ch, HGRN_HEADS, hg_dk, hg_dv)
        yd = _to_row_major(_headnorm(of, ob, pdg, 0, hgrn_norm[:, None, :], l, HGRN_HEADS, tm, (n_lat, seq)), nb, seq)

        zm = _merge((ya, yb, yc, yd), pg, w_branch, l, m_out, tm)
        mo = _mm(zm, w_out, l, d, F32, m_out, tw, mix, "proj_out")
        xs, f, ft = _resid(xs, mo, norm_gain, mod, l, m_out, seq, nb, tr, 2, 1, True, False)

        q = _mm(f, peer_w_q, l, peer_w_q.shape[2], F32, m_out, tw, mix, "peer_query")
        e1, cnt, rank2, e2 = _route(q, peer_keys, l, _tile(m_out, (256, 128)))
        yt = _peer(ft, u_tab, vt_tab, l, e1, cnt, rank2, e2, _tile(m_out, (512, 256, 128)), 512)
        xs = _resid(xs, yt, norm_gain, mod, l, m_out, seq, nb, tr, 5, 3, False, True)[0]
    return xs[:n_lat].reshape(nb, seq, d)
```

```python
import functools
import math

import jax
import jax.numpy as jnp
from jax import lax
from jax.experimental import pallas as pl
from jax.experimental.pallas import tpu as pltpu

F32 = jnp.float32
BF16 = jnp.bfloat16
EPS = 1e-6
GRID_W = 64
N_MOD = 6
N_BRANCH = 4
LRU_BLOCKS = 16
LRU_C = 8.0
S5_GROUP = 16
S5_STATE = 64
S5_CHUNKS = 4
GLA_HEADS = 4
GLA_TAU = 16.0
HGRN_HEADS = 8
PEER_HEADS = 8
N_KEYS = 128
PEER_TOPK = 16
NO_RANK = 255.0
SUBLANES = 8
LANES = 128
VMEM_LIMIT = 56 * 1024 * 1024

NT_DIMS = (((1,), (1,)), ((), ()))
TN_DIMS = (((0,), (0,)), ((), ()))


def _params(n_axes, vmem=VMEM_LIMIT):
    return pltpu.CompilerParams(dimension_semantics=("arbitrary",) * n_axes, vmem_limit_bytes=vmem)


def _tile(n, prefs):
    for p in prefs:
        if n % p == 0:
            return p
    raise ValueError(f"no tile for {n} in {prefs}")


def _rms(x):
    return x * lax.rsqrt(jnp.mean(x * x, axis=-1, keepdims=True) + EPS)


def _ada_body(c_ref, w_ref, b_ref, o_ref):
    c = c_ref[...]
    s = (c * jax.nn.sigmoid(c)).astype(BF16)
    o_ref[...] = jnp.dot(s, w_ref[...].astype(BF16), preferred_element_type=F32) + b_ref[...]


def _adaln(cvec, w_ada, b_ada3, l):
    rows, d = cvec.shape
    n = w_ada.shape[2]
    tn = 512
    return pl.pallas_call(
        _ada_body,
        grid=(n // tn,),
        in_specs=[
            pl.BlockSpec((rows, d), lambda j: (0, 0)),
            pl.BlockSpec((None, d, tn), lambda j: (l, 0, j)),
            pl.BlockSpec((None, 1, tn), lambda j: (l, 0, j)),
        ],
        out_specs=pl.BlockSpec((rows, tn), lambda j: (0, j)),
        out_shape=jax.ShapeDtypeStruct((rows, n), F32),
        compiler_params=_params(1),
        name="adaln",
    )(cvec, w_ada, b_ada3)


def _normmod_body(x_ref, g_ref, mod_ref, o_ref, *, gi, shift_i, scale_i):
    y = _rms(x_ref[...]) * g_ref[gi:gi + 1, :]
    o_ref[...] = (y * (1.0 + mod_ref[scale_i:scale_i + 1, :]) + mod_ref[shift_i:shift_i + 1, :]).astype(o_ref.dtype)


def _mod_row(i, tm, seq, nb):
    return jnp.minimum((i * tm) // seq, nb)


def _normmod(x, gain, mod, l, seq, nb, tm):
    t, d = x.shape
    return pl.pallas_call(
        functools.partial(_normmod_body, gi=0, shift_i=0, scale_i=1),
        grid=(t // tm,),
        in_specs=[
            pl.BlockSpec((tm, d), lambda i: (i, 0)),
            pl.BlockSpec((None, 4, d), lambda i: (l, 0, 0)),
            pl.BlockSpec((None, N_MOD, d), lambda i: (_mod_row(i, tm, seq, nb), 0, 0)),
        ],
        out_specs=pl.BlockSpec((tm, d), lambda i: (i, 0)),
        out_shape=jax.ShapeDtypeStruct((t, d), BF16),
        compiler_params=_params(1),
        name="normmod",
    )(x, gain, mod)


def _mm_body(a_ref, w_ref, o_ref, wb_ref):
    @pl.when(pl.program_id(1) == 0)
    def _():
        wb_ref[...] = w_ref[...].astype(BF16)

    o_ref[...] = jnp.dot(a_ref[...], wb_ref[...], preferred_element_type=F32).astype(o_ref.dtype)


def _mm(a, w, l, ncols, out_dtype, m, tm, tn, name):
    k = a.shape[1]
    w_spec = pl.BlockSpec((None, k, tn), lambda j, i: (l, 0, j))
    return pl.pallas_call(
        _mm_body,
        grid=(ncols // tn, m // tm),
        in_specs=[pl.BlockSpec((tm, k), lambda j, i: (i, 0)), w_spec],
        out_specs=pl.BlockSpec((tm, tn), lambda j, i: (i, j)),
        out_shape=jax.ShapeDtypeStruct((m, ncols), out_dtype),
        scratch_shapes=[pltpu.VMEM((k, tn), BF16)],
        compiler_params=_params(2),
        name=name,
    )(a, w)


def _mmt_body(a_ref, w_ref, *rest, shift):
    if shift:
        tail_ref, o_ref, wb_ref = rest
    else:
        o_ref, wb_ref = rest
    tn = wb_ref.shape[0]

    @pl.when(pl.program_id(1) == 0)
    def _():
        if shift:
            wb_ref[0:tn - shift, :] = w_ref[shift:tn, :].astype(BF16)
            wb_ref[tn - shift:tn, :] = tail_ref[...].astype(BF16)
        else:
            wb_ref[...] = w_ref[...].astype(BF16)

    o_ref[...] = lax.dot_general(a_ref[...], wb_ref[...], NT_DIMS, preferred_element_type=F32).astype(o_ref.dtype)


def _mmt(a, wt, l, row_off, ncols, shift, tn, m, tm, name):
    k = a.shape[1]
    base = row_off // tn
    in_specs = [pl.BlockSpec((tm, k), lambda j, i: (i, 0)), pl.BlockSpec((None, tn, k), lambda j, i: (l, base + j, 0))]
    args = [a, wt]
    if shift:
        per = tn // shift
        in_specs.append(pl.BlockSpec((None, shift, k), lambda j, i: (l, (base + j + 1) * per, 0)))
        args.append(wt)
    return pl.pallas_call(
        functools.partial(_mmt_body, shift=shift),
        grid=(ncols // tn, m // tm),
        in_specs=in_specs,
        out_specs=pl.BlockSpec((tm, tn), lambda j, i: (i, j)),
        out_shape=jax.ShapeDtypeStruct((m, ncols), F32),
        scratch_shapes=[pltpu.VMEM((tn, k), BF16)],
        compiler_params=_params(2),
        name=name,
    )(*args)


def _seq_block(b, i, rev, nb, seq, ctx, tt):
    nctx, nlat = ctx // tt, seq // tt
    ic = (nctx - 1 - i) if rev else i
    il = (nlat - 1 - (i - nctx)) if rev else (i - nctx)
    return jnp.where(i < nctx, (nb * seq) // tt + b * nctx + ic, b * nlat + il)


def _seq_spec(cols, col_blk, rev, nb, seq, ctx, tt):
    return pl.BlockSpec((tt, cols), lambda b, i: (_seq_block(b, i, rev, nb, seq, ctx, tt), col_blk))


def _grid_view(arr):
    t, c = arr.shape
    return arr.reshape(t // GRID_W, GRID_W, c)


def _col_tokens(lat_ref, first, k, a, b):
    return jnp.concatenate([lat_ref[:, first + j, a:b] for j in range(k)], axis=0)


def _lat_step(i, rev, seq, ctx, tt):
    nctx, nlat = ctx // tt, seq // tt
    il = (nlat - 1 - (i - nctx)) if rev else (i - nctx)
    return jnp.clip(il, 0, nlat - 1)


def _col_lat_spec(width, col_blk, rev, seq, ctx, tt):
    rows = seq // GRID_W
    per = SUBLANES // (tt // rows)
    return pl.BlockSpec((rows, SUBLANES, width), lambda b, i: (b, _lat_step(i, rev, seq, ctx, tt) // per, col_blk))


def _first_col(i, rev, seq, ctx, tt):
    k = tt // (seq // GRID_W)
    return (_lat_step(i, rev, seq, ctx, tt) % (SUBLANES // k)) * k


def _ctx_spec(cols, col_blk, rev, nb, seq, ctx, tt):
    nctx = ctx // tt

    def index(b, i):
        ic = (nctx - 1 - i) if rev else i
        return (nb * seq) // tt + b * nctx + jnp.clip(ic, 0, nctx - 1), col_blk

    return pl.BlockSpec((tt, cols), index)


def _halo_spec(cols, col_blk, rev, side, nb, seq, ctx, tt, total):
    per = tt // SUBLANES
    last = total // SUBLANES - 1

    def index(b, i):
        blk = _seq_block(b, i, rev, nb, seq, ctx, tt) * per
        blk = blk - 1 if side < 0 else blk + per
        return jnp.clip(blk, 0, last), col_blk

    return pl.BlockSpec((SUBLANES, cols), index)


def _stream_pos(i, rev, seq, ctx, tt):
    nctx, nlat = ctx // tt, seq // tt
    is_ctx = i < nctx
    ii = jnp.where(is_ctx, (nctx - 1 - i) if rev else i, (nlat - 1 - (i - nctx)) if rev else (i - nctx))
    n = jnp.where(is_ctx, nctx, nlat)
    return ii == 0, ii == n - 1


def _scan_tile_real(a, u, carry, rev):
    row = lax.broadcasted_iota(jnp.int32, a.shape, 0)
    for s in (1, 2, 4):
        if rev:
            a_sh, u_sh = pltpu.roll(a, SUBLANES - s, 0), pltpu.roll(u, SUBLANES - s, 0)
            ok = row < SUBLANES - s
        else:
            a_sh, u_sh = pltpu.roll(a, s, 0), pltpu.roll(u, s, 0)
            ok = row >= s
        u = jnp.where(ok, a * u_sh + u, u)
        a = jnp.where(ok, a * a_sh, a)
    h = u + a * carry
    return h, (h[0:1] if rev else h[SUBLANES - 1:SUBLANES])


def _lru_body(xf_ref, pf_ref, nf_ref, xb_ref, pb_ref, nb_ref, cw_ref, cb_ref, wd_ref, bd_ref, sp_ref,
              hf_ref, hb_ref, a_s, u_s, carry_s, *, seq, ctx, tt, c):
    i = pl.program_id(1)

    @pl.when(i == 0)
    def _():
        carry_s[...] = jnp.zeros_like(carry_s)

    row = lax.broadcasted_iota(jnp.int32, (tt, c), 0)
    for d, (x_ref, p_ref, n_ref) in enumerate(((xf_ref, pf_ref, nf_ref), (xb_ref, pb_ref, nb_ref))):
        first, last = _stream_pos(i, d == 1, seq, ctx, tt)
        x = x_ref[...]
        prev = jnp.where(first, 0.0, p_ref[SUBLANES - 1:SUBLANES, :])
        nx1 = jnp.where(last, 0.0, n_ref[0:1, :])
        nx2 = jnp.where(last, 0.0, n_ref[1:2, :])
        xm1 = jnp.where(row == 0, prev, pltpu.roll(x, 1, 0))
        xp1 = jnp.where(row == tt - 1, nx1, pltpu.roll(x, tt - 1, 0))
        xp2 = jnp.where(row == tt - 1, nx2, jnp.where(row == tt - 2, nx1, pltpu.roll(x, tt - 2, 0)))
        xl = xm1 * cw_ref[0:1, :] + x * cw_ref[1:2, :] + xp1 * cw_ref[2:3, :] + xp2 * cw_ref[3:4, :] + cb_ref[...]
        z = jnp.dot(xl.astype(BF16), wd_ref[d], preferred_element_type=F32) + bd_ref[d]
        r = jax.nn.sigmoid(z[:, :c])
        gi = jax.nn.sigmoid(z[:, c:])
        log_a = -LRU_C * r * sp_ref[d]
        a = jnp.exp(log_a)
        a_s[d] = a
        u_s[d] = jnp.sqrt(-jnp.tanh(log_a) * (a * a + 1.0)) * gi * xl

    ntile = tt // SUBLANES

    def step(k, carry):
        cf, cb = carry
        rows_f = pl.ds(pl.multiple_of(k * SUBLANES, SUBLANES), SUBLANES)
        rows_b = pl.ds(pl.multiple_of((ntile - 1 - k) * SUBLANES, SUBLANES), SUBLANES)
        h, cf = _scan_tile_real(a_s[0, rows_f, :], u_s[0, rows_f, :], cf, False)
        hf_ref[rows_f, :] = h
        h, cb = _scan_tile_real(a_s[1, rows_b, :], u_s[1, rows_b, :], cb, True)
        hb_ref[rows_b, :] = h
        return cf, cb

    cf, cb = lax.fori_loop(0, ntile, step, (carry_s[0, 0:1, :], carry_s[1, 0:1, :]))
    carry_s[0, 0:1, :] = cf
    carry_s[1, 0:1, :] = cb


def _layer_spec(arr, l):
    return pl.BlockSpec((None,) + arr.shape[1:], lambda b, i: (l,) + (0,) * (arr.ndim - 1))


def _lru(pa, cw, cb, wd, bd, sp, l, nb, seq, ctx, tt):
    t = pa.shape[0]
    c = cw.shape[-1]
    geo = (nb, seq, ctx, tt)
    return pl.pallas_call(
        functools.partial(_lru_body, seq=seq, ctx=ctx, tt=tt, c=c),
        grid=(nb, (seq + ctx) // tt),
        in_specs=[
            _seq_spec(c, 0, False, *geo), _halo_spec(c, 0, False, -1, *geo, t), _halo_spec(c, 0, False, 1, *geo, t),
            _seq_spec(c, 0, True, *geo), _halo_spec(c, 0, True, -1, *geo, t), _halo_spec(c, 0, True, 1, *geo, t),
            *(_layer_spec(arr, l) for arr in (cw, cb, wd, bd, sp)),
        ],
        out_specs=[_seq_spec(c, 0, False, *geo), _seq_spec(c, 0, True, *geo)],
        out_shape=[jax.ShapeDtypeStruct((t, c), F32)] * 2,
        scratch_shapes=[pltpu.VMEM((2, tt, c), F32), pltpu.VMEM((2, tt, c), F32), pltpu.VMEM((2, SUBLANES, c), F32)],
        compiler_params=_params(2),
        name="rglru_scan",
    )(pa, pa, pa, pa, pa, pa, cw, cb, wd, bd, sp)


def _lru_out_body(ay_ref, hf_ref, hb_ref, o_ref):
    o_ref[...] = (jax.nn.gelu(ay_ref[...]) * (hf_ref[...] + hb_ref[...])).astype(o_ref.dtype)


def _lru_out(pa, hf, hb, tm):
    t, c = hf.shape
    return pl.pallas_call(
        _lru_out_body,
        grid=(t // tm,),
        in_specs=[pl.BlockSpec((tm, c), lambda i: (i, 1)), pl.BlockSpec((tm, c), lambda i: (i, 0)),
                  pl.BlockSpec((tm, c), lambda i: (i, 0))],
        out_specs=pl.BlockSpec((tm, c), lambda i: (i, 0)),
        out_shape=jax.ShapeDtypeStruct((t, c), BF16),
        compiler_params=_params(1),
        name="rglru_out",
    )(pa, hf, hb)


def _scan_tile_cplx(xr, xi, dbl_ref, pw_ref, d, lanes, cr, ci, rev):
    for n, s in enumerate((1, 2, 4)):
        ar = dbl_ref[d, 0, n, :, lanes]
        ai = dbl_ref[d, 1, n, :, lanes]
        shift = SUBLANES - s if rev else s
        sr, si = pltpu.roll(xr, shift, 0), pltpu.roll(xi, shift, 0)
        xr, xi = xr + ar * sr - ai * si, xi + ar * si + ai * sr
    pr = pw_ref[d, 0, :, lanes]
    pi = pw_ref[d, 1, :, lanes]
    hr = xr + pr * cr - pi * ci
    hi = xi + pr * ci + pi * cr
    sel = slice(0, 1) if rev else slice(SUBLANES - 1, SUBLANES)
    return hr, hi, hr[sel], hi[sel]


def _load_seq_block(dst_ref, d, lat_ref, ctx_ref, is_ctx, first, k):
    @pl.when(is_ctx)
    def _():
        dst_ref[d] = ctx_ref[...]

    @pl.when(jnp.logical_not(is_ctx))
    def _():
        dst_ref[d] = _col_tokens(lat_ref, first, k, 0, lat_ref.shape[2])


def _s5_body(ulf_ref, ucf_ref, ulb_ref, ucb_ref, bre_ref, bim_ref, cre_ref, cim_ref, dbl_ref, pw_ref, yf_ref, yb_ref,
             u_s, hr_s, hi_s, carry_s, *, tt, gw, sw, seq, ctx):
    i = pl.program_id(1)

    @pl.when(i == 0)
    def _():
        carry_s[...] = jnp.zeros_like(carry_s)

    for d, (lat_ref, ctx_ref) in enumerate(((ulf_ref, ucf_ref), (ulb_ref, ucb_ref))):
        _load_seq_block(u_s, d, lat_ref, ctx_ref, i < ctx // tt, _first_col(i, d == 1, seq, ctx, tt),
                        tt // (seq // GRID_W))

    ntile = tt // SUBLANES
    for ch in range(S5_CHUNKS):
        cols = slice(ch * gw, (ch + 1) * gw)
        lanes = slice(ch * sw, (ch + 1) * sw)
        for d in range(2):
            u = u_s[d, :, cols].astype(BF16)
            hr_s[d] = jnp.dot(u, bre_ref[d, ch], preferred_element_type=F32)
            hi_s[d] = jnp.dot(u, bim_ref[d, ch], preferred_element_type=F32)

        def step(k, carry):
            crf, cif, crb, cib = carry
            rows_f = pl.ds(pl.multiple_of(k * SUBLANES, SUBLANES), SUBLANES)
            rows_b = pl.ds(pl.multiple_of((ntile - 1 - k) * SUBLANES, SUBLANES), SUBLANES)
            hr, hi, crf, cif = _scan_tile_cplx(hr_s[0, rows_f, :], hi_s[0, rows_f, :], dbl_ref, pw_ref, 0, lanes,
                                               crf, cif, False)
            hr_s[0, rows_f, :] = hr
            hi_s[0, rows_f, :] = hi
            hr, hi, crb, cib = _scan_tile_cplx(hr_s[1, rows_b, :], hi_s[1, rows_b, :], dbl_ref, pw_ref, 1, lanes,
                                               crb, cib, True)
            hr_s[1, rows_b, :] = hr
            hi_s[1, rows_b, :] = hi
            return crf, cif, crb, cib

        init = tuple(carry_s[n, 0:1, lanes] for n in range(4))
        fin = lax.fori_loop(0, ntile, step, init)
        for n in range(4):
            carry_s[n, 0:1, lanes] = fin[n]
        for d, y_ref in enumerate((yf_ref, yb_ref)):
            y_ref[:, cols] = (jnp.dot(hr_s[d].astype(BF16), cre_ref[d, ch], preferred_element_type=F32)
                              - jnp.dot(hi_s[d].astype(BF16), cim_ref[d, ch], preferred_element_type=F32))


def _s5(u, bre, bim, cre, cim, dbl, pw, l, nb, seq, ctx, tt):
    t, c = u.shape
    gw = c // S5_CHUNKS
    sw = bre.shape[-1]
    geo = (nb, seq, ctx, tt)
    lat = lambda rev: _col_lat_spec(c, 0, rev, seq, ctx, tt)
    cx = lambda rev: _ctx_spec(c, 0, rev, *geo)
    uv = _grid_view(u)
    return pl.pallas_call(
        functools.partial(_s5_body, tt=tt, gw=gw, sw=sw, seq=seq, ctx=ctx),
        grid=(nb, (seq + ctx) // tt),
        in_specs=[lat(False), cx(False), lat(True), cx(True),
                  *(_layer_spec(arr, l) for arr in (bre, bim, cre, cim, dbl, pw))],
        out_specs=[_seq_spec(c, 0, False, *geo), _seq_spec(c, 0, True, *geo)],
        out_shape=[jax.ShapeDtypeStruct((t, c), F32)] * 2,
        scratch_shapes=[pltpu.VMEM((2, tt, c), F32), pltpu.VMEM((2, tt, sw), F32), pltpu.VMEM((2, tt, sw), F32),
                        pltpu.VMEM((4, SUBLANES, sw * S5_CHUNKS), F32)],
        compiler_params=_params(2),
        name="s5_scan",
    )(uv, u, uv, u, bre, bim, cre, cim, dbl, pw)


def _tile_specs(cols, col_blk, n_lat, seq, tm):
    rows = seq // GRID_W
    nlat, per = n_lat // tm, seq // tm

    def lat_index(i):
        il = jnp.minimum(i, nlat - 1)
        return il // per, il % per

    def lat_index3(i):
        b, blk = lat_index(i)
        return b, blk, col_blk

    return (pl.BlockSpec((rows, tm // rows, cols), lat_index3),
            pl.BlockSpec((tm, cols), lambda i: (jnp.maximum(i, nlat), col_blk)))


def _load_tile(dst_ref, lat_ref, ctx_ref, is_ctx):
    @pl.when(is_ctx)
    def _():
        dst_ref[...] = ctx_ref[...]

    @pl.when(jnp.logical_not(is_ctx))
    def _():
        dst_ref[...] = _col_tokens(lat_ref, 0, lat_ref.shape[1], 0, lat_ref.shape[2])


def _s5_glu_body(ul_ref, uc_ref, yf_ref, yb_ref, d_ref, w_ref, o_ref, wb_ref, u_s, *, c, nlat):
    @pl.when(pl.program_id(0) == 0)
    def _():
        wb_ref[...] = w_ref[...].astype(BF16)

    _load_tile(u_s, ul_ref, uc_ref, pl.program_id(0) >= nlat)
    y = jax.nn.gelu(u_s[...] * d_ref[...] + yf_ref[...] + yb_ref[...])
    z = jnp.dot(y.astype(BF16), wb_ref[...], preferred_element_type=F32)
    o_ref[...] = (z[:, :c] * jax.nn.sigmoid(z[:, c:])).astype(o_ref.dtype)


def _s5_glu(u, yf, yb, dskip, w_glu, l, n_lat, seq, tm):
    t, c = u.shape
    lat_spec, ctx_spec = _tile_specs(c, 0, n_lat, seq, tm)
    return pl.pallas_call(
        functools.partial(_s5_glu_body, c=c, nlat=n_lat // tm),
        grid=(t // tm,),
        in_specs=[lat_spec, ctx_spec] + [pl.BlockSpec((tm, c), lambda i: (i, 0))] * 2
        + [pl.BlockSpec((None, 1, c), lambda i: (l, 0, 0)), pl.BlockSpec((None, c, 2 * c), lambda i: (l, 0, 0))],
        out_specs=pl.BlockSpec((tm, c), lambda i: (i, 0)),
        out_shape=jax.ShapeDtypeStruct((t, c), BF16),
        scratch_shapes=[pltpu.VMEM((c, 2 * c), BF16), pltpu.VMEM((tm, c), F32)],
        compiler_params=_params(1),
        name="s5_glu",
    )(_grid_view(u), u, yf, yb, dskip, w_glu)


def _chunk_cumsum(x, rev, ch):
    row = lax.broadcasted_iota(jnp.int32, x.shape, 0)
    s = 1
    while s < ch:
        if rev:
            x = x + jnp.where(row < ch - s, pltpu.roll(x, ch - s, 0), 0.0)
        else:
            x = x + jnp.where(row >= s, pltpu.roll(x, s, 0), 0.0)
        s *= 2
    return x


def _chunk_scan(q_s, k_s, v_s, lf_s, st_s, o_refs, *, tt, ch, heads, dk, dv):
    nchunk = tt // ch
    tri_r = lax.broadcasted_iota(jnp.int32, (ch, ch), 0)
    tri_c = lax.broadcasted_iota(jnp.int32, (ch, ch), 1)

    def step(n, _):
        for d in range(2):
            rev = d == 1
            cc = (nchunk - 1 - n) if rev else n
            rows = pl.ds(pl.multiple_of(cc * ch, ch), ch)
            b = _chunk_cumsum(lf_s[d, rows, :], rev, ch)
            piv = ch // 2 if rev else ch // 2 - 1
            end = 0 if rev else ch - 1
            m = b[piv:piv + 1, :]
            bl = b[end:end + 1, :]
            qm = q_s[d, rows, :] * jnp.exp(b - m)
            km = k_s[d, rows, :] * jnp.exp(m - b)
            qg = (qm * jnp.exp(m)).astype(BF16)
            kg = (km * jnp.exp(bl - m)).astype(BF16)
            dec = jnp.exp(bl)
            qm = qm.astype(BF16)
            km = km.astype(BF16)
            v = v_s[d, rows, :].astype(BF16)
            keep = (tri_r <= tri_c) if rev else (tri_r >= tri_c)
            for h in range(heads):
                ks = slice(h * dk, (h + 1) * dk)
                vs = slice(h * dv, (h + 1) * dv)
                sc = lax.dot_general(qm[:, ks], km[:, ks], NT_DIMS, preferred_element_type=F32)
                sc = jnp.where(keep, sc, 0.0).astype(BF16)
                st = st_s[d, h]
                o = jnp.dot(sc, v[:, vs], preferred_element_type=F32)
                o = o + lax.dot_general(qg[:, ks], st.astype(BF16), NT_DIMS, preferred_element_type=F32)
                st_s[d, h] = st * dec[:, ks] + lax.dot_general(v[:, vs], kg[:, ks], TN_DIMS,
                                                               preferred_element_type=F32)
                o_refs[d][rows, vs] = o
        return 0

    lax.fori_loop(0, nchunk, step, 0)


def _gla_body(qf_ref, zf_ref, qb_ref, zb_ref, wa_ref, ba_ref, of_ref, ob_ref, q_s, k_s, v_s, lf_s, st_s,
              *, tt, ch, heads, dk, dv):
    @pl.when(pl.program_id(1) == 0)
    def _():
        st_s[...] = jnp.zeros_like(st_s)

    hk = heads * dk
    for d, (x_ref, z_ref) in enumerate(((qf_ref, zf_ref), (qb_ref, zb_ref))):
        q_s[d] = x_ref[:, :hk] * dk ** -0.5
        k_s[d] = x_ref[:, hk:2 * hk]
        v_s[d] = x_ref[:, 2 * hk:]
        la = jnp.dot(z_ref[...].astype(BF16), wa_ref[d], preferred_element_type=F32) + ba_ref[d]
        lf_s[d] = jax.nn.log_sigmoid(la) / GLA_TAU
    _chunk_scan(q_s, k_s, v_s, lf_s, st_s, (of_ref, ob_ref), tt=tt, ch=ch, heads=heads, dk=dk, dv=dv)


def _hgrn_body(*refs, tt, ch, heads, dk, dv, seq, ctx):
    lat_refs, ctx_refs = (refs[0:3], refs[6:9]), (refs[3:6], refs[9:12])
    lb_ref, of_ref, ob_ref, q_s, k_s, v_s, lf_s, st_s = refs[12:]
    i = pl.program_id(1)

    @pl.when(i == 0)
    def _():
        st_s[...] = jnp.zeros_like(st_s)

    lb = lb_ref[...]

    def prep(d, z, q, v):
        q_s[d] = jax.nn.silu(q) * dk ** -0.5
        v_s[d] = v
        k_s[d] = (1.0 - lb) * jax.nn.sigmoid(-z)
        lf_s[d] = jnp.log(lb + (1.0 - lb) * jax.nn.sigmoid(z))

    k_cols = tt // (seq // GRID_W)
    for d in range(2):
        @pl.when(i < ctx // tt)
        def _():
            prep(d, *(r[...] for r in ctx_refs[d]))

        @pl.when(i >= ctx // tt)
        def _():
            first = _first_col(i, d == 1, seq, ctx, tt)
            prep(d, *(_col_tokens(r, first, k_cols, 0, r.shape[2]) for r in lat_refs[d]))

    _chunk_scan(q_s, k_s, v_s, lf_s, st_s, (of_ref, ob_ref), tt=tt, ch=ch, heads=heads, dk=dk, dv=dv)


def _chunk_scratch(tt, heads, dk, dv):
    return [pltpu.VMEM((2, tt, heads * dk), F32), pltpu.VMEM((2, tt, heads * dk), F32),
            pltpu.VMEM((2, tt, heads * dv), F32), pltpu.VMEM((2, tt, heads * dk), F32),
            pltpu.VMEM((2, heads, dv, dk), F32)]


def _gla(pc, za, wa, ba, nb, seq, ctx, tt, ch, heads, dk, dv):
    t = pc.shape[0]
    geo = (nb, seq, ctx, tt)
    wide = 2 * heads * dk + heads * dv
    full = lambda shape: pl.BlockSpec(shape, lambda b, i: (0,) * len(shape))
    return pl.pallas_call(
        functools.partial(_gla_body, tt=tt, ch=ch, heads=heads, dk=dk, dv=dv),
        grid=(nb, (seq + ctx) // tt),
        in_specs=[_seq_spec(wide, 0, False, *geo), _seq_spec(za.shape[1], 0, False, *geo),
                  _seq_spec(wide, 0, True, *geo), _seq_spec(za.shape[1], 0, True, *geo),
                  full(wa.shape), full(ba.shape)],
        out_specs=[_seq_spec(heads * dv, 0, False, *geo), _seq_spec(heads * dv, 0, True, *geo)],
        out_shape=[jax.ShapeDtypeStruct((t, heads * dv), F32)] * 2,
        scratch_shapes=_chunk_scratch(tt, heads, dk, dv),
        compiler_params=_params(2),
        name="gla_scan",
    )(pc, za, pc, za, wa, ba)


def _hgrn(pd, lb, nb, seq, ctx, tt, ch, heads, dk, dv):
    t = pd.shape[0]
    geo = (nb, seq, ctx, tt)
    hk = heads * dk
    lat = lambda d: [_col_lat_spec(hk, blk, d == 1, seq, ctx, tt) for blk in (d, 2, 3)]
    cx = lambda d: [_ctx_spec(hk, blk, d == 1, *geo) for blk in (d, 2, 3)]
    pv = _grid_view(pd)
    return pl.pallas_call(
        functools.partial(_hgrn_body, tt=tt, ch=ch, heads=heads, dk=dk, dv=dv, seq=seq, ctx=ctx),
        grid=(nb, (seq + ctx) // tt),
        in_specs=lat(0) + cx(0) + lat(1) + cx(1) + [pl.BlockSpec(lb.shape, lambda b, i: (0, 0))],
        out_specs=[_seq_spec(heads * dv, 0, False, *geo), _seq_spec(heads * dv, 0, True, *geo)],
        out_shape=[jax.ShapeDtypeStruct((t, heads * dv), F32)] * 2,
        scratch_shapes=_chunk_scratch(tt, heads, dk, dv),
        compiler_params=_params(2),
        name="hgrn_scan",
    )(*([pv] * 3 + [pd] * 3) * 2, lb)


def _headnorm_body(of_ref, ob_ref, *rest, heads, dv, nlat):
    if nlat is None:
        g_ref, n_ref, o_ref = rest
        gate = g_ref[...]
    else:
        gl_ref, gc_ref, n_ref, o_ref, g_s = rest
        _load_tile(g_s, gl_ref, gc_ref, pl.program_id(0) >= nlat)
        gate = g_s[...]
    o = of_ref[...] + ob_ref[...]
    parts = [_rms(o[:, h * dv:(h + 1) * dv]) for h in range(heads)]
    y = jnp.concatenate(parts, axis=-1) * n_ref[...]
    o_ref[...] = (y * jax.nn.silu(gate)).astype(o_ref.dtype)


def _headnorm(of, ob, gate_arr, gate_blk, gain, l, heads, tm, col_major=None):
    t, c = of.shape
    row = pl.BlockSpec((tm, c), lambda i: (i, 0))
    if col_major is None:
        gate_specs, gate_args, scratch, nlat = [pl.BlockSpec((tm, c), lambda i: (i, gate_blk))], [gate_arr], [], None
    else:
        n_lat, seq = col_major
        gate_specs = list(_tile_specs(c, gate_blk, n_lat, seq, tm))
        gate_args, scratch, nlat = [_grid_view(gate_arr), gate_arr], [pltpu.VMEM((tm, c), F32)], n_lat // tm
    return pl.pallas_call(
        functools.partial(_headnorm_body, heads=heads, dv=c // heads, nlat=nlat),
        grid=(t // tm,),
        in_specs=[row, row] + gate_specs + [pl.BlockSpec((None, 1, c), lambda i: (l, 0, 0))],
        out_specs=row,
        out_shape=jax.ShapeDtypeStruct((t, c), BF16),
        scratch_shapes=scratch,
        compiler_params=_params(1),
        name="headnorm_gate",
    )(of, ob, *gate_args, gain)


def _merge_body(ya_ref, yb_ref, yc_ref, yd_ref, g0_ref, g1_ref, g2_ref, g3_ref, w_ref, o_ref, wb_ref):
    @pl.when(pl.program_id(1) == 0)
    def _():
        wb_ref[...] = w_ref[...].astype(BF16)

    acc = None
    for k, (y_ref, g_ref) in enumerate(((ya_ref, g0_ref), (yb_ref, g1_ref), (yc_ref, g2_ref), (yd_ref, g3_ref))):
        term = jax.nn.sigmoid(g_ref[...]) * jnp.dot(y_ref[...], wb_ref[k], preferred_element_type=F32)
        acc = term if acc is None else acc + term
    o_ref[...] = acc.astype(o_ref.dtype)


def _merge(ys, pg, w_branch, l, m, tm):
    c = ys[0].shape[1]
    d = w_branch.shape[3]
    tn = 512
    nj = d // tn
    y_spec = pl.BlockSpec((tm, c), lambda j, i: (i, 0))
    g_specs = [pl.BlockSpec((tm, tn), functools.partial(lambda j, i, k: (i, k * nj + j), k=k)) for k in range(N_BRANCH)]
    return pl.pallas_call(
        _merge_body,
        grid=(nj, m // tm),
        in_specs=[y_spec] * N_BRANCH + g_specs + [pl.BlockSpec((None, N_BRANCH, c, tn), lambda j, i: (l, 0, 0, j))],
        out_specs=pl.BlockSpec((tm, tn), lambda j, i: (i, j)),
        out_shape=jax.ShapeDtypeStruct((m, d), BF16),
        scratch_shapes=[pltpu.VMEM((N_BRANCH, c, tn), BF16)],
        compiler_params=_params(2),
        name="branch_merge",
    )(*ys, pg, pg, pg, pg, w_branch)


def _resid_body(x_ref, y_ref, g_ref, mod_ref, xo_ref, *f_refs, gate_i, gy, gf, shift_i, scale_i, y_transposed):
    y = y_ref[...].T if y_transposed else y_ref[...]
    xn = x_ref[...] + mod_ref[gate_i:gate_i + 1, :] * (_rms(y) * g_ref[gy:gy + 1, :])
    xo_ref[...] = xn
    if f_refs:
        f = _rms(xn) * g_ref[gf:gf + 1, :]
        f = f * (1.0 + mod_ref[scale_i:scale_i + 1, :]) + mod_ref[shift_i:shift_i + 1, :]
        f_refs[0][...] = f.astype(BF16)
        f_refs[1][...] = f.T.astype(BF16)


def _resid(x, y, gain, mod, l, m, seq, nb, tm, gate_i, gy, with_f, y_transposed):
    d = x.shape[1]
    row = pl.BlockSpec((tm, d), lambda i: (i, 0))
    col = pl.BlockSpec((d, tm), lambda i: (0, i))
    out_shape = [jax.ShapeDtypeStruct((m, d), F32)]
    out_specs = [row]
    if with_f:
        out_shape += [jax.ShapeDtypeStruct((m, d), BF16), jax.ShapeDtypeStruct((d, m), BF16)]
        out_specs += [row, col]
    return pl.pallas_call(
        functools.partial(_resid_body, gate_i=gate_i, gy=gy, gf=2, shift_i=3, scale_i=4, y_transposed=y_transposed),
        grid=(m // tm,),
        in_specs=[row, col if y_transposed else row, pl.BlockSpec((None, 4, d), lambda i: (l, 0, 0)),
                  pl.BlockSpec((None, N_MOD, d), lambda i: (_mod_row(i, tm, seq, nb), 0, 0))],
        out_specs=out_specs,
        out_shape=out_shape,
        compiler_params=_params(1),
        name="residual",
    )(x, y, gain, mod)


def _top_rows(x, vals_ref, idx_ref, want_rank):
    n_rows = x.shape[0]
    iota = lax.broadcasted_iota(jnp.int32, x.shape, 0)

    def step(r, carry):
        x = carry[0]
        mx = jnp.max(x, axis=0, keepdims=True)
        vals_ref[pl.ds(r, 1), :] = mx
        first = jnp.min(jnp.where(x == mx, iota, n_rows), axis=0, keepdims=True)
        idx_ref[pl.ds(r, 1), :] = first
        hit = iota == first
        x = jnp.where(hit, -jnp.inf, x)
        return (x, jnp.where(hit, jnp.asarray(r, F32), carry[1])) if want_rank else (x,)

    init = (x, jnp.full(x.shape, NO_RANK, F32)) if want_rank else (x,)
    return lax.fori_loop(0, PEER_TOPK, step, init)[-1]


def _top_rows_distinct(x, vals_ref, want_rank):
    def step(r, carry):
        x = carry[0]
        mx = jnp.max(x, axis=0, keepdims=True)
        vals_ref[pl.ds(r, 1), :] = mx
        hit = x == mx
        x = jnp.where(hit, -jnp.inf, x)
        return (x, jnp.where(hit, jnp.asarray(r, F32), carry[1])) if want_rank else (x,)

    init = (x, jnp.full(x.shape, NO_RANK, F32)) if want_rank else (x,)
    out = lax.fori_loop(0, PEER_TOPK, step, init)
    removed = jnp.sum((out[0] == -jnp.inf).astype(F32), axis=0, keepdims=True)
    return removed != float(PEER_TOPK), out[-1]


def _pair_rows(a, b):
    half = PEER_TOPK // 2
    return jnp.concatenate([a[0:1, :] + b] + [a[r:r + 1, :] + b[0:half, :] for r in range(1, half)]
                           + [a[half:, :] + b[0:1, :]], axis=0)


def _route_body(q_ref, keys_ref, e1_ref, cnt_ref, rank2_ref, e2_ref, va_s, vb_s, vc_s, ia_s, ib_s, *, dq):
    half = PEER_TOPK // 2
    iota = lax.broadcasted_iota(jnp.int32, (N_KEYS, q_ref.shape[0]), 0)
    for h in range(PEER_HEADS):
        scores = []
        for p in range(2):
            qh = q_ref[:, h * 2 * dq + p * dq:h * 2 * dq + (p + 1) * dq].astype(BF16)
            scores.append(lax.dot_general(keys_ref[h, p].astype(BF16), qh, NT_DIMS, preferred_element_type=F32))

        def emit(a, b, best, cnt, rank2):
            z = jnp.sum(jnp.exp(best - best[0:1, :]), axis=0, keepdims=True)
            e1_ref[h] = jnp.exp(scores[0] - a[0:1, :]) / z
            cnt_ref[h] = cnt
            rank2_ref[h] = rank2.astype(BF16)
            e2_ref[h] = jnp.exp(scores[1] - b[0:1, :]).astype(BF16)

        tied_a, _ = _top_rows_distinct(scores[0], va_s, False)
        tied_b, _ = _top_rows_distinct(scores[1], vb_s, False)
        a, b = va_s[...], vb_s[...]
        rank2 = jnp.full(scores[1].shape, NO_RANK, F32)
        for r in range(PEER_TOPK):
            rank2 = jnp.where(scores[1] == b[r:r + 1, :], float(r), rank2)
        cand = _pair_rows(a, b)
        tied_c, _ = _top_rows_distinct(cand, vc_s, False)
        best = vc_s[...]
        keep = (cand >= best[PEER_TOPK - 1:PEER_TOPK, :]).astype(F32)
        groups = [(0, PEER_TOPK)] + [(PEER_TOPK + half * (r - 1), PEER_TOPK + half * r) for r in range(1, half)] \
            + [(PEER_TOPK + half * (half - 1) + r, PEER_TOPK + half * (half - 1) + r + 1) for r in range(half)]
        cnt = jnp.zeros(scores[0].shape, F32)
        for i, (lo, hi) in enumerate(groups):
            n_i = jnp.sum(keep[lo:hi, :], axis=0, keepdims=True)
            cnt = jnp.where(scores[0] == a[i:i + 1, :], n_i, cnt)
        emit(a, b, best, cnt, rank2)
        tied = jnp.max((tied_a | tied_b | tied_c).astype(F32)) > 0.0

        @pl.when(tied)
        def _():
            _top_rows(scores[0], va_s, ia_s, False)
            rank2 = _top_rows(scores[1], vb_s, ib_s, True)
            a, b = va_s[...], vb_s[...]
            _top_rows(_pair_rows(a, b), vc_s, ib_s, False)
            row = ib_s[...]
            sel_i = jnp.where(row < PEER_TOPK, 0,
                              jnp.where(row < PEER_TOPK + half * (half - 1), (row - half) // half, row - half * half))
            key_a = ia_s[...]
            cnt = jnp.zeros(scores[0].shape, F32)
            for i in range(PEER_TOPK):
                n_i = jnp.sum((sel_i == i).astype(F32), axis=0, keepdims=True)
                cnt = jnp.where(iota == key_a[i:i + 1, :], n_i, cnt)
            emit(a, b, vc_s[...], cnt, rank2)


def _route(q, keys, l, tm):
    t = q.shape[0]
    dq = keys.shape[-1]
    spec = pl.BlockSpec((PEER_HEADS, N_KEYS, tm), lambda i: (0, 0, i))
    return pl.pallas_call(
        functools.partial(_route_body, dq=dq),
        grid=(t // tm,),
        in_specs=[pl.BlockSpec((tm, q.shape[1]), lambda i: (i, 0)),
                  pl.BlockSpec((None,) + keys.shape[1:], lambda i: (l, 0, 0, 0, 0))],
        out_specs=[spec] * 4,
        out_shape=[jax.ShapeDtypeStruct((PEER_HEADS, N_KEYS, t), dt) for dt in (F32, F32, BF16, BF16)],
        scratch_shapes=[pltpu.VMEM((PEER_TOPK, tm), F32)] * 3 + [pltpu.VMEM((PEER_TOPK, tm), jnp.int32)] * 2,
        compiler_params=_params(1),
        name="peer_route",
    )(q, keys)


def _peer_body(ft_ref, u_ref, vt_ref, e1_ref, cnt_ref, rank2_ref, e2_ref, o_ref, g_s, act_s, *, te, nj):
    j = pl.program_id(1)

    @pl.when(j == 0)
    def _():
        o_ref[...] = jnp.zeros_like(o_ref)
        act_s[...] = jnp.zeros_like(act_s)

    tile = jnp.maximum(j - 1, 0)
    tm = act_s.shape[1]
    zero = jnp.zeros((), BF16)
    for r in range(te // N_KEYS):
        i1 = tile * (te // N_KEYS) + r
        rows = slice(r * N_KEYS, (r + 1) * N_KEYS)
        e1_rows = [e1_ref[h, pl.ds(i1, 1), :].astype(BF16) for h in range(PEER_HEADS)]
        cnt_rows = [cnt_ref[h, pl.ds(i1, 1), :].astype(BF16) for h in range(PEER_HEADS)]
        for cb in range(tm // LANES):
            cs = slice(cb * LANES, (cb + 1) * LANES)
            w = None
            for h in range(PEER_HEADS):
                term = jnp.where(rank2_ref[h, :, cs] < cnt_rows[h][:, cs], e2_ref[h, :, cs] * e1_rows[h][:, cs], zero)
                w = term if w is None else w + term
            g_s[rows, cs] = w * act_s[rows, cs]

    o_ref[...] += jnp.dot(vt_ref[...], g_s[...], preferred_element_type=F32)
    act_s[...] = jax.nn.gelu(jnp.dot(u_ref[...], ft_ref[...], preferred_element_type=F32)).astype(BF16)


def _peer(ft, u, vt, l, e1, cnt, rank2, e2, tm, te):
    d, t = ft.shape
    nj = u.shape[1] // te
    once = pl.Buffered(1)
    big_spec = pl.BlockSpec((PEER_HEADS, N_KEYS, tm), lambda i, j: (0, 0, i), pipeline_mode=once)
    return pl.pallas_call(
        functools.partial(_peer_body, te=te, nj=nj),
        grid=(t // tm, nj + 1),
        in_specs=[pl.BlockSpec((d, tm), lambda i, j: (0, i), pipeline_mode=once),
                  pl.BlockSpec((None, te, d), lambda i, j: (l, jnp.minimum(j, nj - 1), 0)),
                  pl.BlockSpec((None, d, te), lambda i, j: (l, 0, jnp.maximum(j - 1, 0))),
                  big_spec, big_spec, big_spec, big_spec],
        out_specs=pl.BlockSpec((d, tm), lambda i, j: (0, i)),
        out_shape=jax.ShapeDtypeStruct((d, t), F32),
        scratch_shapes=[pltpu.VMEM((te, tm), BF16), pltpu.VMEM((te, tm), BF16)],
        compiler_params=_params(2),
        name="peer_experts",
    )(ft, u, vt, e1, cnt, rank2, e2)


def _block_diag(w):
    *lead, n, r, c = w.shape
    eye = jnp.eye(n, dtype=w.dtype)
    return (w[..., :, :, None, :] * eye[:, None, :, None]).reshape(*lead, n * r, n * c)


def _lru_params(w_a, b_a, w_i, b_i, lam):
    wd = jnp.concatenate([_block_diag(w_a.astype(BF16)), _block_diag(w_i.astype(BF16))], axis=-1)
    bd = jnp.concatenate([b_a, b_i], axis=-1)[..., None, :]
    sp = jax.nn.softplus(-lam)[..., None, :]
    return wd, bd, sp


def _s5_params(a_re, a_im, log_dt, b_re, b_im, c_re, c_im):
    lead = a_re.shape[:2]
    g, p = a_re.shape[2:]
    gc = g // S5_CHUNKS
    big_a = lax.complex(a_re, a_im)
    adt = big_a * jnp.exp(log_dt)[..., None]
    a_bar = jnp.exp(adt)
    b_bar = ((a_bar - 1.0) / big_a)[..., None] * lax.complex(b_re, b_im)
    c_mat = lax.complex(c_re, c_im)
    bm = jnp.swapaxes(b_bar, -1, -2).reshape(*lead, S5_CHUNKS, gc, S5_GROUP, p)
    cm = jnp.swapaxes(c_mat, -1, -2).reshape(*lead, S5_CHUNKS, gc, p, S5_GROUP)
    bre, bim, cre, cim = (_block_diag(part(m).astype(BF16)) for m in (bm, cm) for part in (jnp.real, jnp.imag))
    steps = jnp.arange(1, SUBLANES + 1, dtype=F32)[:, None, None]
    pows = jnp.exp(adt[:, :, None] * steps).reshape(*lead, SUBLANES, g * p)
    row = jnp.arange(SUBLANES)
    shifts = jnp.array([1, 2, 4])
    ok = jnp.stack([row[None, :] >= shifts[:, None], row[None, :] < SUBLANES - shifts[:, None]])
    dbl = jnp.where(ok[None, :, :, :, None], pows[:, :, shifts - 1][:, :, :, None, :], 0.0)
    pw = jnp.stack([pows[:, 0], pows[:, 1, ::-1]], axis=1)
    split = lambda z: jnp.stack([jnp.real(z), jnp.imag(z)], axis=2)
    return bre, bim, cre, cim, split(dbl), split(pw)


def _to_row_major(t, nb, seq):
    rows = seq // GRID_W
    lat = t[:nb * seq].reshape(nb, GRID_W, rows, -1).transpose(0, 2, 1, 3).reshape(nb * seq, -1)
    return jnp.concatenate([lat, t[nb * seq:]], axis=0)


def kernel(x, c, ctx, c_ctx, w_ada, b_ada, norm_gain, w_in, lru_conv_w, lru_conv_b, lru_w_a, lru_b_a, lru_w_i,
           lru_b_i, lru_lambda, s5_a_re, s5_a_im, s5_log_dt, s5_b_re, s5_b_im, s5_c_re, s5_c_im, s5_d, s5_w_glu,
           gla_w_alpha, gla_b_alpha, gla_norm, hgrn_lb_logits, hgrn_norm, w_branch, w_out, peer_w_q, peer_keys,
           peer_u, peer_v):
    nb, seq, d = x.shape
    nctx = ctx.shape[1]
    depth = w_ada.shape[0]
    mix = d // 4
    n_lat = nb * seq
    n_tok = n_lat + nb * nctx
    assert nb + 1 <= SUBLANES and seq % GRID_W == 0
    gla_dk, gla_dv = mix // 2 // GLA_HEADS, mix // GLA_HEADS
    hg_dk = hg_dv = mix // HGRN_HEADS
    gla_rank = gla_w_alpha.shape[2]
    tm = _tile(math.gcd(seq, nb * nctx), (512, 256, 128))
    tr = min(tm, 256)
    tw = min(tm, 256)
    tt = _tile(math.gcd(seq, nctx), (256, 128, 64, 32))
    ch = 64

    w_in_t = jnp.swapaxes(w_in, 1, 2)
    o_c = 3 * mix
    o_ca = o_c + 2 * GLA_HEADS * gla_dk + GLA_HEADS * gla_dv + mix
    o_dg = 2 * HGRN_HEADS * hg_dk + HGRN_HEADS * hg_dk + HGRN_HEADS * hg_dv
    o_gt = o_dg + mix
    za_w = 128

    p_lb = jax.nn.softmax(hgrn_lb_logits.astype(F32), axis=0)
    lower = jnp.cumsum(p_lb, axis=0) - p_lb[0]
    cvec = jnp.zeros((SUBLANES, d), F32).at[:nb].set(c).at[nb].set(c_ctx)
    b_ada3 = b_ada[:, None, :]
    u_tab = peer_u.astype(BF16)
    vt_tab = jnp.swapaxes(peer_v, 1, 2).astype(BF16)

    wd, bd, sp = _lru_params(lru_w_a, lru_b_a, lru_w_i, lru_b_i, lru_lambda)
    s5p = _s5_params(s5_a_re, s5_a_im, s5_log_dt, s5_b_re, s5_b_im, s5_c_re, s5_c_im)

    xs = jnp.concatenate([x.reshape(n_lat, d), ctx.reshape(nb * nctx, d)], axis=0)
    for l in range(depth):
        last = l == depth - 1
        m_out = n_lat if last else n_tok
        mod = _adaln(cvec, w_ada, b_ada3, l).reshape(SUBLANES, N_MOD, d)
        h = _normmod(xs, norm_gain, mod, l, seq, nb, tr)

        mm_in = lambda off, n, shift, tn, name: _mmt(h, w_in_t, l, off, n, shift, tn, n_tok, tw, name)
        pa = mm_in(0, 2 * mix, 0, mix, "proj_lru")
        pb = mm_in(2 * mix, mix, 0, mix, "proj_s5")
        pc = mm_in(o_c, o_ca - o_c - mix, 0, mix, "proj_gla")
        pcg = mm_in(o_ca - mix, mix, 0, mix, "proj_gla_gate")
        za = mm_in(o_ca, za_w, 0, za_w, "proj_gla_rank")
        pd = mm_in(o_ca, o_dg, gla_rank, mix, "proj_hgrn")
        pdg = mm_in(o_ca + o_dg, mix, gla_rank, mix, "proj_hgrn_gate")
        pg = mm_in(o_ca + o_gt, N_BRANCH * d, gla_rank, mix, "proj_gate")

        hf, hb = _lru(pa, lru_conv_w, lru_conv_b[:, None, :], wd, bd, sp, l, nb, seq, nctx, tt)
        ya = _lru_out(pa, hf, hb, tm)

        yf, ybk = _s5(pb, *s5p, l, nb, seq, nctx, tt)
        yb = _to_row_major(_s5_glu(pb, yf, ybk, s5_d[:, None, :], s5_w_glu, l, n_lat, seq, tm), nb, seq)

        wa = jnp.zeros((2, za_w, GLA_HEADS * gla_dk), F32).at[:, :gla_rank].set(gla_w_alpha[l]).astype(BF16)
        of, ob = _gla(pc, za, wa, gla_b_alpha[l][:, None, :], nb, seq, nctx, tt, min(2 * ch, tt), GLA_HEADS, gla_dk, gla_dv)
        yc = _headnorm(of, ob, pcg, 0, gla_norm[:, None, :], l, GLA_HEADS, tm)

        of, ob = _hgrn(pd, lower[l][None, :], nb, seq, nctx, tt, ch, HGRN_HEADS, hg_dk, hg_dv)
        yd = _to_row_major(_headnorm(of, ob, pdg, 0, hgrn_norm[:, None, :], l, HGRN_HEADS, tm, (n_lat, seq)), nb, seq)

        zm = _merge((ya, yb, yc, yd), pg, w_branch, l, m_out, tm)
        mo = _mm(zm, w_out, l, d, F32, m_out, tw, mix, "proj_out")
        xs, f, ft = _resid(xs, mo, norm_gain, mod, l, m_out, seq, nb, tr, 2, 1, True, False)

        q = _mm(f, peer_w_q, l, peer_w_q.shape[2], F32, m_out, tw, mix, "peer_query")
        e1, cnt, rank2, e2 = _route(q, peer_keys, l, _tile(m_out, (256, 128)))
        yt = _peer(ft, u_tab, vt_tab, l, e1, cnt, rank2, e2, _tile(m_out, (512, 256, 128)), 512)
        xs = _resid(xs, yt, norm_gain, mod, l, m_out, seq, nb, tr, 5, 3, False, True)[0]
    return xs[:n_lat].reshape(nb, seq, d)
```

```python
import functools
import math

import jax
import jax.numpy as jnp
from jax import lax
from jax.experimental import pallas as pl
from jax.experimental.pallas import tpu as pltpu

F32 = jnp.float32
BF16 = jnp.bfloat16
EPS = 1e-6
GRID_W = 64
N_MOD = 6
N_BRANCH = 4
LRU_BLOCKS = 16
LRU_C = 8.0
S5_GROUP = 16
S5_STATE = 64
S5_CHUNKS = 4
GLA_HEADS = 4
GLA_TAU = 16.0
HGRN_HEADS = 8
PEER_HEADS = 8
N_KEYS = 128
PEER_TOPK = 16
NO_RANK = 255.0
SUBLANES = 8
LANES = 128
VMEM_LIMIT = 56 * 1024 * 1024

NT_DIMS = (((1,), (1,)), ((), ()))
TN_DIMS = (((0,), (0,)), ((), ()))


def _params(n_axes, vmem=VMEM_LIMIT):
    return pltpu.CompilerParams(dimension_semantics=("arbitrary",) * n_axes, vmem_limit_bytes=vmem)


def _tile(n, prefs):
    for p in prefs:
        if n % p == 0:
            return p
    raise ValueError(f"no tile for {n} in {prefs}")


def _rms(x):
    return x * lax.rsqrt(jnp.mean(x * x, axis=-1, keepdims=True) + EPS)


def _ada_body(c_ref, w_ref, b_ref, o_ref):
    c = c_ref[...]
    s = (c * jax.nn.sigmoid(c)).astype(BF16)
    o_ref[...] = jnp.dot(s, w_ref[...].astype(BF16), preferred_element_type=F32) + b_ref[...]


def _adaln(cvec, w_ada, b_ada3, l):
    rows, d = cvec.shape
    n = w_ada.shape[2]
    tn = 512
    return pl.pallas_call(
        _ada_body,
        grid=(n // tn,),
        in_specs=[
            pl.BlockSpec((rows, d), lambda j: (0, 0)),
            pl.BlockSpec((None, d, tn), lambda j: (l, 0, j)),
            pl.BlockSpec((None, 1, tn), lambda j: (l, 0, j)),
        ],
        out_specs=pl.BlockSpec((rows, tn), lambda j: (0, j)),
        out_shape=jax.ShapeDtypeStruct((rows, n), F32),
        compiler_params=_params(1),
        name="adaln",
    )(cvec, w_ada, b_ada3)


def _normmod_body(x_ref, g_ref, mod_ref, o_ref, *, gi, shift_i, scale_i):
    y = _rms(x_ref[...]) * g_ref[gi:gi + 1, :]
    o_ref[...] = (y * (1.0 + mod_ref[scale_i:scale_i + 1, :]) + mod_ref[shift_i:shift_i + 1, :]).astype(o_ref.dtype)


def _mod_row(i, tm, seq, nb):
    return jnp.minimum((i * tm) // seq, nb)


def _normmod(x, gain, mod, l, seq, nb, tm):
    t, d = x.shape
    return pl.pallas_call(
        functools.partial(_normmod_body, gi=0, shift_i=0, scale_i=1),
        grid=(t // tm,),
        in_specs=[
            pl.BlockSpec((tm, d), lambda i: (i, 0)),
            pl.BlockSpec((None, 4, d), lambda i: (l, 0, 0)),
            pl.BlockSpec((None, N_MOD, d), lambda i: (_mod_row(i, tm, seq, nb), 0, 0)),
        ],
        out_specs=pl.BlockSpec((tm, d), lambda i: (i, 0)),
        out_shape=jax.ShapeDtypeStruct((t, d), BF16),
        compiler_params=_params(1),
        name="normmod",
    )(x, gain, mod)


def _mm_body(a_ref, w_ref, o_ref, wb_ref):
    @pl.when(pl.program_id(1) == 0)
    def _():
        wb_ref[...] = w_ref[...].astype(BF16)

    o_ref[...] = jnp.dot(a_ref[...], wb_ref[...], preferred_element_type=F32).astype(o_ref.dtype)


def _mm(a, w, l, ncols, out_dtype, m, tm, tn, name):
    k = a.shape[1]
    w_spec = pl.BlockSpec((None, k, tn), lambda j, i: (l, 0, j))
    return pl.pallas_call(
        _mm_body,
        grid=(ncols // tn, m // tm),
        in_specs=[pl.BlockSpec((tm, k), lambda j, i: (i, 0)), w_spec],
        out_specs=pl.BlockSpec((tm, tn), lambda j, i: (i, j)),
        out_shape=jax.ShapeDtypeStruct((m, ncols), out_dtype),
        scratch_shapes=[pltpu.VMEM((k, tn), BF16)],
        compiler_params=_params(2),
        name=name,
    )(a, w)


def _mmt_body(a_ref, w_ref, *rest, shift):
    if shift:
        tail_ref, o_ref, wb_ref = rest
    else:
        o_ref, wb_ref = rest
    tn = wb_ref.shape[0]

    @pl.when(pl.program_id(1) == 0)
    def _():
        if shift:
            wb_ref[0:tn - shift, :] = w_ref[shift:tn, :].astype(BF16)
            wb_ref[tn - shift:tn, :] = tail_ref[...].astype(BF16)
        else:
            wb_ref[...] = w_ref[...].astype(BF16)

    o_ref[...] = lax.dot_general(a_ref[...], wb_ref[...], NT_DIMS, preferred_element_type=F32).astype(o_ref.dtype)


def _mmt(a, wt, l, row_off, ncols, shift, tn, m, tm, name):
    k = a.shape[1]
    base = row_off // tn
    in_specs = [pl.BlockSpec((tm, k), lambda j, i: (i, 0)), pl.BlockSpec((None, tn, k), lambda j, i: (l, base + j, 0))]
    args = [a, wt]
    if shift:
        per = tn // shift
        in_specs.append(pl.BlockSpec((None, shift, k), lambda j, i: (l, (base + j + 1) * per, 0)))
        args.append(wt)
    return pl.pallas_call(
        functools.partial(_mmt_body, shift=shift),
        grid=(ncols // tn, m // tm),
        in_specs=in_specs,
        out_specs=pl.BlockSpec((tm, tn), lambda j, i: (i, j)),
        out_shape=jax.ShapeDtypeStruct((m, ncols), F32),
        scratch_shapes=[pltpu.VMEM((tn, k), BF16)],
        compiler_params=_params(2),
        name=name,
    )(*args)


def _seq_block(b, i, rev, nb, seq, ctx, tt):
    nctx, nlat = ctx // tt, seq // tt
    ic = (nctx - 1 - i) if rev else i
    il = (nlat - 1 - (i - nctx)) if rev else (i - nctx)
    return jnp.where(i < nctx, (nb * seq) // tt + b * nctx + ic, b * nlat + il)


def _seq_spec(cols, col_blk, rev, nb, seq, ctx, tt):
    return pl.BlockSpec((tt, cols), lambda b, i: (_seq_block(b, i, rev, nb, seq, ctx, tt), col_blk))


def _grid_view(arr):
    t, c = arr.shape
    return arr.reshape(t // GRID_W, GRID_W, c)


def _col_tokens(lat_ref, first, k, a, b):
    return jnp.concatenate([lat_ref[:, first + j, a:b] for j in range(k)], axis=0)


def _lat_step(i, rev, seq, ctx, tt):
    nctx, nlat = ctx // tt, seq // tt
    il = (nlat - 1 - (i - nctx)) if rev else (i - nctx)
    return jnp.clip(il, 0, nlat - 1)


def _col_lat_spec(width, col_blk, rev, seq, ctx, tt):
    rows = seq // GRID_W
    per = SUBLANES // (tt // rows)
    return pl.BlockSpec((rows, SUBLANES, width), lambda b, i: (b, _lat_step(i, rev, seq, ctx, tt) // per, col_blk))


def _first_col(i, rev, seq, ctx, tt):
    k = tt // (seq // GRID_W)
    return (_lat_step(i, rev, seq, ctx, tt) % (SUBLANES // k)) * k


def _ctx_spec(cols, col_blk, rev, nb, seq, ctx, tt):
    nctx = ctx // tt

    def index(b, i):
        ic = (nctx - 1 - i) if rev else i
        return (nb * seq) // tt + b * nctx + jnp.clip(ic, 0, nctx - 1), col_blk

    return pl.BlockSpec((tt, cols), index)


def _halo_spec(cols, col_blk, rev, side, nb, seq, ctx, tt, total):
    per = tt // SUBLANES
    last = total // SUBLANES - 1

    def index(b, i):
        blk = _seq_block(b, i, rev, nb, seq, ctx, tt) * per
        blk = blk - 1 if side < 0 else blk + per
        return jnp.clip(blk, 0, last), col_blk

    return pl.BlockSpec((SUBLANES, cols), index)


def _stream_pos(i, rev, seq, ctx, tt):
    nctx, nlat = ctx // tt, seq // tt
    is_ctx = i < nctx
    ii = jnp.where(is_ctx, (nctx - 1 - i) if rev else i, (nlat - 1 - (i - nctx)) if rev else (i - nctx))
    n = jnp.where(is_ctx, nctx, nlat)
    return ii == 0, ii == n - 1


def _scan_tile_real(a, u, carry, rev):
    row = lax.broadcasted_iota(jnp.int32, a.shape, 0)
    for s in (1, 2, 4):
        if rev:
            a_sh, u_sh = pltpu.roll(a, SUBLANES - s, 0), pltpu.roll(u, SUBLANES - s, 0)
            ok = row < SUBLANES - s
        else:
            a_sh, u_sh = pltpu.roll(a, s, 0), pltpu.roll(u, s, 0)
            ok = row >= s
        u = jnp.where(ok, a * u_sh + u, u)
        a = jnp.where(ok, a * a_sh, a)
    h = u + a * carry
    return h, (h[0:1] if rev else h[SUBLANES - 1:SUBLANES])


def _lru_body(xf_ref, pf_ref, nf_ref, xb_ref, pb_ref, nb_ref, cw_ref, cb_ref, wd_ref, bd_ref, sp_ref,
              hf_ref, hb_ref, a_s, u_s, carry_s, *, seq, ctx, tt, c):
    i = pl.program_id(1)

    @pl.when(i == 0)
    def _():
        carry_s[...] = jnp.zeros_like(carry_s)

    row = lax.broadcasted_iota(jnp.int32, (tt, c), 0)
    for d, (x_ref, p_ref, n_ref) in enumerate(((xf_ref, pf_ref, nf_ref), (xb_ref, pb_ref, nb_ref))):
        first, last = _stream_pos(i, d == 1, seq, ctx, tt)
        x = x_ref[...]
        prev = jnp.where(first, 0.0, p_ref[SUBLANES - 1:SUBLANES, :])
        nx1 = jnp.where(last, 0.0, n_ref[0:1, :])
        nx2 = jnp.where(last, 0.0, n_ref[1:2, :])
        xm1 = jnp.where(row == 0, prev, pltpu.roll(x, 1, 0))
        xp1 = jnp.where(row == tt - 1, nx1, pltpu.roll(x, tt - 1, 0))
        xp2 = jnp.where(row == tt - 1, nx2, jnp.where(row == tt - 2, nx1, pltpu.roll(x, tt - 2, 0)))
        xl = xm1 * cw_ref[0:1, :] + x * cw_ref[1:2, :] + xp1 * cw_ref[2:3, :] + xp2 * cw_ref[3:4, :] + cb_ref[...]
        z = jnp.dot(xl.astype(BF16), wd_ref[d], preferred_element_type=F32) + bd_ref[d]
        r = jax.nn.sigmoid(z[:, :c])
        gi = jax.nn.sigmoid(z[:, c:])
        log_a = -LRU_C * r * sp_ref[d]
        a = jnp.exp(log_a)
        a_s[d] = a
        u_s[d] = jnp.sqrt(-jnp.tanh(log_a) * (a * a + 1.0)) * gi * xl

    ntile = tt // SUBLANES

    def step(k, carry):
        cf, cb = carry
        rows_f = pl.ds(pl.multiple_of(k * SUBLANES, SUBLANES), SUBLANES)
        rows_b = pl.ds(pl.multiple_of((ntile - 1 - k) * SUBLANES, SUBLANES), SUBLANES)
        h, cf = _scan_tile_real(a_s[0, rows_f, :], u_s[0, rows_f, :], cf, False)
        hf_ref[rows_f, :] = h
        h, cb = _scan_tile_real(a_s[1, rows_b, :], u_s[1, rows_b, :], cb, True)
        hb_ref[rows_b, :] = h
        return cf, cb

    cf, cb = lax.fori_loop(0, ntile, step, (carry_s[0, 0:1, :], carry_s[1, 0:1, :]))
    carry_s[0, 0:1, :] = cf
    carry_s[1, 0:1, :] = cb


def _layer_spec(arr, l):
    return pl.BlockSpec((None,) + arr.shape[1:], lambda b, i: (l,) + (0,) * (arr.ndim - 1))


def _lru(pa, cw, cb, wd, bd, sp, l, nb, seq, ctx, tt):
    t = pa.shape[0]
    c = cw.shape[-1]
    geo = (nb, seq, ctx, tt)
    return pl.pallas_call(
        functools.partial(_lru_body, seq=seq, ctx=ctx, tt=tt, c=c),
        grid=(nb, (seq + ctx) // tt),
        in_specs=[
            _seq_spec(c, 0, False, *geo), _halo_spec(c, 0, False, -1, *geo, t), _halo_spec(c, 0, False, 1, *geo, t),
            _seq_spec(c, 0, True, *geo), _halo_spec(c, 0, True, -1, *geo, t), _halo_spec(c, 0, True, 1, *geo, t),
            *(_layer_spec(arr, l) for arr in (cw, cb, wd, bd, sp)),
        ],
        out_specs=[_seq_spec(c, 0, False, *geo), _seq_spec(c, 0, True, *geo)],
        out_shape=[jax.ShapeDtypeStruct((t, c), F32)] * 2,
        scratch_shapes=[pltpu.VMEM((2, tt, c), F32), pltpu.VMEM((2, tt, c), F32), pltpu.VMEM((2, SUBLANES, c), F32)],
        compiler_params=_params(2),
        name="rglru_scan",
    )(pa, pa, pa, pa, pa, pa, cw, cb, wd, bd, sp)


def _lru_out_body(ay_ref, hf_ref, hb_ref, o_ref):
    o_ref[...] = (jax.nn.gelu(ay_ref[...]) * (hf_ref[...] + hb_ref[...])).astype(o_ref.dtype)


def _lru_out(pa, hf, hb, tm):
    t, c = hf.shape
    return pl.pallas_call(
        _lru_out_body,
        grid=(t // tm,),
        in_specs=[pl.BlockSpec((tm, c), lambda i: (i, 1)), pl.BlockSpec((tm, c), lambda i: (i, 0)),
                  pl.BlockSpec((tm, c), lambda i: (i, 0))],
        out_specs=pl.BlockSpec((tm, c), lambda i: (i, 0)),
        out_shape=jax.ShapeDtypeStruct((t, c), BF16),
        compiler_params=_params(1),
        name="rglru_out",
    )(pa, hf, hb)


def _scan_tile_cplx(xr, xi, dbl_ref, pw_ref, d, lanes, cr, ci, rev):
    for n, s in enumerate((1, 2, 4)):
        ar = dbl_ref[d, 0, n, :, lanes]
        ai = dbl_ref[d, 1, n, :, lanes]
        shift = SUBLANES - s if rev else s
        sr, si = pltpu.roll(xr, shift, 0), pltpu.roll(xi, shift, 0)
        xr, xi = xr + ar * sr - ai * si, xi + ar * si + ai * sr
    pr = pw_ref[d, 0, :, lanes]
    pi = pw_ref[d, 1, :, lanes]
    hr = xr + pr * cr - pi * ci
    hi = xi + pr * ci + pi * cr
    sel = slice(0, 1) if rev else slice(SUBLANES - 1, SUBLANES)
    return hr, hi, hr[sel], hi[sel]


def _load_seq_block(dst_ref, d, lat_ref, ctx_ref, is_ctx, first, k):
    @pl.when(is_ctx)
    def _():
        dst_ref[d] = ctx_ref[...]

    @pl.when(jnp.logical_not(is_ctx))
    def _():
        dst_ref[d] = _col_tokens(lat_ref, first, k, 0, lat_ref.shape[2])


def _s5_body(ulf_ref, ucf_ref, ulb_ref, ucb_ref, bre_ref, bim_ref, cre_ref, cim_ref, dbl_ref, pw_ref, yf_ref, yb_ref,
             u_s, hr_s, hi_s, carry_s, *, tt, gw, sw, seq, ctx):
    i = pl.program_id(1)

    @pl.when(i == 0)
    def _():
        carry_s[...] = jnp.zeros_like(carry_s)

    for d, (lat_ref, ctx_ref) in enumerate(((ulf_ref, ucf_ref), (ulb_ref, ucb_ref))):
        _load_seq_block(u_s, d, lat_ref, ctx_ref, i < ctx // tt, _first_col(i, d == 1, seq, ctx, tt),
                        tt // (seq // GRID_W))

    ntile = tt // SUBLANES
    for ch in range(S5_CHUNKS):
        cols = slice(ch * gw, (ch + 1) * gw)
        lanes = slice(ch * sw, (ch + 1) * sw)
        for d in range(2):
            u = u_s[d, :, cols].astype(BF16)
            hr_s[d] = jnp.dot(u, bre_ref[d, ch], preferred_element_type=F32)
            hi_s[d] = jnp.dot(u, bim_ref[d, ch], preferred_element_type=F32)

        def step(k, carry):
            crf, cif, crb, cib = carry
            rows_f = pl.ds(pl.multiple_of(k * SUBLANES, SUBLANES), SUBLANES)
            rows_b = pl.ds(pl.multiple_of((ntile - 1 - k) * SUBLANES, SUBLANES), SUBLANES)
            hr, hi, crf, cif = _scan_tile_cplx(hr_s[0, rows_f, :], hi_s[0, rows_f, :], dbl_ref, pw_ref, 0, lanes,
                                               crf, cif, False)
            hr_s[0, rows_f, :] = hr
            hi_s[0, rows_f, :] = hi
            hr, hi, crb, cib = _scan_tile_cplx(hr_s[1, rows_b, :], hi_s[1, rows_b, :], dbl_ref, pw_ref, 1, lanes,
                                               crb, cib, True)
            hr_s[1, rows_b, :] = hr
            hi_s[1, rows_b, :] = hi
            return crf, cif, crb, cib

        init = tuple(carry_s[n, 0:1, lanes] for n in range(4))
        fin = lax.fori_loop(0, ntile, step, init)
        for n in range(4):
            carry_s[n, 0:1, lanes] = fin[n]
        for d, y_ref in enumerate((yf_ref, yb_ref)):
            y_ref[:, cols] = (jnp.dot(hr_s[d].astype(BF16), cre_ref[d, ch], preferred_element_type=F32)
                              - jnp.dot(hi_s[d].astype(BF16), cim_ref[d, ch], preferred_element_type=F32))


def _s5(u, bre, bim, cre, cim, dbl, pw, l, nb, seq, ctx, tt):
    t, c = u.shape
    gw = c // S5_CHUNKS
    sw = bre.shape[-1]
    geo = (nb, seq, ctx, tt)
    lat = lambda rev: _col_lat_spec(c, 0, rev, seq, ctx, tt)
    cx = lambda rev: _ctx_spec(c, 0, rev, *geo)
    uv = _grid_view(u)
    return pl.pallas_call(
        functools.partial(_s5_body, tt=tt, gw=gw, sw=sw, seq=seq, ctx=ctx),
        grid=(nb, (seq + ctx) // tt),
        in_specs=[lat(False), cx(False), lat(True), cx(True),
                  *(_layer_spec(arr, l) for arr in (bre, bim, cre, cim, dbl, pw))],
        out_specs=[_seq_spec(c, 0, False, *geo), _seq_spec(c, 0, True, *geo)],
        out_shape=[jax.ShapeDtypeStruct((t, c), F32)] * 2,
        scratch_shapes=[pltpu.VMEM((2, tt, c), F32), pltpu.VMEM((2, tt, sw), F32), pltpu.VMEM((2, tt, sw), F32),
                        pltpu.VMEM((4, SUBLANES, sw * S5_CHUNKS), F32)],
        compiler_params=_params(2),
        name="s5_scan",
    )(uv, u, uv, u, bre, bim, cre, cim, dbl, pw)


def _tile_specs(cols, col_blk, n_lat, seq, tm):
    rows = seq // GRID_W
    nlat, per = n_lat // tm, seq // tm

    def lat_index(i):
        il = jnp.minimum(i, nlat - 1)
        return il // per, il % per

    def lat_index3(i):
        b, blk = lat_index(i)
        return b, blk, col_blk

    return (pl.BlockSpec((rows, tm // rows, cols), lat_index3),
            pl.BlockSpec((tm, cols), lambda i: (jnp.maximum(i, nlat), col_blk)))


def _load_tile(dst_ref, lat_ref, ctx_ref, is_ctx):
    @pl.when(is_ctx)
    def _():
        dst_ref[...] = ctx_ref[...]

    @pl.when(jnp.logical_not(is_ctx))
    def _():
        dst_ref[...] = _col_tokens(lat_ref, 0, lat_ref.shape[1], 0, lat_ref.shape[2])


def _s5_glu_body(ul_ref, uc_ref, yf_ref, yb_ref, d_ref, w_ref, o_ref, wb_ref, u_s, *, c, nlat):
    @pl.when(pl.program_id(0) == 0)
    def _():
        wb_ref[...] = w_ref[...].astype(BF16)

    _load_tile(u_s, ul_ref, uc_ref, pl.program_id(0) >= nlat)
    y = jax.nn.gelu(u_s[...] * d_ref[...] + yf_ref[...] + yb_ref[...])
    z = jnp.dot(y.astype(BF16), wb_ref[...], preferred_element_type=F32)
    o_ref[...] = (z[:, :c] * jax.nn.sigmoid(z[:, c:])).astype(o_ref.dtype)


def _s5_glu(u, yf, yb, dskip, w_glu, l, n_lat, seq, tm):
    t, c = u.shape
    lat_spec, ctx_spec = _tile_specs(c, 0, n_lat, seq, tm)
    return pl.pallas_call(
        functools.partial(_s5_glu_body, c=c, nlat=n_lat // tm),
        grid=(t // tm,),
        in_specs=[lat_spec, ctx_spec] + [pl.BlockSpec((tm, c), lambda i: (i, 0))] * 2
        + [pl.BlockSpec((None, 1, c), lambda i: (l, 0, 0)), pl.BlockSpec((None, c, 2 * c), lambda i: (l, 0, 0))],
        out_specs=pl.BlockSpec((tm, c), lambda i: (i, 0)),
        out_shape=jax.ShapeDtypeStruct((t, c), BF16),
        scratch_shapes=[pltpu.VMEM((c, 2 * c), BF16), pltpu.VMEM((tm, c), F32)],
        compiler_params=_params(1),
        name="s5_glu",
    )(_grid_view(u), u, yf, yb, dskip, w_glu)


def _chunk_cumsum(x, rev, ch):
    row = lax.broadcasted_iota(jnp.int32, x.shape, 0)
    s = 1
    while s < ch:
        if rev:
            x = x + jnp.where(row < ch - s, pltpu.roll(x, ch - s, 0), 0.0)
        else:
            x = x + jnp.where(row >= s, pltpu.roll(x, s, 0), 0.0)
        s *= 2
    return x


def _chunk_scan(q_s, k_s, v_s, lf_s, st_s, o_refs, *, tt, ch, heads, dk, dv):
    nchunk = tt // ch
    tri_r = lax.broadcasted_iota(jnp.int32, (ch, ch), 0)
    tri_c = lax.broadcasted_iota(jnp.int32, (ch, ch), 1)

    def step(n, _):
        for d in range(2):
            rev = d == 1
            cc = (nchunk - 1 - n) if rev else n
            rows = pl.ds(pl.multiple_of(cc * ch, ch), ch)
            b = _chunk_cumsum(lf_s[d, rows, :], rev, ch)
            piv = ch // 2 if rev else ch // 2 - 1
            end = 0 if rev else ch - 1
            m = b[piv:piv + 1, :]
            bl = b[end:end + 1, :]
            qm = q_s[d, rows, :] * jnp.exp(b - m)
            km = k_s[d, rows, :] * jnp.exp(m - b)
            qg = (qm * jnp.exp(m)).astype(BF16)
            kg = (km * jnp.exp(bl - m)).astype(BF16)
            dec = jnp.exp(bl)
            qm = qm.astype(BF16)
            km = km.astype(BF16)
            v = v_s[d, rows, :].astype(BF16)
            keep = (tri_r <= tri_c) if rev else (tri_r >= tri_c)
            for h in range(heads):
                ks = slice(h * dk, (h + 1) * dk)
                vs = slice(h * dv, (h + 1) * dv)
                sc = lax.dot_general(qm[:, ks], km[:, ks], NT_DIMS, preferred_element_type=F32)
                sc = jnp.where(keep, sc, 0.0).astype(BF16)
                st = st_s[d, h]
                o = jnp.dot(sc, v[:, vs], preferred_element_type=F32)
                o = o + lax.dot_general(qg[:, ks], st.astype(BF16), NT_DIMS, preferred_element_type=F32)
                st_s[d, h] = st * dec[:, ks] + lax.dot_general(v[:, vs], kg[:, ks], TN_DIMS,
                                                               preferred_element_type=F32)
                o_refs[d][rows, vs] = o
        return 0

    lax.fori_loop(0, nchunk, step, 0)


def _gla_body(qf_ref, zf_ref, qb_ref, zb_ref, wa_ref, ba_ref, of_ref, ob_ref, q_s, k_s, v_s, lf_s, st_s,
              *, tt, ch, heads, dk, dv):
    @pl.when(pl.program_id(1) == 0)
    def _():
        st_s[...] = jnp.zeros_like(st_s)

    hk = heads * dk
    for d, (x_ref, z_ref) in enumerate(((qf_ref, zf_ref), (qb_ref, zb_ref))):
        q_s[d] = x_ref[:, :hk] * dk ** -0.5
        k_s[d] = x_ref[:, hk:2 * hk]
        v_s[d] = x_ref[:, 2 * hk:]
        la = jnp.dot(z_ref[...].astype(BF16), wa_ref[d], preferred_element_type=F32) + ba_ref[d]
        lf_s[d] = jax.nn.log_sigmoid(la) / GLA_TAU
    _chunk_scan(q_s, k_s, v_s, lf_s, st_s, (of_ref, ob_ref), tt=tt, ch=ch, heads=heads, dk=dk, dv=dv)


def _hgrn_body(*refs, tt, ch, heads, dk, dv, seq, ctx):
    lat_refs, ctx_refs = (refs[0:3], refs[6:9]), (refs[3:6], refs[9:12])
    lb_ref, of_ref, ob_ref, q_s, k_s, v_s, lf_s, st_s = refs[12:]
    i = pl.program_id(1)

    @pl.when(i == 0)
    def _():
        st_s[...] = jnp.zeros_like(st_s)

    lb = lb_ref[...]

    def prep(d, z, q, v):
        q_s[d] = jax.nn.silu(q) * dk ** -0.5
        v_s[d] = v
        k_s[d] = (1.0 - lb) * jax.nn.sigmoid(-z)
        lf_s[d] = jnp.log(lb + (1.0 - lb) * jax.nn.sigmoid(z))

    k_cols = tt // (seq // GRID_W)
    for d in range(2):
        @pl.when(i < ctx // tt)
        def _():
            prep(d, *(r[...] for r in ctx_refs[d]))

        @pl.when(i >= ctx // tt)
        def _():
            first = _first_col(i, d == 1, seq, ctx, tt)
            prep(d, *(_col_tokens(r, first, k_cols, 0, r.shape[2]) for r in lat_refs[d]))

    _chunk_scan(q_s, k_s, v_s, lf_s, st_s, (of_ref, ob_ref), tt=tt, ch=ch, heads=heads, dk=dk, dv=dv)


def _chunk_scratch(tt, heads, dk, dv):
    return [pltpu.VMEM((2, tt, heads * dk), F32), pltpu.VMEM((2, tt, heads * dk), F32),
            pltpu.VMEM((2, tt, heads * dv), F32), pltpu.VMEM((2, tt, heads * dk), F32),
            pltpu.VMEM((2, heads, dv, dk), F32)]


def _gla(pc, za, wa, ba, nb, seq, ctx, tt, ch, heads, dk, dv):
    t = pc.shape[0]
    geo = (nb, seq, ctx, tt)
    wide = 2 * heads * dk + heads * dv
    full = lambda shape: pl.BlockSpec(shape, lambda b, i: (0,) * len(shape))
    return pl.pallas_call(
        functools.partial(_gla_body, tt=tt, ch=ch, heads=heads, dk=dk, dv=dv),
        grid=(nb, (seq + ctx) // tt),
        in_specs=[_seq_spec(wide, 0, False, *geo), _seq_spec(za.shape[1], 0, False, *geo),
                  _seq_spec(wide, 0, True, *geo), _seq_spec(za.shape[1], 0, True, *geo),
                  full(wa.shape), full(ba.shape)],
        out_specs=[_seq_spec(heads * dv, 0, False, *geo), _seq_spec(heads * dv, 0, True, *geo)],
        out_shape=[jax.ShapeDtypeStruct((t, heads * dv), F32)] * 2,
        scratch_shapes=_chunk_scratch(tt, heads, dk, dv),
        compiler_params=_params(2),
        name="gla_scan",
    )(pc, za, pc, za, wa, ba)


def _hgrn(pd, lb, nb, seq, ctx, tt, ch, heads, dk, dv):
    t = pd.shape[0]
    geo = (nb, seq, ctx, tt)
    hk = heads * dk
    lat = lambda d: [_col_lat_spec(hk, blk, d == 1, seq, ctx, tt) for blk in (d, 2, 3)]
    cx = lambda d: [_ctx_spec(hk, blk, d == 1, *geo) for blk in (d, 2, 3)]
    pv = _grid_view(pd)
    return pl.pallas_call(
        functools.partial(_hgrn_body, tt=tt, ch=ch, heads=heads, dk=dk, dv=dv, seq=seq, ctx=ctx),
        grid=(nb, (seq + ctx) // tt),
        in_specs=lat(0) + cx(0) + lat(1) + cx(1) + [pl.BlockSpec(lb.shape, lambda b, i: (0, 0))],
        out_specs=[_seq_spec(heads * dv, 0, False, *geo), _seq_spec(heads * dv, 0, True, *geo)],
        out_shape=[jax.ShapeDtypeStruct((t, heads * dv), F32)] * 2,
        scratch_shapes=_chunk_scratch(tt, heads, dk, dv),
        compiler_params=_params(2),
        name="hgrn_scan",
    )(*([pv] * 3 + [pd] * 3) * 2, lb)


def _headnorm_body(of_ref, ob_ref, *rest, heads, dv, nlat):
    if nlat is None:
        g_ref, n_ref, o_ref = rest
        gate = g_ref[...]
    else:
        gl_ref, gc_ref, n_ref, o_ref, g_s = rest
        _load_tile(g_s, gl_ref, gc_ref, pl.program_id(0) >= nlat)
        gate = g_s[...]
    o = of_ref[...] + ob_ref[...]
    parts = [_rms(o[:, h * dv:(h + 1) * dv]) for h in range(heads)]
    y = jnp.concatenate(parts, axis=-1) * n_ref[...]
    o_ref[...] = (y * jax.nn.silu(gate)).astype(o_ref.dtype)


def _headnorm(of, ob, gate_arr, gate_blk, gain, l, heads, tm, col_major=None):
    t, c = of.shape
    row = pl.BlockSpec((tm, c), lambda i: (i, 0))
    if col_major is None:
        gate_specs, gate_args, scratch, nlat = [pl.BlockSpec((tm, c), lambda i: (i, gate_blk))], [gate_arr], [], None
    else:
        n_lat, seq = col_major
        gate_specs = list(_tile_specs(c, gate_blk, n_lat, seq, tm))
        gate_args, scratch, nlat = [_grid_view(gate_arr), gate_arr], [pltpu.VMEM((tm, c), F32)], n_lat // tm
    return pl.pallas_call(
        functools.partial(_headnorm_body, heads=heads, dv=c // heads, nlat=nlat),
        grid=(t // tm,),
        in_specs=[row, row] + gate_specs + [pl.BlockSpec((None, 1, c), lambda i: (l, 0, 0))],
        out_specs=row,
        out_shape=jax.ShapeDtypeStruct((t, c), BF16),
        scratch_shapes=scratch,
        compiler_params=_params(1),
        name="headnorm_gate",
    )(of, ob, *gate_args, gain)


def _merge_body(ya_ref, yb_ref, yc_ref, yd_ref, g0_ref, g1_ref, g2_ref, g3_ref, w_ref, o_ref, wb_ref):
    @pl.when(pl.program_id(1) == 0)
    def _():
        wb_ref[...] = w_ref[...].astype(BF16)

    acc = None
    for k, (y_ref, g_ref) in enumerate(((ya_ref, g0_ref), (yb_ref, g1_ref), (yc_ref, g2_ref), (yd_ref, g3_ref))):
        term = jax.nn.sigmoid(g_ref[...]) * jnp.dot(y_ref[...], wb_ref[k], preferred_element_type=F32)
        acc = term if acc is None else acc + term
    o_ref[...] = acc.astype(o_ref.dtype)


def _merge(ys, pg, w_branch, l, m, tm):
    c = ys[0].shape[1]
    d = w_branch.shape[3]
    tn = 512
    nj = d // tn
    y_spec = pl.BlockSpec((tm, c), lambda j, i: (i, 0))
    g_specs = [pl.BlockSpec((tm, tn), functools.partial(lambda j, i, k: (i, k * nj + j), k=k)) for k in range(N_BRANCH)]
    return pl.pallas_call(
        _merge_body,
        grid=(nj, m // tm),
        in_specs=[y_spec] * N_BRANCH + g_specs + [pl.BlockSpec((None, N_BRANCH, c, tn), lambda j, i: (l, 0, 0, j))],
        out_specs=pl.BlockSpec((tm, tn), lambda j, i: (i, j)),
        out_shape=jax.ShapeDtypeStruct((m, d), BF16),
        scratch_shapes=[pltpu.VMEM((N_BRANCH, c, tn), BF16)],
        compiler_params=_params(2),
        name="branch_merge",
    )(*ys, pg, pg, pg, pg, w_branch)


def _resid_body(x_ref, y_ref, g_ref, mod_ref, xo_ref, *f_refs, gate_i, gy, gf, shift_i, scale_i, y_transposed):
    y = y_ref[...].T if y_transposed else y_ref[...]
    xn = x_ref[...] + mod_ref[gate_i:gate_i + 1, :] * (_rms(y) * g_ref[gy:gy + 1, :])
    xo_ref[...] = xn
    if f_refs:
        f = _rms(xn) * g_ref[gf:gf + 1, :]
        f = f * (1.0 + mod_ref[scale_i:scale_i + 1, :]) + mod_ref[shift_i:shift_i + 1, :]
        f_refs[0][...] = f.astype(BF16)
        f_refs[1][...] = f.T.astype(BF16)


def _resid(x, y, gain, mod, l, m, seq, nb, tm, gate_i, gy, with_f, y_transposed):
    d = x.shape[1]
    row = pl.BlockSpec((tm, d), lambda i: (i, 0))
    col = pl.BlockSpec((d, tm), lambda i: (0, i))
    out_shape = [jax.ShapeDtypeStruct((m, d), F32)]
    out_specs = [row]
    if with_f:
        out_shape += [jax.ShapeDtypeStruct((m, d), BF16), jax.ShapeDtypeStruct((d, m), BF16)]
        out_specs += [row, col]
    return pl.pallas_call(
        functools.partial(_resid_body, gate_i=gate_i, gy=gy, gf=2, shift_i=3, scale_i=4, y_transposed=y_transposed),
        grid=(m // tm,),
        in_specs=[row, col if y_transposed else row, pl.BlockSpec((None, 4, d), lambda i: (l, 0, 0)),
                  pl.BlockSpec((None, N_MOD, d), lambda i: (_mod_row(i, tm, seq, nb), 0, 0))],
        out_specs=out_specs,
        out_shape=out_shape,
        compiler_params=_params(1),
        name="residual",
    )(x, y, gain, mod)


def _top_rows(x, vals_ref, idx_ref, want_rank):
    n_rows = x.shape[0]
    iota = lax.broadcasted_iota(jnp.int32, x.shape, 0)

    def step(r, carry):
        x = carry[0]
        mx = jnp.max(x, axis=0, keepdims=True)
        vals_ref[pl.ds(r, 1), :] = mx
        first = jnp.min(jnp.where(x == mx, iota, n_rows), axis=0, keepdims=True)
        idx_ref[pl.ds(r, 1), :] = first
        hit = iota == first
        x = jnp.where(hit, -jnp.inf, x)
        return (x, jnp.where(hit, jnp.asarray(r, F32), carry[1])) if want_rank else (x,)

    init = (x, jnp.full(x.shape, NO_RANK, F32)) if want_rank else (x,)
    return lax.fori_loop(0, PEER_TOPK, step, init)[-1]


def _top_rows_distinct(x, vals_ref, want_rank):
    def step(r, carry):
        x = carry[0]
        mx = jnp.max(x, axis=0, keepdims=True)
        vals_ref[pl.ds(r, 1), :] = mx
        hit = x == mx
        x = jnp.where(hit, -jnp.inf, x)
        return (x, jnp.where(hit, jnp.asarray(r, F32), carry[1])) if want_rank else (x,)

    init = (x, jnp.full(x.shape, NO_RANK, F32)) if want_rank else (x,)
    out = lax.fori_loop(0, PEER_TOPK, step, init)
    removed = jnp.sum((out[0] == -jnp.inf).astype(F32), axis=0, keepdims=True)
    return removed != float(PEER_TOPK), out[-1]


def _pair_rows(a, b):
    half = PEER_TOPK // 2
    return jnp.concatenate([a[0:1, :] + b] + [a[r:r + 1, :] + b[0:half, :] for r in range(1, half)]
                           + [a[half:, :] + b[0:1, :]], axis=0)


def _route_body(q_ref, keys_ref, e1_ref, cnt_ref, rank2_ref, e2_ref, va_s, vb_s, vc_s, ia_s, ib_s, *, dq):
    half = PEER_TOPK // 2
    iota = lax.broadcasted_iota(jnp.int32, (N_KEYS, q_ref.shape[0]), 0)
    for h in range(PEER_HEADS):
        scores = []
        for p in range(2):
            qh = q_ref[:, h * 2 * dq + p * dq:h * 2 * dq + (p + 1) * dq].astype(BF16)
            scores.append(lax.dot_general(keys_ref[h, p].astype(BF16), qh, NT_DIMS, preferred_element_type=F32))

        def emit(a, b, best, cnt, rank2):
            z = jnp.sum(jnp.exp(best - best[0:1, :]), axis=0, keepdims=True)
            e1_ref[h] = jnp.exp(scores[0] - a[0:1, :]) / z
            cnt_ref[h] = cnt
            rank2_ref[h] = rank2.astype(BF16)
            e2_ref[h] = jnp.exp(scores[1] - b[0:1, :]).astype(BF16)

        tied_a, _ = _top_rows_distinct(scores[0], va_s, False)
        tied_b, _ = _top_rows_distinct(scores[1], vb_s, False)
        a, b = va_s[...], vb_s[...]
        rank2 = jnp.full(scores[1].shape, NO_RANK, F32)
        for r in range(PEER_TOPK):
            rank2 = jnp.where(scores[1] == b[r:r + 1, :], float(r), rank2)
        cand = _pair_rows(a, b)
        tied_c, _ = _top_rows_distinct(cand, vc_s, False)
        best = vc_s[...]
        keep = (cand >= best[PEER_TOPK - 1:PEER_TOPK, :]).astype(F32)
        groups = [(0, PEER_TOPK)] + [(PEER_TOPK + half * (r - 1), PEER_TOPK + half * r) for r in range(1, half)] \
            + [(PEER_TOPK + half * (half - 1) + r, PEER_TOPK + half * (half - 1) + r + 1) for r in range(half)]
        cnt = jnp.zeros(scores[0].shape, F32)
        for i, (lo, hi) in enumerate(groups):
            n_i = jnp.sum(keep[lo:hi, :], axis=0, keepdims=True)
            cnt = jnp.where(scores[0] == a[i:i + 1, :], n_i, cnt)
        emit(a, b, best, cnt, rank2)
        tied = jnp.max((tied_a | tied_b | tied_c).astype(F32)) > 0.0

        @pl.when(tied)
        def _():
            _top_rows(scores[0], va_s, ia_s, False)
            rank2 = _top_rows(scores[1], vb_s, ib_s, True)
            a, b = va_s[...], vb_s[...]
            _top_rows(_pair_rows(a, b), vc_s, ib_s, False)
            row = ib_s[...]
            sel_i = jnp.where(row < PEER_TOPK, 0,
                              jnp.where(row < PEER_TOPK + half * (half - 1), (row - half) // half, row - half * half))
            key_a = ia_s[...]
            cnt = jnp.zeros(scores[0].shape, F32)
            for i in range(PEER_TOPK):
                n_i = jnp.sum((sel_i == i).astype(F32), axis=0, keepdims=True)
                cnt = jnp.where(iota == key_a[i:i + 1, :], n_i, cnt)
            emit(a, b, vc_s[...], cnt, rank2)


def _route(q, keys, l, tm):
    t = q.shape[0]
    dq = keys.shape[-1]
    spec = pl.BlockSpec((PEER_HEADS, N_KEYS, tm), lambda i: (0, 0, i))
    return pl.pallas_call(
        functools.partial(_route_body, dq=dq),
        grid=(t // tm,),
        in_specs=[pl.BlockSpec((tm, q.shape[1]), lambda i: (i, 0)),
                  pl.BlockSpec((None,) + keys.shape[1:], lambda i: (l, 0, 0, 0, 0))],
        out_specs=[spec] * 4,
        out_shape=[jax.ShapeDtypeStruct((PEER_HEADS, N_KEYS, t), dt) for dt in (F32, F32, BF16, BF16)],
        scratch_shapes=[pltpu.VMEM((PEER_TOPK, tm), F32)] * 3 + [pltpu.VMEM((PEER_TOPK, tm), jnp.int32)] * 2,
        compiler_params=_params(1),
        name="peer_route",
    )(q, keys)


def _peer_body(ft_ref, u_ref, vt_ref, e1_ref, cnt_ref, rank2_ref, e2_ref, o_ref, g_s, act_s, *, te, nj):
    j = pl.program_id(1)

    @pl.when(j == 0)
    def _():
        o_ref[...] = jnp.zeros_like(o_ref)
        act_s[...] = jnp.zeros_like(act_s)

    tile = jnp.maximum(j - 1, 0)
    tm = act_s.shape[1]
    zero = jnp.zeros((), BF16)
    for r in range(te // N_KEYS):
        i1 = tile * (te // N_KEYS) + r
        rows = slice(r * N_KEYS, (r + 1) * N_KEYS)
        e1_rows = [e1_ref[h, pl.ds(i1, 1), :].astype(BF16) for h in range(PEER_HEADS)]
        cnt_rows = [cnt_ref[h, pl.ds(i1, 1), :].astype(BF16) for h in range(PEER_HEADS)]
        for cb in range(tm // LANES):
            cs = slice(cb * LANES, (cb + 1) * LANES)
            w = None
            for h in range(PEER_HEADS):
                term = jnp.where(rank2_ref[h, :, cs] < cnt_rows[h][:, cs], e2_ref[h, :, cs] * e1_rows[h][:, cs], zero)
                w = term if w is None else w + term
            g_s[rows, cs] = w * act_s[rows, cs]

    o_ref[...] += jnp.dot(vt_ref[...], g_s[...], preferred_element_type=F32)
    act_s[...] = jax.nn.gelu(jnp.dot(u_ref[...], ft_ref[...], preferred_element_type=F32)).astype(BF16)


def _peer(ft, u, vt, l, e1, cnt, rank2, e2, tm, te):
    d, t = ft.shape
    nj = u.shape[1] // te
    once = pl.Buffered(1)
    big_spec = pl.BlockSpec((PEER_HEADS, N_KEYS, tm), lambda i, j: (0, 0, i), pipeline_mode=once)
    return pl.pallas_call(
        functools.partial(_peer_body, te=te, nj=nj),
        grid=(t // tm, nj + 1),
        in_specs=[pl.BlockSpec((d, tm), lambda i, j: (0, i), pipeline_mode=once),
                  pl.BlockSpec((None, te, d), lambda i, j: (l, jnp.minimum(j, nj - 1), 0)),
                  pl.BlockSpec((None, d, te), lambda i, j: (l, 0, jnp.maximum(j - 1, 0))),
                  big_spec, big_spec, big_spec, big_spec],
        out_specs=pl.BlockSpec((d, tm), lambda i, j: (0, i)),
        out_shape=jax.ShapeDtypeStruct((d, t), F32),
        scratch_shapes=[pltpu.VMEM((te, tm), BF16), pltpu.VMEM((te, tm), BF16)],
        compiler_params=_params(2),
        name="peer_experts",
    )(ft, u, vt, e1, cnt, rank2, e2)


def _block_diag(w):
    *lead, n, r, c = w.shape
    eye = jnp.eye(n, dtype=w.dtype)
    return (w[..., :, :, None, :] * eye[:, None, :, None]).reshape(*lead, n * r, n * c)


def _lru_params(w_a, b_a, w_i, b_i, lam):
    wd = jnp.concatenate([_block_diag(w_a.astype(BF16)), _block_diag(w_i.astype(BF16))], axis=-1)
    bd = jnp.concatenate([b_a, b_i], axis=-1)[..., None, :]
    sp = jax.nn.softplus(-lam)[..., None, :]
    return wd, bd, sp


def _s5_params(a_re, a_im, log_dt, b_re, b_im, c_re, c_im):
    lead = a_re.shape[:2]
    g, p = a_re.shape[2:]
    gc = g // S5_CHUNKS
    big_a = lax.complex(a_re, a_im)
    adt = big_a * jnp.exp(log_dt)[..., None]
    a_bar = jnp.exp(adt)
    b_bar = ((a_bar - 1.0) / big_a)[..., None] * lax.complex(b_re, b_im)
    c_mat = lax.complex(c_re, c_im)
    bm = jnp.swapaxes(b_bar, -1, -2).reshape(*lead, S5_CHUNKS, gc, S5_GROUP, p)
    cm = jnp.swapaxes(c_mat, -1, -2).reshape(*lead, S5_CHUNKS, gc, p, S5_GROUP)
    bre, bim, cre, cim = (_block_diag(part(m).astype(BF16)) for m in (bm, cm) for part in (jnp.real, jnp.imag))
    steps = jnp.arange(1, SUBLANES + 1, dtype=F32)[:, None, None]
    pows = jnp.exp(adt[:, :, None] * steps).reshape(*lead, SUBLANES, g * p)
    row = jnp.arange(SUBLANES)
    shifts = jnp.array([1, 2, 4])
    ok = jnp.stack([row[None, :] >= shifts[:, None], row[None, :] < SUBLANES - shifts[:, None]])
    dbl = jnp.where(ok[None, :, :, :, None], pows[:, :, shifts - 1][:, :, :, None, :], 0.0)
    pw = jnp.stack([pows[:, 0], pows[:, 1, ::-1]], axis=1)
    split = lambda z: jnp.stack([jnp.real(z), jnp.imag(z)], axis=2)
    return bre, bim, cre, cim, split(dbl), split(pw)


def _to_row_major(t, nb, seq):
    rows = seq // GRID_W
    lat = t[:nb * seq].reshape(nb, GRID_W, rows, -1).transpose(0, 2, 1, 3).reshape(nb * seq, -1)
    return jnp.concatenate([lat, t[nb * seq:]], axis=0)


def kernel(x, c, ctx, c_ctx, w_ada, b_ada, norm_gain, w_in, lru_conv_w, lru_conv_b, lru_w_a, lru_b_a, lru_w_i,
           lru_b_i, lru_lambda, s5_a_re, s5_a_im, s5_log_dt, s5_b_re, s5_b_im, s5_c_re, s5_c_im, s5_d, s5_w_glu,
           gla_w_alpha, gla_b_alpha, gla_norm, hgrn_lb_logits, hgrn_norm, w_branch, w_out, peer_w_q, peer_keys,
           peer_u, peer_v):
    nb, seq, d = x.shape
    nctx = ctx.shape[1]
    depth = w_ada.shape[0]
    mix = d // 4
    n_lat = nb * seq
    n_tok = n_lat + nb * nctx
    assert nb + 1 <= SUBLANES and seq % GRID_W == 0
    gla_dk, gla_dv = mix // 2 // GLA_HEADS, mix // GLA_HEADS
    hg_dk = hg_dv = mix // HGRN_HEADS
    gla_rank = gla_w_alpha.shape[2]
    tm = _tile(math.gcd(seq, nb * nctx), (512, 256, 128))
    tr = min(tm, 256)
    tw = min(tm, 256)
    tt = _tile(math.gcd(seq, nctx), (256, 128, 64, 32))
    ch = 64

    w_in_t = jnp.swapaxes(w_in, 1, 2)
    o_c = 3 * mix
    o_ca = o_c + 2 * GLA_HEADS * gla_dk + GLA_HEADS * gla_dv + mix
    o_dg = 2 * HGRN_HEADS * hg_dk + HGRN_HEADS * hg_dk + HGRN_HEADS * hg_dv
    o_gt = o_dg + mix
    za_w = 128

    p_lb = jax.nn.softmax(hgrn_lb_logits.astype(F32), axis=0)
    lower = jnp.cumsum(p_lb, axis=0) - p_lb[0]
    cvec = jnp.zeros((SUBLANES, d), F32).at[:nb].set(c).at[nb].set(c_ctx)
    b_ada3 = b_ada[:, None, :]
    u_tab = peer_u.astype(BF16)
    vt_tab = jnp.swapaxes(peer_v, 1, 2).astype(BF16)

    wd, bd, sp = _lru_params(lru_w_a, lru_b_a, lru_w_i, lru_b_i, lru_lambda)
    s5p = _s5_params(s5_a_re, s5_a_im, s5_log_dt, s5_b_re, s5_b_im, s5_c_re, s5_c_im)

    xs = jnp.concatenate([x.reshape(n_lat, d), ctx.reshape(nb * nctx, d)], axis=0)
    for l in range(depth):
        last = l == depth - 1
        m_out = n_lat if last else n_tok
        mod = _adaln(cvec, w_ada, b_ada3, l).reshape(SUBLANES, N_MOD, d)
        h = _normmod(xs, norm_gain, mod, l, seq, nb, tr)

        mm_in = lambda off, n, shift, tn, name: _mmt(h, w_in_t, l, off, n, shift, tn, n_tok, tw, name)
        pa = mm_in(0, 2 * mix, 0, mix, "proj_lru")
        pb = mm_in(2 * mix, mix, 0, mix, "proj_s5")
        pc = mm_in(o_c, o_ca - o_c - mix, 0, mix, "proj_gla")
        pcg = mm_in(o_ca - mix, mix, 0, mix, "proj_gla_gate")
        za = mm_in(o_ca, za_w, 0, za_w, "proj_gla_rank")
        pd = mm_in(o_ca, o_dg, gla_rank, mix, "proj_hgrn")
        pdg = mm_in(o_ca + o_dg, mix, gla_rank, mix, "proj_hgrn_gate")
        pg = mm_in(o_ca + o_gt, N_BRANCH * d, gla_rank, mix, "proj_gate")

        hf, hb = _lru(pa, lru_conv_w, lru_conv_b[:, None, :], wd, bd, sp, l, nb, seq, nctx, tt)
        ya = _lru_out(pa, hf, hb, tm)

        yf, ybk = _s5(pb, *s5p, l, nb, seq, nctx, tt)
        yb = _to_row_major(_s5_glu(pb, yf, ybk, s5_d[:, None, :], s5_w_glu, l, n_lat, seq, tm), nb, seq)

        wa = jnp.zeros((2, za_w, GLA_HEADS * gla_dk), F32).at[:, :gla_rank].set(gla_w_alpha[l]).astype(BF16)
        of, ob = _gla(pc, za, wa, gla_b_alpha[l][:, None, :], nb, seq, nctx, tt, min(2 * ch, tt), GLA_HEADS, gla_dk, gla_dv)
        yc = _headnorm(of, ob, pcg, 0, gla_norm[:, None, :], l, GLA_HEADS, tm)

        of, ob = _hgrn(pd, lower[l][None, :], nb, seq, nctx, tt, ch, HGRN_HEADS, hg_dk, hg_dv)
        yd = _to_row_major(_headnorm(of, ob, pdg, 0, hgrn_norm[:, None, :], l, HGRN_HEADS, tm, (n_lat, seq)), nb, seq)

        zm = _merge((ya, yb, yc, yd), pg, w_branch, l, m_out, tm)
        mo = _mm(zm, w_out, l, d, F32, m_out, tw, mix, "proj_out")
        xs, f, ft = _resid(xs, mo, norm_gain, mod, l, m_out, seq, nb, tr, 2, 1, True, False)

        q = _mm(f, peer_w_q, l, peer_w_q.shape[2], F32, m_out, tw, mix, "peer_query")
        e1, cnt, rank2, e2 = _route(q, peer_keys, l, _tile(m_out, (512, 256, 128)))
        yt = _peer(ft, u_tab, vt_tab, l, e1, cnt, rank2, e2, _tile(m_out, (512, 256, 128)), 512)
        xs = _resid(xs, yt, norm_gain, mod, l, m_out, seq, nb, tr, 5, 3, False, True)[0]
    return xs[:n_lat].reshape(nb, seq, d)
```
